```python
import math
import jax, jax.numpy as jnp
from jax import lax
import numpy as np

D_MODEL = 1024
BATCH = 4
SEQ = 4096
DEPTH = 1

DA_HEADS = 8
DA_HD = 64
DA_QK = DA_HEADS * 2 * DA_HD
DA_V = DA_HEADS * 2 * DA_HD
Q_BLOCK = 128
NUM_BUCKETS = 32
MAX_DISTANCE = 128
RW_HD = 64
RW_HEADS = D_MODEL // RW_HD
RW_W = RW_HEADS * RW_HD
DECAY_LORA = max(32, int(round(1.8 * D_MODEL ** 0.5 / 32)) * 32)
AAA_LORA = max(32, int(round(1.8 * D_MODEL ** 0.5 / 32)) * 32)
GATE_LORA = max(32, int(round(0.6 * D_MODEL ** 0.8 / 32)) * 32)
RW_LNX_EPS = 64e-5
RW_SHIFT_COLS = 3 * RW_W + DECAY_LORA + AAA_LORA + GATE_LORA
IN_SPLITS = (DA_QK, DA_QK, DA_V, RW_SHIFT_COLS, 2 * D_MODEL)
IN_COLS = sum(IN_SPLITS)
MEM_TOKENS = 256
CA_HEADS = 4
CA_HD = D_MODEL // CA_HEADS
D_FF = -(-8 * D_MODEL // (3 * 256)) * 256
DN_ALPHA = (2.0 * DEPTH) ** 0.25
DN_BETA = (8.0 * DEPTH) ** -0.25
LN_EPS = 1e-5

kernel_name = "hybrid_diffattn_rwkv7_gated_deepnorm"


def _split_cols(p, sizes):
    outs, off = [], 0
    for s in sizes:
        outs.append(p[..., off:off + s])
        off += s
    return outs


def _layer_norm(x, g, b, eps=LN_EPS):
    xf = x.astype(jnp.float32)
    mu = jnp.mean(xf, axis=-1, keepdims=True)
    var = jnp.mean(jnp.square(xf - mu), axis=-1, keepdims=True)
    return ((xf - mu) * lax.rsqrt(var + eps) * g + b).astype(x.dtype)


def _rms_norm(x, g, eps=LN_EPS):
    xf = x.astype(jnp.float32)
    ms = jnp.mean(jnp.square(xf), axis=-1, keepdims=True)
    return (xf * lax.rsqrt(ms + eps) * g).astype(x.dtype)


def _token_shift(p):
    return jnp.pad(p, ((0, 0), (1, 0), (0, 0)))[:, :-1]


def _t5_bucket(rel):
    n = jnp.maximum(rel, 0)
    max_exact = NUM_BUCKETS // 2
    nf = jnp.maximum(n, 1).astype(jnp.float32)
    large = max_exact + (jnp.log(nf / max_exact) / math.log(MAX_DISTANCE / max_exact)
                         * (NUM_BUCKETS - max_exact)).astype(jnp.int32)
    large = jnp.minimum(large, NUM_BUCKETS - 1)
    return jnp.where(n < max_exact, n, large)


def _diff_attention(q, k, v, lam, bias_table):
    S = q.shape[1]
    scale = DA_HD ** -0.5
    outs = []
    for i in range(S // Q_BLOCK):
        q0, kend = i * Q_BLOCK, (i + 1) * Q_BLOCK
        qb, kb, vb = q[:, q0:kend], k[:, :kend], v[:, :kend]
        rel = jnp.arange(q0, kend)[:, None] - jnp.arange(kend)[None, :]
        bias = jnp.transpose(bias_table[_t5_bucket(rel)], (2, 0, 1)).astype(jnp.float32)
        logits = jnp.einsum('bqhcd,bkhcd->bhcqk', qb, kb).astype(jnp.float32) * scale
        logits = logits + bias[None, :, None]
        logits = jnp.where((rel >= 0)[None, None, None], logits, -jnp.inf)
        p = jax.nn.softmax(logits, axis=-1)
        attn = (p[:, :, 0] - lam * p[:, :, 1]).astype(v.dtype)
        outs.append(jnp.einsum('bhqk,bkhd->bqhd', attn, vb))
    return jnp.concatenate(outs, axis=1)


def _rwkv7_recurrence(r, decay, k, v, a_vec, b_vec):
    Bn, _, H, N = r.shape

    def step(state, inp):
        r_t, w_t, k_t, v_t, a_t, b_t = inp
        sa = jnp.einsum('bhvk,bhk->bhv', state, a_t)
        state = (state * w_t[:, :, None, :] + sa[..., None] * b_t[:, :, None, :]
                 + v_t[..., None] * k_t[:, :, None, :])
        return state, jnp.einsum('bhvk,bhk->bhv', state, r_t)

    xs = tuple(jnp.moveaxis(t, 1, 0) for t in (r, decay, k, v, a_vec, b_vec))
    state0 = jnp.zeros((Bn, H, N, N), jnp.float32)
    _, ys = lax.scan(step, state0, xs)
    return jnp.moveaxis(ys, 0, 1)


def setup_inputs(seed: int = 0) -> dict:
    key = jax.random.key(seed)
    ks = iter(jax.random.split(key, 48))
    D = D_MODEL

    def nrm(shape, scale):
        return jax.random.normal(next(ks), shape, jnp.float32) * scale

    def gain(shape):
        return 1.0 + nrm(shape, 0.02)

    return {
        'x': nrm((BATCH, SEQ, D), 1.0),
        'mem': nrm((BATCH, MEM_TOKENS, D), 1.0),
        'rel_bias': nrm((NUM_BUCKETS, DA_HEADS), 0.5),
        'w_in': nrm((DEPTH, D, IN_COLS), D ** -0.5),
        'shift_mu': jax.random.uniform(next(ks), (DEPTH, RW_SHIFT_COLS), jnp.float32),
        'lambda_q1': nrm((DEPTH, DA_HD), 0.1),
        'lambda_k1': nrm((DEPTH, DA_HD), 0.1),
        'lambda_q2': nrm((DEPTH, DA_HD), 0.1),
        'lambda_k2': nrm((DEPTH, DA_HD), 0.1),
        'da_subln_g': gain((DEPTH, 2 * DA_HD)),
        'w_da_proj': nrm((DEPTH, DA_V, D), DA_V ** -0.5),
        'rw_w0': jax.random.uniform(next(ks), (DEPTH, RW_W), jnp.float32, -6.0, -1.0),
        'rw_w2': nrm((DEPTH, DECAY_LORA, RW_W), 0.1 * DECAY_LORA ** -0.5),
        'rw_a0': nrm((DEPTH, RW_W), 0.1),
        'rw_a2': nrm((DEPTH, AAA_LORA, RW_W), AAA_LORA ** -0.5),
        'rw_g2': nrm((DEPTH, GATE_LORA, RW_W), GATE_LORA ** -0.5),
        'rw_k_k': 0.85 + nrm((DEPTH, RW_W), 0.05),
        'rw_k_a': 1.0 + nrm((DEPTH, RW_W), 0.05),
        'rw_r_k': nrm((DEPTH, RW_HEADS, RW_HD), 0.1),
        'rw_lnx_g': gain((DEPTH, RW_W)),
        'rw_lnx_b': nrm((DEPTH, RW_W), 0.02),
        'w_rw_proj': nrm((DEPTH, RW_W, D), RW_W ** -0.5),
        'w_mix_out': nrm((DEPTH, D, D), DN_BETA * D ** -0.5),
        'ln1_g': gain((DEPTH, D)),
        'ln1_b': nrm((DEPTH, D), 0.02),
        'mem_ln_g': gain((DEPTH, D)),
        'mem_ln_b': nrm((DEPTH, D), 0.02),
        'w_cq': nrm((DEPTH, D, D), D ** -0.5),
        'w_ckv': nrm((DEPTH, D, 2 * D), D ** -0.5),
        'w_co': nrm((DEPTH, D, D), DN_BETA * D ** -0.5),
        'ln2_g': gain((DEPTH, D)),
        'ln2_b': nrm((DEPTH, D), 0.02),
        'w_ffn_in': nrm((DEPTH, D, 2 * D_FF), D ** -0.5),
        'w_ffn_out': nrm((DEPTH, D_FF, D), DN_BETA * D_FF ** -0.5),
        'ln3_g': gain((DEPTH, D)),
        'ln3_b': nrm((DEPTH, D), 0.02),
    }


def reference(x, mem, rel_bias, w_in, shift_mu, lambda_q1, lambda_k1, lambda_q2, lambda_k2,
              da_subln_g, w_da_proj, rw_w0, rw_w2, rw_a0, rw_a2, rw_g2, rw_k_k, rw_k_a, rw_r_k,
              rw_lnx_g, rw_lnx_b, w_rw_proj, w_mix_out, ln1_g, ln1_b, mem_ln_g, mem_ln_b,
              w_cq, w_ckv, w_co, ln2_g, ln2_b, w_ffn_in, w_ffn_out, ln3_g, ln3_b):
    f32 = jnp.float32
    B, S, D = x.shape
    M = mem.shape[1]

    def heads(t):
        return t.reshape(B, S, RW_HEADS, RW_HD)

    for l in range(DEPTH):
        proj = x @ w_in[l]
        da_q, da_k, da_v, rw_cols, gate_cols = _split_cols(proj, IN_SPLITS)

        lam_init = 0.8 - 0.6 * math.exp(-0.3 * l)
        lam = (jnp.exp(jnp.sum(lambda_q1[l].astype(f32) * lambda_k1[l].astype(f32)))
               - jnp.exp(jnp.sum(lambda_q2[l].astype(f32) * lambda_k2[l].astype(f32))) + lam_init)
        o = _diff_attention(da_q.reshape(B, S, DA_HEADS, 2, DA_HD),
                            da_k.reshape(B, S, DA_HEADS, 2, DA_HD),
                            da_v.reshape(B, S, DA_HEADS, 2 * DA_HD), lam, rel_bias)
        o = _rms_norm(o, da_subln_g[l]) * (1.0 - lam_init)
        y_da = o.reshape(B, S, DA_V) @ w_da_proj[l]

        rw_cols = rw_cols + shift_mu[l] * (_token_shift(rw_cols) - rw_cols)
        rw_r, rw_k, rw_v, rw_wd, rw_ad, rw_gd = _split_cols(
            rw_cols, (RW_W, RW_W, RW_W, DECAY_LORA, AAA_LORA, GATE_LORA))
        w_log = -jax.nn.softplus(-(rw_w0[l] + jnp.tanh(rw_wd) @ rw_w2[l]).astype(f32)) - 0.5
        decay = jnp.exp(-jnp.exp(w_log))
        a_ic = jax.nn.sigmoid((rw_a0[l] + rw_ad @ rw_a2[l]).astype(f32))
        g_out = jax.nn.sigmoid(rw_gd) @ rw_g2[l]
        kf = rw_k.astype(f32)
        kk = heads(kf * rw_k_k[l])
        kk = kk / jnp.maximum(jnp.sqrt(jnp.sum(jnp.square(kk), axis=-1, keepdims=True)), 1e-12)
        a_h = heads(a_ic)
        k_h = heads(kf * (1.0 + (a_ic - 1.0) * rw_k_a[l]))
        r_h = heads(rw_r.astype(f32))
        v_h = heads(rw_v.astype(f32))
        y = _rwkv7_recurrence(r_h, heads(decay), k_h, v_h, -kk, kk * a_h)
        mu = jnp.mean(y, axis=-1, keepdims=True)
        var = jnp.mean(jnp.square(y - mu), axis=-1, keepdims=True)
        y = ((y - mu) * lax.rsqrt(var + RW_LNX_EPS)).reshape(B, S, RW_W) * rw_lnx_g[l] + rw_lnx_b[l]
        bonus = jnp.sum(r_h * k_h * rw_r_k[l], axis=-1, keepdims=True) * v_h
        y = (y + bonus.reshape(B, S, RW_W)).astype(x.dtype)
        y_rw = (y * g_out) @ w_rw_proj[l]

        gate_a, gate_b = _split_cols(gate_cols, (D, D))
        mixed = jax.nn.sigmoid(gate_a) * y_da + jax.nn.sigmoid(gate_b) * y_rw
        x = _layer_norm(DN_ALPHA * x + mixed @ w_mix_out[l], ln1_g[l], ln1_b[l])

        m = _layer_norm(mem, mem_ln_g[l], mem_ln_b[l])
        cq = (x @ w_cq[l]).reshape(B, S, CA_HEADS, CA_HD)
        ck, cv = _split_cols(m @ w_ckv[l], (D, D))
        ck = ck.reshape(B, M, CA_HEADS, CA_HD)
        cv = cv.reshape(B, M, CA_HEADS, CA_HD)
        logits = jnp.einsum('bqhd,bmhd->bhqm', cq, ck).astype(f32) * (CA_HD ** -0.5)
        p = jax.nn.softmax(logits, axis=-1).astype(x.dtype)
        co = jnp.einsum('bhqm,bmhd->bqhd', p, cv).reshape(B, S, D) @ w_co[l]
        x = _layer_norm(DN_ALPHA * x + co, ln2_g[l], ln2_b[l])

        gate, up = _split_cols(x @ w_ffn_in[l], (D_FF, D_FF))
        f = (jax.nn.silu(gate) * up) @ w_ffn_out[l]
        x = _layer_norm(DN_ALPHA * x + f, ln3_g[l], ln3_b[l])
    return x
```

```python
import functools
import math

import jax
import jax.numpy as jnp
from jax import lax
from jax.experimental import pallas as pl
from jax.experimental.pallas import tpu as pltpu

F32 = jnp.float32
BF16 = jnp.bfloat16

D_MODEL = 1024
DA_HEADS = 8
DA_HD = 64
NUM_BUCKETS = 32
MAX_DISTANCE = 128
RW_HD = 64
RW_HEADS = D_MODEL // RW_HD
DECAY_LORA = 64
AAA_LORA = 64
GATE_LORA = 160
RW_LNX_EPS = 64e-5
MEM_TOKENS = 256
CA_HEADS = 4
CA_HD = D_MODEL // CA_HEADS
D_FF = 2816
LN_EPS = 1e-5
NEG_BIG = -1e30

ATT_TILE = 256
RW_CHUNK = 64
VMEM_LIMIT = 56 * 1024 * 1024

NT_DIMS = (((1,), (1,)), ((), ()))
TN_DIMS = (((0,), (0,)), ((), ()))


def _cparams(sem, vmem=VMEM_LIMIT):
    return pltpu.CompilerParams(dimension_semantics=sem, vmem_limit_bytes=vmem)


def _split2(a):
    hi = a.astype(BF16)
    lo = (a - hi.astype(F32)).astype(BF16)
    return hi, lo


def _dot3(a, b, dims=(((1,), (0,)), ((), ()))):
    ah, al = _split2(a)
    bh, bl = _split2(b)
    d = functools.partial(lax.dot_general, dimension_numbers=dims, preferred_element_type=F32)
    return d(ah, bh) + (d(ah, bl) + d(al, bh))


def _layer_norm_rows(z, g, b):
    mu = jnp.mean(z, axis=-1, keepdims=True)
    zc = z - mu
    var = jnp.mean(zc * zc, axis=-1, keepdims=True)
    return zc * lax.rsqrt(var + LN_EPS) * g + b


def _mm_kernel(x_ref, w_ref, o_ref, xb_ref):
    @pl.when(pl.program_id(1) == 0)
    def _():
        xb_ref[...] = x_ref[...].astype(BF16)

    o_ref[...] = jnp.dot(xb_ref[...], w_ref[...], preferred_element_type=F32).astype(o_ref.dtype)


def _matmul(x, w, out_dtype, tm, tn, name):
    t, k = x.shape
    n = w.shape[1]
    return pl.pallas_call(
        _mm_kernel,
        grid=(t // tm, n // tn),
        in_specs=[pl.BlockSpec((tm, k), lambda i, j: (i, 0)),
                  pl.BlockSpec((k, tn), lambda i, j: (0, j))],
        out_specs=pl.BlockSpec((tm, tn), lambda i, j: (i, j)),
        out_shape=jax.ShapeDtypeStruct((t, n), out_dtype),
        scratch_shapes=[pltpu.VMEM((tm, k), BF16)],
        compiler_params=_cparams(("parallel", "arbitrary")),
        name=name,
    )(x, w)


def _t5_bucket(rel):
    n = jnp.maximum(rel, 0)
    max_exact = NUM_BUCKETS // 2
    nf = jnp.maximum(n, 1).astype(jnp.float32)
    large = max_exact + (jnp.log(nf / max_exact) / math.log(MAX_DISTANCE / max_exact)
                         * (NUM_BUCKETS - max_exact)).astype(jnp.int32)
    large = jnp.minimum(large, NUM_BUCKETS - 1)
    return jnp.where(n < max_exact, n, large)


def _attn_bias_tiles(rel_bias):
    ta = ATT_TILE
    tab = rel_bias.astype(F32) - rel_bias[NUM_BUCKETS - 1].astype(F32)[None, :]
    rel = jnp.arange(ta)[:, None] - jnp.arange(ta)[None, :]
    diag = jnp.where((rel >= 0)[:, :, None], tab[_t5_bucket(rel)], NEG_BIG)
    sub = tab[_t5_bucket(rel + ta)]
    return jnp.transpose(jnp.stack([sub, diag], axis=0), (3, 0, 1, 2))


def _attn_kernel(lamv_ref, q_ref, k_ref, v_ref, bias_ref, g_ref, o_ref, m_sc, l_sc, acc_sc, *, lam_init):
    ta = ATT_TILE
    qi = pl.program_id(2)

    lv = lamv_ref[...]
    lam = (jnp.exp(jnp.sum(lv[0:1] * lv[1:2], axis=-1, keepdims=True))
           - jnp.exp(jnp.sum(lv[2:3] * lv[3:4], axis=-1, keepdims=True)) + lam_init)

    q = q_ref[0] * jnp.asarray(DA_HD ** -0.5, BF16)
    lane = lax.broadcasted_iota(jnp.int32, q.shape, 1)
    zero = jnp.zeros_like(q)
    qq = jnp.concatenate([jnp.where(lane < DA_HD, q, zero), jnp.where(lane >= DA_HD, q, zero)], axis=0)

    m_sc[...] = jnp.full(m_sc.shape, NEG_BIG, F32)
    l_sc[...] = jnp.zeros(l_sc.shape, F32)
    acc_sc[...] = jnp.zeros(acc_sc.shape, F32)

    def step(j, bias):
        k0 = pl.multiple_of(j * ta, ta)
        k = k_ref[0, pl.ds(k0, ta), :]
        v = v_ref[0, pl.ds(k0, ta), :]
        s = lax.dot_general(qq, k, NT_DIMS, preferred_element_type=F32)
        if bias is not None:
            s = s + jnp.concatenate([bias, bias], axis=0)
        m_prev = m_sc[...]
        m_new = jnp.maximum(m_prev, jnp.max(s, axis=-1, keepdims=True))
        alpha = jnp.exp(m_prev - m_new)
        p = jnp.exp(s - m_new)
        l_sc[...] = alpha * l_sc[...] + jnp.sum(p, axis=-1, keepdims=True)
        acc_sc[...] = alpha * acc_sc[...] + jnp.dot(p.astype(BF16), v, preferred_element_type=F32)
        m_sc[...] = m_new

    def far_body(j, carry):
        step(j, None)
        return carry

    lax.fori_loop(0, qi - 1, far_body, 0)

    @pl.when(qi >= 1)
    def _():
        step(qi - 1, bias_ref[0, 0])

    step(qi, bias_ref[0, 1])

    inv_l = 1.0 / l_sc[...]
    o = acc_sc[0:ta] * inv_l[0:ta] - lam * (acc_sc[ta:2 * ta] * inv_l[ta:2 * ta])
    ms = jnp.mean(o * o, axis=-1, keepdims=True)
    o = o * lax.rsqrt(ms + LN_EPS) * g_ref[...] * (1.0 - lam_init)
    o_ref[0] = o.astype(o_ref.dtype)


def _diff_attention(qkv, lamv, bias_tiles, subln_g, lam_init):
    b, s, _ = qkv.shape
    ta = ATT_TILE
    h = DA_HEADS
    return pl.pallas_call(
        functools.partial(_attn_kernel, lam_init=lam_init),
        grid=(b, h, s // ta),
        in_specs=[
            pl.BlockSpec((4, DA_HD), lambda bi, hi, qi: (0, 0)),
            pl.BlockSpec((1, ta, 128), lambda bi, hi, qi: (bi, qi, hi)),
            pl.BlockSpec((1, s, 128), lambda bi, hi, qi: (bi, 0, h + hi)),
            pl.BlockSpec((1, s, 128), lambda bi, hi, qi: (bi, 0, 2 * h + hi)),
            pl.BlockSpec((1, 2, ta, ta), lambda bi, hi, qi: (hi, 0, 0, 0)),
            pl.BlockSpec((1, 128), lambda bi, hi, qi: (0, 0)),
        ],
        out_specs=pl.BlockSpec((1, ta, 128), lambda bi, hi, qi: (bi, qi, hi)),
        out_shape=jax.ShapeDtypeStruct((b, s, h * 128), BF16),
        scratch_shapes=[pltpu.VMEM((2 * ta, 1), F32), pltpu.VMEM((2 * ta, 1), F32),
                        pltpu.VMEM((2 * ta, 128), F32)],
        compiler_params=_cparams(("parallel", "parallel", "arbitrary")),
        name="diff_attention",
    )(lamv, qkv, qkv, qkv, bias_tiles, subln_g)


def _rw_prep_kernel(r_ref, k_ref, v_ref, lo_ref, rp_ref, kp_ref, vp_ref, lop_ref,
                    mu_ref, mulo_ref, w0_ref, w2h_ref, w2l_ref, a0_ref, a2h_ref, a2l_ref, g2_ref,
                    kk_ref, ka_ref,
                    r_o, lw_o, km_o, v_o, kk_o, a_o, g_o, *, tiles_per_seq):
    i = pl.program_id(0)
    keep = (i % tiles_per_seq != 0).astype(F32)

    def shifted(cur_ref, prev_ref, mu):
        p = cur_ref[...]
        row = lax.broadcasted_iota(jnp.int32, p.shape, 0)
        prev_last = prev_ref[7:8, :] * keep
        pm1 = jnp.where(row == 0, prev_last, pltpu.roll(p, 1, 0))
        return p + mu * (pm1 - p)

    r = shifted(r_ref, rp_ref, mu_ref[0:1])
    k = shifted(k_ref, kp_ref, mu_ref[1:2])
    v = shifted(v_ref, vp_ref, mu_ref[2:3])
    lo = shifted(lo_ref, lop_ref, mulo_ref[...])

    def dot3w(a, wh_ref, wl_ref):
        ah, al = _split2(a)
        wh = wh_ref[...]
        d = functools.partial(jnp.dot, preferred_element_type=F32)
        return d(ah, wh) + (d(ah, wl_ref[...]) + d(al, wh))

    zw = w0_ref[...] + dot3w(jnp.tanh(lo[:, 0:128]), w2h_ref, w2l_ref)
    nz = -zw
    softplus = jnp.maximum(nz, 0.0) + jnp.log(1.0 + jnp.exp(-jnp.abs(nz)))
    w_log = -softplus - 0.5
    lw_o[...] = -jnp.exp(w_log)
    a_ic = jax.nn.sigmoid(a0_ref[...] + dot3w(lo[:, 128:256], a2h_ref, a2l_ref))
    g_o[...] = jnp.dot(jax.nn.sigmoid(lo[:, 256:512]).astype(BF16), g2_ref[...], preferred_element_type=F32)
    r_o[...] = r
    v_o[...] = v
    kk_o[...] = k * kk_ref[...]
    km_o[...] = k * (1.0 + (a_ic - 1.0) * ka_ref[...])
    a_o[...] = a_ic


def _rw_prep(rest, seq, mu_rkv, mu_lo, w0, w2h, w2l, a0, a2h, a2l, g2, k_k, k_a, tm=256):
    t = rest.shape[0]
    d = D_MODEL
    cur = lambda c, w: pl.BlockSpec((tm, w), lambda i: (i, c))
    prev = lambda c, w: pl.BlockSpec((8, w), lambda i: (jnp.maximum(i * (tm // 8) - 1, 0), c))
    full = lambda a: pl.BlockSpec(a.shape, lambda i: (0,) * a.ndim)
    params = (mu_rkv, mu_lo, w0, w2h, w2l, a0, a2h, a2l, g2, k_k, k_a)
    out = jax.ShapeDtypeStruct((t, d), F32)
    return pl.pallas_call(
        functools.partial(_rw_prep_kernel, tiles_per_seq=seq // tm),
        grid=(t // tm,),
        in_specs=[cur(0, d), cur(1, d), cur(2, d), cur(10, 512),
                  prev(0, d), prev(1, d), prev(2, d), prev(10, 512)] + [full(a) for a in params],
        out_specs=[pl.BlockSpec((tm, d), lambda i: (i, 0))] * 7,
        out_shape=[out] * 7,
        compiler_params=_cparams(("parallel",)),
        name="rwkv_prep",
    )(rest, rest, rest, rest, rest, rest, rest, rest, *params)


def _rw_rec_kernel(r_ref, lw_ref, km_ref, v_ref, kk_ref, a_ref, rk_ref, lng_ref, lnb_ref, y_ref, s_sc):
    c = RW_CHUNK
    n = RW_HD
    nchunks = r_ref.shape[1] // c
    s_sc[...] = jnp.zeros(s_sc.shape, F32)

    row = lax.broadcasted_iota(jnp.int32, (c, c), 0)
    col = lax.broadcasted_iota(jnp.int32, (c, c), 1)
    incl = row >= col
    strict = row > col
    tri_ones = incl.astype(BF16)
    eye = (row == col).astype(F32)

    def chunk_body(ci, carry):
        t0 = pl.multiple_of(ci * c, c)
        r2 = r_ref[0, pl.ds(t0, c), :]
        lw2 = lw_ref[0, pl.ds(t0, c), :]
        km2 = km_ref[0, pl.ds(t0, c), :]
        v2 = v_ref[0, pl.ds(t0, c), :]
        kk2 = kk_ref[0, pl.ds(t0, c), :]
        a2 = a_ref[0, pl.ds(t0, c), :]
        ys = []
        for hh in range(2):
            sl = slice(hh * n, (hh + 1) * n)
            r, lw, km, v, kk, a_ic = r2[:, sl], lw2[:, sl], km2[:, sl], v2[:, sl], kk2[:, sl], a2[:, sl]
            l1 = lw.astype(BF16)
            rem = lw - l1.astype(F32)
            l2 = rem.astype(BF16)
            l3 = (rem - l2.astype(F32)).astype(BF16)
            d = functools.partial(jnp.dot, preferred_element_type=F32)
            cum = d(tri_ones, l1) + (d(tri_ones, l2) + d(tri_ones, l3))
            cum_last = cum[c - 1:c, :]
            g_in = jnp.exp(cum)
            g_prev = jnp.exp(cum - lw)
            g_inv = jnp.exp(-cum)
            g_rest = jnp.exp(cum_last - cum)
            nrm = jnp.sqrt(jnp.sum(kk * kk, axis=-1, keepdims=True))
            kkn = kk / jnp.maximum(nrm, 1e-12)
            b = kkn * a_ic
            at = -kkn * g_prev
            bt = b * g_inv
            kt = km * g_inv
            rt = r * g_in
            big = _dot3(jnp.concatenate([at, rt], axis=0), jnp.concatenate([bt, kt], axis=0), NT_DIMS)
            m_ab = jnp.where(strict, big[0:c, 0:c], 0.0)
            m_ak = jnp.where(strict, big[0:c, c:2 * c], 0.0)
            m_rb = jnp.where(incl, big[c:2 * c, 0:c], 0.0)
            m_rk = jnp.where(incl, big[c:2 * c, c:2 * c], 0.0)
            tinv = eye + m_ab
            pw = m_ab
            for _ in range(5):
                pw = _dot3(pw, pw)
                tinv = tinv + _dot3(tinv, pw)
            s0 = s_sc[hh]
            u = _dot3(tinv, _dot3(at, s0, NT_DIMS) + _dot3(m_ak, v))
            y = _dot3(rt, s0, NT_DIMS) + _dot3(m_rb, u) + _dot3(m_rk, v)
            s_sc[hh] = (s0 * jnp.exp(cum_last)
                        + _dot3(u, b * g_rest, TN_DIMS) + _dot3(v, km * g_rest, TN_DIMS))
            mu = jnp.mean(y, axis=-1, keepdims=True)
            yc = y - mu
            var = jnp.mean(yc * yc, axis=-1, keepdims=True)
            yn = yc * lax.rsqrt(var + RW_LNX_EPS) * lng_ref[:, sl] + lnb_ref[:, sl]
            bonus = jnp.sum(r * km * rk_ref[:, sl], axis=-1, keepdims=True) * v
            ys.append(yn + bonus)
        y_ref[0, pl.ds(t0, c), :] = jnp.concatenate(ys, axis=1)
        return carry

    lax.fori_loop(0, nchunks, chunk_body, 0)


def _rw_recurrence(r, lw, km, v, kk, a_ic, r_k, lnx_g, lnx_b):
    b, s, d = r.shape
    seq = pl.BlockSpec((1, s, 128), lambda bi, hp: (bi, 0, hp))
    par = pl.BlockSpec((1, 128), lambda bi, hp: (0, hp))
    return pl.pallas_call(
        _rw_rec_kernel,
        grid=(b, d // 128),
        in_specs=[seq] * 6 + [par] * 3,
        out_specs=seq,
        out_shape=jax.ShapeDtypeStruct((b, s, d), F32),
        scratch_shapes=[pltpu.VMEM((2, RW_HD, RW_HD), F32)],
        compiler_params=_cparams(("parallel", "parallel")),
        name="rwkv_recurrence",
    )(r, lw, km, v, kk, a_ic, r_k, lnx_g, lnx_b)


def _mix_kernel(oda_ref, yrw_ref, gout_ref, ga_ref, gb_ref, x_ref, wda_ref, wrw_ref, wmix_ref,
                lng_ref, lnb_ref, o_ref, *, alpha):
    y_da = jnp.dot(oda_ref[...], wda_ref[...], preferred_element_type=F32)
    y_rw = jnp.dot((yrw_ref[...] * gout_ref[...]).astype(BF16), wrw_ref[...], preferred_element_type=F32)
    mixed = jax.nn.sigmoid(ga_ref[...]) * y_da + jax.nn.sigmoid(gb_ref[...]) * y_rw
    z = alpha * x_ref[...] + jnp.dot(mixed.astype(BF16), wmix_ref[...], preferred_element_type=F32)
    o_ref[...] = _layer_norm_rows(z, lng_ref[...], lnb_ref[...])


def _mix(o_da, y_rw, g_out, rest, x, w_da, w_rw, w_mix, ln_g, ln_b, alpha, tm=512):
    t, d = x.shape
    rowb = lambda c: pl.BlockSpec((tm, d), lambda i: (i, c))
    full = lambda a: pl.BlockSpec(a.shape, lambda i: (0,) * a.ndim)
    return pl.pallas_call(
        functools.partial(_mix_kernel, alpha=alpha),
        grid=(t // tm,),
        in_specs=[rowb(0), rowb(0), rowb(0), rowb(3), rowb(4), rowb(0),
                  full(w_da), full(w_rw), full(w_mix), full(ln_g), full(ln_b)],
        out_specs=rowb(0),
        out_shape=jax.ShapeDtypeStruct((t, d), F32),
        compiler_params=_cparams(("parallel",)),
        name="mix_merge_ln1",
    )(o_da, y_rw, g_out, rest, rest, x, w_da, w_rw, w_mix, ln_g, ln_b)


def _mem_kv_kernel(mem_ref, g_ref, b_ref, w_ref, ck_ref, cv_ref):
    m = _layer_norm_rows(mem_ref[0], g_ref[...], b_ref[...])
    kv = jnp.dot(m.astype(BF16), w_ref[...], preferred_element_type=F32)
    ck_ref[0] = kv[:, :D_MODEL].astype(BF16)
    cv_ref[0] = kv[:, D_MODEL:].astype(BF16)


def _mem_kv(mem, g, b, w_ckv):
    bsz, m, d = mem.shape
    full = lambda a: pl.BlockSpec(a.shape, lambda i: (0,) * a.ndim)
    blk = pl.BlockSpec((1, m, d), lambda i: (i, 0, 0))
    return pl.pallas_call(
        _mem_kv_kernel,
        grid=(bsz,),
        in_specs=[blk, full(g), full(b), full(w_ckv)],
        out_specs=[blk, blk],
        out_shape=[jax.ShapeDtypeStruct((bsz, m, d), BF16)] * 2,
        compiler_params=_cparams(("parallel",)),
        name="mem_kv",
    )(mem, g, b, w_ckv)


def _cross_kernel(x_ref, ck_ref, cv_ref, wq_ref, wo_ref, lng_ref, lnb_ref, o_ref, *, alpha):
    x = x_ref[...]
    cq = jnp.dot(x.astype(BF16), wq_ref[...], preferred_element_type=F32)
    cq = (cq * (CA_HD ** -0.5)).astype(BF16)
    outs = []
    for h in range(CA_HEADS):
        sl = slice(h * CA_HD, (h + 1) * CA_HD)
        s = lax.dot_general(cq[:, sl], ck_ref[0, :, sl], NT_DIMS, preferred_element_type=F32)
        s = s - jnp.max(s, axis=-1, keepdims=True)
        p = jnp.exp(s)
        p = p / jnp.sum(p, axis=-1, keepdims=True)
        outs.append(jnp.dot(p.astype(BF16), cv_ref[0, :, sl], preferred_element_type=F32))
    co = jnp.dot(jnp.concatenate(outs, axis=1).astype(BF16), wo_ref[...], preferred_element_type=F32)
    o_ref[...] = _layer_norm_rows(alpha * x + co, lng_ref[...], lnb_ref[...])


def _cross_attention(x, ck, cv, w_cq, w_co, ln_g, ln_b, seq, alpha, tm=512):
    t, d = x.shape
    tiles_per_seq = seq // tm
    full = lambda a: pl.BlockSpec(a.shape, lambda i: (0,) * a.ndim)
    rowb = pl.BlockSpec((tm, d), lambda i: (i, 0))
    memb = pl.BlockSpec((1,) + ck.shape[1:], lambda i: (i // tiles_per_seq, 0, 0))
    return pl.pallas_call(
        functools.partial(_cross_kernel, alpha=alpha),
        grid=(t // tm,),
        in_specs=[rowb, memb, memb, full(w_cq), full(w_co), full(ln_g), full(ln_b)],
        out_specs=rowb,
        out_shape=jax.ShapeDtypeStruct((t, d), F32),
        compiler_params=_cparams(("parallel",)),
        name="cross_attention_ln2",
    )(x, ck, cv, w_cq, w_co, ln_g, ln_b)


def _ffn_kernel(x_ref, wg_ref, wu_ref, wo_ref, lng_ref, lnb_ref, o_ref, xb_ref, acc_ref, *, alpha):
    j = pl.program_id(1)

    @pl.when(j == 0)
    def _():
        xb_ref[...] = x_ref[...].astype(BF16)
        acc_ref[...] = jnp.zeros(acc_ref.shape, F32)

    xb = xb_ref[...]
    gate = jnp.dot(xb, wg_ref[...], preferred_element_type=F32)
    up = jnp.dot(xb, wu_ref[...], preferred_element_type=F32)
    h = (gate * jax.nn.sigmoid(gate) * up).astype(BF16)
    acc_ref[...] += jnp.dot(h, wo_ref[...], preferred_element_type=F32)

    @pl.when(j == pl.num_programs(1) - 1)
    def _():
        o_ref[...] = _layer_norm_rows(alpha * x_ref[...] + acc_ref[...], lng_ref[...], lnb_ref[...])


def _ffn(x, w_in, w_out, ln_g, ln_b, alpha, tm=512, tf=1408):
    t, d = x.shape
    nf = D_FF // tf
    full = lambda a: pl.BlockSpec(a.shape, lambda i, j: (0,) * a.ndim)
    rowb = pl.BlockSpec((tm, d), lambda i, j: (i, 0))
    return pl.pallas_call(
        functools.partial(_ffn_kernel, alpha=alpha),
        grid=(t // tm, nf),
        in_specs=[rowb,
                  pl.BlockSpec((d, tf), lambda i, j: (0, j)),
                  pl.BlockSpec((d, tf), lambda i, j: (0, nf + j)),
                  pl.BlockSpec((tf, d), lambda i, j: (j, 0)),
                  full(ln_g), full(ln_b)],
        out_specs=rowb,
        out_shape=jax.ShapeDtypeStruct((t, d), F32),
        scratch_shapes=[pltpu.VMEM((tm, d), BF16), pltpu.VMEM((tm, d), F32)],
        compiler_params=_cparams(("parallel", "arbitrary")),
        name="swiglu_ffn_ln3",
    )(x, w_in, w_in, w_out, ln_g, ln_b)


def _pad_cols(a, width):
    return jnp.pad(a, ((0, 0), (0, width - a.shape[1])))


def _pad_rows(a, height):
    return jnp.pad(a, ((0, height - a.shape[0]), (0, 0)))


def kernel(x, mem, rel_bias, w_in, shift_mu, lambda_q1, lambda_k1, lambda_q2, lambda_k2, da_subln_g, w_da_proj, rw_w0, rw_w2, rw_a0, rw_a2, rw_g2, rw_k_k, rw_k_a, rw_r_k, rw_lnx_g, rw_lnx_b, w_rw_proj, w_mix_out, ln1_g, ln1_b, mem_ln_g, mem_ln_b, w_cq, w_ckv, w_co, ln2_g, ln2_b, w_ffn_in, w_ffn_out, ln3_g, ln3_b):
    bsz, seq, d = x.shape
    depth = w_in.shape[0]
    alpha = (2.0 * depth) ** 0.25
    t = bsz * seq
    row = lambda a: a.reshape(1, -1).astype(F32)
    bias_tiles = _attn_bias_tiles(rel_bias)
    xf = x.reshape(t, d)

    qkv_w = 3 * D_MODEL
    rw0 = qkv_w
    lo0 = rw0 + 3 * D_MODEL
    gate0 = lo0 + DECAY_LORA + AAA_LORA + GATE_LORA

    for l in range(depth):
        w = w_in[l]
        w_qkv = w[:, :qkv_w].astype(BF16)
        w_rest = jnp.concatenate([
            w[:, rw0:lo0], w[:, gate0:gate0 + 2 * D_MODEL],
            _pad_cols(w[:, lo0:lo0 + DECAY_LORA], 128),
            _pad_cols(w[:, lo0 + DECAY_LORA:lo0 + DECAY_LORA + AAA_LORA], 128),
            _pad_cols(w[:, lo0 + DECAY_LORA + AAA_LORA:gate0], 256)], axis=1).astype(BF16)
        mu = shift_mu[l].astype(F32)
        mu_rkv = mu[:3 * D_MODEL].reshape(3, D_MODEL)
        o1 = 3 * D_MODEL
        mu_lo = jnp.concatenate([
            _pad_cols(mu[None, o1:o1 + DECAY_LORA], 128),
            _pad_cols(mu[None, o1 + DECAY_LORA:o1 + DECAY_LORA + AAA_LORA], 128),
            _pad_cols(mu[None, o1 + DECAY_LORA + AAA_LORA:], 256)], axis=1)
        w2 = _pad_rows(rw_w2[l].astype(F32), 128)
        a2 = _pad_rows(rw_a2[l].astype(F32), 128)
        w2h, w2l = _split2(w2)
        a2h, a2l = _split2(a2)
        g2 = _pad_rows(rw_g2[l], 256).astype(BF16)

        qkv = _matmul(xf, w_qkv, BF16, 1024, 1024, "in_proj_qkv")
        rest = _matmul(xf, w_rest, F32, 1024, 512, "in_proj_rest")

        lam_init = 0.8 - 0.6 * math.exp(-0.3 * l)
        lamv = jnp.stack([lambda_q1[l], lambda_k1[l], lambda_q2[l], lambda_k2[l]]).astype(F32)
        o_da = _diff_attention(qkv.reshape(bsz, seq, qkv_w), lamv, bias_tiles, row(da_subln_g[l]), lam_init)

        r, lw, km, v, kk, a_ic, g_out = _rw_prep(
            rest, seq, mu_rkv, mu_lo, row(rw_w0[l]), w2h, w2l, row(rw_a0[l]), a2h, a2l, g2,
            row(rw_k_k[l]), row(rw_k_a[l]))
        sh = lambda a: a.reshape(bsz, seq, d)
        y_rw = _rw_recurrence(sh(r), sh(lw), sh(km), sh(v), sh(kk), sh(a_ic),
                              row(rw_r_k[l]), row(rw_lnx_g[l]), row(rw_lnx_b[l]))

        x1 = _mix(o_da.reshape(t, d), y_rw.reshape(t, d), g_out, rest, xf,
                  w_da_proj[l].astype(BF16), w_rw_proj[l].astype(BF16), w_mix_out[l].astype(BF16),
                  row(ln1_g[l]), row(ln1_b[l]), alpha)

        ck, cv = _mem_kv(mem, row(mem_ln_g[l]), row(mem_ln_b[l]), w_ckv[l].astype(BF16))
        x2 = _cross_attention(x1, ck, cv, w_cq[l].astype(BF16), w_co[l].astype(BF16),
                              row(ln2_g[l]), row(ln2_b[l]), seq, alpha)

        xf = _ffn(x2, w_ffn_in[l].astype(BF16), w_ffn_out[l].astype(BF16),
                  row(ln3_g[l]), row(ln3_b[l]), alpha)
    return xf.reshape(bsz, seq, d)
```

```python
import functools
import math

import jax
import jax.numpy as jnp
from jax import lax
from jax.experimental import pallas as pl
from jax.experimental.pallas import tpu as pltpu

F32 = jnp.float32
BF16 = jnp.bfloat16

D_MODEL = 1024
DA_HEADS = 8
DA_HD = 64
NUM_BUCKETS = 32
MAX_DISTANCE = 128
RW_HD = 64
RW_HEADS = D_MODEL // RW_HD
DECAY_LORA = 64
AAA_LORA = 64
GATE_LORA = 160
RW_LNX_EPS = 64e-5
MEM_TOKENS = 256
CA_HEADS = 4
CA_HD = D_MODEL // CA_HEADS
D_FF = 2816
LN_EPS = 1e-5
NEG_BIG = -1e30

ATT_TILE = 256
RW_CHUNK = 64
VMEM_LIMIT = 56 * 1024 * 1024

NT_DIMS = (((1,), (1,)), ((), ()))
TN_DIMS = (((0,), (0,)), ((), ()))


def _cparams(sem, vmem=VMEM_LIMIT):
    return pltpu.CompilerParams(dimension_semantics=sem, vmem_limit_bytes=vmem)


def _split2(a):
    hi = a.astype(BF16)
    lo = (a - hi.astype(F32)).astype(BF16)
    return hi, lo


def _dot3(a, b, dims=(((1,), (0,)), ((), ()))):
    ah, al = _split2(a)
    bh, bl = _split2(b)
    d = functools.partial(lax.dot_general, dimension_numbers=dims, preferred_element_type=F32)
    return d(ah, bh) + (d(ah, bl) + d(al, bh))


def _layer_norm_rows(z, g, b):
    mu = jnp.mean(z, axis=-1, keepdims=True)
    zc = z - mu
    var = jnp.mean(zc * zc, axis=-1, keepdims=True)
    return zc * lax.rsqrt(var + LN_EPS) * g + b


def _mm_kernel(x_ref, w_ref, o_ref, xb_ref):
    @pl.when(pl.program_id(1) == 0)
    def _():
        xb_ref[...] = x_ref[...].astype(BF16)

    o_ref[...] = jnp.dot(xb_ref[...], w_ref[...], preferred_element_type=F32).astype(o_ref.dtype)


def _matmul(x, w, out_dtype, tm, tn, name):
    t, k = x.shape
    n = w.shape[1]
    return pl.pallas_call(
        _mm_kernel,
        grid=(t // tm, n // tn),
        in_specs=[pl.BlockSpec((tm, k), lambda i, j: (i, 0)),
                  pl.BlockSpec((k, tn), lambda i, j: (0, j))],
        out_specs=pl.BlockSpec((tm, tn), lambda i, j: (i, j)),
        out_shape=jax.ShapeDtypeStruct((t, n), out_dtype),
        scratch_shapes=[pltpu.VMEM((tm, k), BF16)],
        compiler_params=_cparams(("parallel", "arbitrary")),
        name=name,
    )(x, w)


def _qkv_kernel(x_ref, wqt_ref, wk_ref, wvt_ref, qt_ref, k_ref, vt_ref):
    xb = x_ref[...].astype(BF16)
    k_ref[...] = jnp.dot(xb, wk_ref[...], preferred_element_type=F32).astype(BF16)
    qt_ref[0] = lax.dot_general(wqt_ref[...], xb, NT_DIMS, preferred_element_type=F32).astype(BF16)
    vt_ref[0] = lax.dot_general(wvt_ref[...], xb, NT_DIMS, preferred_element_type=F32).astype(BF16)


def _qkv_proj(x, wqt, wk, wvt, bsz, seq, tm=512):
    t, d = x.shape
    tiles_per_seq = seq // tm
    full = lambda a: pl.BlockSpec(a.shape, lambda i: (0,) * a.ndim)
    tblk = pl.BlockSpec((1, d, tm), lambda i: (i // tiles_per_seq, 0, i % tiles_per_seq))
    rowb = pl.BlockSpec((tm, d), lambda i: (i, 0))
    tshape = jax.ShapeDtypeStruct((bsz, d, seq), BF16)
    return pl.pallas_call(
        _qkv_kernel,
        grid=(t // tm,),
        in_specs=[rowb, full(wqt), full(wk), full(wvt)],
        out_specs=[tblk, rowb, tblk],
        out_shape=[tshape, jax.ShapeDtypeStruct((t, d), BF16), tshape],
        compiler_params=_cparams(("parallel",)),
        name="in_proj_qkv",
    )(x, wqt, wk, wvt)


def _t5_bucket(rel):
    n = jnp.maximum(rel, 0)
    max_exact = NUM_BUCKETS // 2
    nf = jnp.maximum(n, 1).astype(jnp.float32)
    large = max_exact + (jnp.log(nf / max_exact) / math.log(MAX_DISTANCE / max_exact)
                         * (NUM_BUCKETS - max_exact)).astype(jnp.int32)
    large = jnp.minimum(large, NUM_BUCKETS - 1)
    return jnp.where(n < max_exact, n, large)


def _attn_bias_tiles(rel_bias):
    ta = ATT_TILE
    tab = rel_bias.astype(F32) - rel_bias[NUM_BUCKETS - 1].astype(F32)[None, :]
    rel = jnp.arange(ta)[None, :] - jnp.arange(ta)[:, None]

    def lookup(r):
        onehot = (_t5_bucket(r)[:, :, None] == jnp.arange(NUM_BUCKETS)[None, None, :]).astype(F32)
        return jnp.einsum('kqb,bh->hkq', onehot, tab, precision=lax.Precision.HIGHEST)

    diag = jnp.where((rel >= 0)[None], lookup(rel), NEG_BIG)
    return jnp.stack([lookup(rel + ta), diag], axis=1)


def _attn_kernel(lamv_ref, qt_ref, k_ref, vt_ref, bias_ref, g_ref, o_ref, m_sc, l_sc, acc_sc, *, lam_init):
    ta = ATT_TILE
    qi = pl.program_id(2)

    lv = lamv_ref[...]
    lam = (jnp.exp(jnp.sum(lv[0:1] * lv[1:2], axis=-1, keepdims=True))
           - jnp.exp(jnp.sum(lv[2:3] * lv[3:4], axis=-1, keepdims=True)) + lam_init)

    qt = qt_ref[0] * jnp.asarray(DA_HD ** -0.5, BF16)
    rowi = lax.broadcasted_iota(jnp.int32, qt.shape, 0)
    zero = jnp.zeros_like(qt)
    qqt = jnp.concatenate([jnp.where(rowi < DA_HD, qt, zero), jnp.where(rowi >= DA_HD, qt, zero)], axis=1)

    m_sc[...] = jnp.full(m_sc.shape, NEG_BIG, F32)
    l_sc[...] = jnp.zeros(l_sc.shape, F32)
    acc_sc[...] = jnp.zeros(acc_sc.shape, F32)

    def step(j, bias, nk=ta):
        k0 = pl.multiple_of(j * nk, nk)
        k = k_ref[0, pl.ds(k0, nk), :]
        vt = vt_ref[0, :, pl.ds(k0, nk)]
        s = jnp.dot(k, qqt, preferred_element_type=F32)
        if bias is not None:
            s = s + jnp.concatenate([bias, bias], axis=1)
        m_prev = m_sc[...]
        m_new = jnp.maximum(m_prev, jnp.max(s, axis=0, keepdims=True))
        alpha = jnp.exp(m_prev - m_new)
        p = jnp.exp(s - m_new)
        l_sc[...] = alpha * l_sc[...] + jnp.sum(p, axis=0, keepdims=True)
        acc_sc[...] = alpha * acc_sc[...] + jnp.dot(vt, p.astype(BF16), preferred_element_type=F32)
        m_sc[...] = m_new

    nfar = jnp.maximum(qi - 1, 0)

    def far_body(j, carry):
        step(j, None, 2 * ta)
        return carry

    lax.fori_loop(0, nfar // 2, far_body, 0)

    @pl.when(nfar % 2 == 1)
    def _():
        step(nfar - 1, None)

    @pl.when(qi >= 1)
    def _():
        step(qi - 1, bias_ref[0, 0])

    step(qi, bias_ref[0, 1])

    acc = acc_sc[...] * (1.0 / l_sc[...])
    ot = acc[:, 0:ta] - lam * acc[:, ta:2 * ta]
    ms = jnp.mean(ot * ot, axis=0, keepdims=True)
    ot = ot * (lax.rsqrt(ms + LN_EPS) * (1.0 - lam_init))
    o_ref[0] = (ot.T * g_ref[...]).astype(o_ref.dtype)


def _diff_attention(qt, k, vt, lamv, bias_tiles, subln_g, lam_init):
    b, s, _ = k.shape
    ta = ATT_TILE
    h = DA_HEADS
    return pl.pallas_call(
        functools.partial(_attn_kernel, lam_init=lam_init),
        grid=(b, h, s // ta),
        in_specs=[
            pl.BlockSpec((4, DA_HD), lambda bi, hi, qi: (0, 0)),
            pl.BlockSpec((1, 128, ta), lambda bi, hi, qi: (bi, hi, qi)),
            pl.BlockSpec((1, s, 128), lambda bi, hi, qi: (bi, 0, hi)),
            pl.BlockSpec((1, 128, s), lambda bi, hi, qi: (bi, hi, 0)),
            pl.BlockSpec((1, 2, ta, ta), lambda bi, hi, qi: (hi, 0, 0, 0)),
            pl.BlockSpec((1, 128), lambda bi, hi, qi: (0, 0)),
        ],
        out_specs=pl.BlockSpec((1, ta, 128), lambda bi, hi, qi: (bi, qi, hi)),
        out_shape=jax.ShapeDtypeStruct((b, s, h * 128), BF16),
        scratch_shapes=[pltpu.VMEM((1, 2 * ta), F32), pltpu.VMEM((1, 2 * ta), F32),
                        pltpu.VMEM((128, 2 * ta), F32)],
        compiler_params=_cparams(("parallel", "parallel", "arbitrary")),
        name="diff_attention",
    )(lamv, qt, k, vt, bias_tiles, subln_g)


def _rw_prep_kernel(r_ref, k_ref, v_ref, lo_ref, rp_ref, kp_ref, vp_ref, lop_ref,
                    mu_ref, mulo_ref, w0_ref, w2h_ref, w2l_ref, a0_ref, a2h_ref, a2l_ref, g2_ref,
                    kk_ref, ka_ref,
                    r_o, lw_o, km_o, v_o, kk_o, a_o, g_o, *, tiles_per_seq):
    i = pl.program_id(0)
    keep = (i % tiles_per_seq != 0).astype(F32)

    def shifted(cur_ref, prev_ref, mu):
        p = cur_ref[...]
        row = lax.broadcasted_iota(jnp.int32, p.shape, 0)
        prev_last = prev_ref[7:8, :] * keep
        pm1 = jnp.where(row == 0, prev_last, pltpu.roll(p, 1, 0))
        return p + mu * (pm1 - p)

    r = shifted(r_ref, rp_ref, mu_ref[0:1])
    k = shifted(k_ref, kp_ref, mu_ref[1:2])
    v = shifted(v_ref, vp_ref, mu_ref[2:3])
    lo = shifted(lo_ref, lop_ref, mulo_ref[...])

    def dot3w(a, wh_ref, wl_ref):
        ah, al = _split2(a)
        wh = wh_ref[...]
        d = functools.partial(jnp.dot, preferred_element_type=F32)
        return d(ah, wh) + (d(ah, wl_ref[...]) + d(al, wh))

    zw = w0_ref[...] + dot3w(jnp.tanh(lo[:, 0:128]), w2h_ref, w2l_ref)
    nz = -zw
    softplus = jnp.maximum(nz, 0.0) + jnp.log(1.0 + jnp.exp(-jnp.abs(nz)))
    w_log = -softplus - 0.5
    lw_o[...] = -jnp.exp(w_log)
    a_ic = jax.nn.sigmoid(a0_ref[...] + dot3w(lo[:, 128:256], a2h_ref, a2l_ref))
    g_o[...] = jnp.dot(jax.nn.sigmoid(lo[:, 256:512]).astype(BF16), g2_ref[...], preferred_element_type=F32)
    r_o[...] = r
    v_o[...] = v
    kk_o[...] = k * kk_ref[...]
    km_o[...] = k * (1.0 + (a_ic - 1.0) * ka_ref[...])
    a_o[...] = a_ic


def _rw_prep(rest, seq, mu_rkv, mu_lo, w0, w2h, w2l, a0, a2h, a2l, g2, k_k, k_a, tm=256):
    t = rest.shape[0]
    d = D_MODEL
    cur = lambda c, w: pl.BlockSpec((tm, w), lambda i: (i, c))
    prev = lambda c, w: pl.BlockSpec((8, w), lambda i: (jnp.maximum(i * (tm // 8) - 1, 0), c))
    full = lambda a: pl.BlockSpec(a.shape, lambda i: (0,) * a.ndim)
    params = (mu_rkv, mu_lo, w0, w2h, w2l, a0, a2h, a2l, g2, k_k, k_a)
    out = jax.ShapeDtypeStruct((t, d), F32)
    return pl.pallas_call(
        functools.partial(_rw_prep_kernel, tiles_per_seq=seq // tm),
        grid=(t // tm,),
        in_specs=[cur(0, d), cur(1, d), cur(2, d), cur(10, 512),
                  prev(0, d), prev(1, d), prev(2, d), prev(10, 512)] + [full(a) for a in params],
        out_specs=[pl.BlockSpec((tm, d), lambda i: (i, 0))] * 7,
        out_shape=[out] * 7,
        compiler_params=_cparams(("parallel",)),
        name="rwkv_prep",
    )(rest, rest, rest, rest, rest, rest, rest, rest, *params)


def _rw_rec_kernel(r_ref, lw_ref, km_ref, v_ref, kk_ref, a_ref, rk_ref, lng_ref, lnb_ref, y_ref, s_sc):
    c = RW_CHUNK
    n = RW_HD
    nchunks = r_ref.shape[1] // c
    s_sc[...] = jnp.zeros(s_sc.shape, F32)

    row = lax.broadcasted_iota(jnp.int32, (c, c), 0)
    col = lax.broadcasted_iota(jnp.int32, (c, c), 1)
    incl = row >= col
    strict = row > col
    tri_ones = incl.astype(BF16)
    eye = (row == col).astype(F32)

    def chunk_body(ci, carry):
        t0 = pl.multiple_of(ci * c, c)
        r2 = r_ref[0, pl.ds(t0, c), :]
        lw2 = lw_ref[0, pl.ds(t0, c), :]
        km2 = km_ref[0, pl.ds(t0, c), :]
        v2 = v_ref[0, pl.ds(t0, c), :]
        kk2 = kk_ref[0, pl.ds(t0, c), :]
        a2 = a_ref[0, pl.ds(t0, c), :]
        ys = []
        for hh in range(2):
            sl = slice(hh * n, (hh + 1) * n)
            r, lw, km, v, kk, a_ic = r2[:, sl], lw2[:, sl], km2[:, sl], v2[:, sl], kk2[:, sl], a2[:, sl]
            l1 = lw.astype(BF16)
            rem = lw - l1.astype(F32)
            l2 = rem.astype(BF16)
            l3 = (rem - l2.astype(F32)).astype(BF16)
            d = functools.partial(jnp.dot, preferred_element_type=F32)
            cum = d(tri_ones, l1) + (d(tri_ones, l2) + d(tri_ones, l3))
            cum_last = cum[c - 1:c, :]
            g_in = jnp.exp(cum)
            g_prev = jnp.exp(cum - lw)
            g_inv = jnp.exp(-cum)
            g_rest = jnp.exp(cum_last - cum)
            nrm = jnp.sqrt(jnp.sum(kk * kk, axis=-1, keepdims=True))
            kkn = kk / jnp.maximum(nrm, 1e-12)
            b = kkn * a_ic
            at = -kkn * g_prev
            bt = b * g_inv
            kt = km * g_inv
            rt = r * g_in
            big = _dot3(jnp.concatenate([at, rt], axis=0), jnp.concatenate([bt, kt], axis=0), NT_DIMS)
            m_ab = jnp.where(strict, big[0:c, 0:c], 0.0)
            m_ak = jnp.where(strict, big[0:c, c:2 * c], 0.0)
            m_rb = jnp.where(incl, big[c:2 * c, 0:c], 0.0)
            m_rk = jnp.where(incl, big[c:2 * c, c:2 * c], 0.0)
            tinv = eye + m_ab
            pw = m_ab
            for _ in range(5):
                pw = _dot3(pw, pw)
                tinv = tinv + _dot3(tinv, pw)
            s0 = s_sc[hh]
            u = _dot3(tinv, _dot3(at, s0, NT_DIMS) + _dot3(m_ak, v))
            y = _dot3(rt, s0, NT_DIMS) + _dot3(m_rb, u) + _dot3(m_rk, v)
            s_sc[hh] = (s0 * jnp.exp(cum_last)
                        + _dot3(u, b * g_rest, TN_DIMS) + _dot3(v, km * g_rest, TN_DIMS))
            mu = jnp.mean(y, axis=-1, keepdims=True)
            yc = y - mu
            var = jnp.mean(yc * yc, axis=-1, keepdims=True)
            yn = yc * lax.rsqrt(var + RW_LNX_EPS) * lng_ref[:, sl] + lnb_ref[:, sl]
            bonus = jnp.sum(r * km * rk_ref[:, sl], axis=-1, keepdims=True) * v
            ys.append(yn + bonus)
        y_ref[0, pl.ds(t0, c), :] = jnp.concatenate(ys, axis=1)
        return carry

    lax.fori_loop(0, nchunks, chunk_body, 0)


def _rw_recurrence(r, lw, km, v, kk, a_ic, r_k, lnx_g, lnx_b):
    b, s, d = r.shape
    seq = pl.BlockSpec((1, s, 128), lambda bi, hp: (bi, 0, hp))
    par = pl.BlockSpec((1, 128), lambda bi, hp: (0, hp))
    return pl.pallas_call(
        _rw_rec_kernel,
        grid=(b, d // 128),
        in_specs=[seq] * 6 + [par] * 3,
        out_specs=seq,
        out_shape=jax.ShapeDtypeStruct((b, s, d), F32),
        scratch_shapes=[pltpu.VMEM((2, RW_HD, RW_HD), F32)],
        compiler_params=_cparams(("parallel", "parallel")),
        name="rwkv_recurrence",
    )(r, lw, km, v, kk, a_ic, r_k, lnx_g, lnx_b)


def _mix_kernel(oda_ref, yrw_ref, gout_ref, ga_ref, gb_ref, x_ref, wda_ref, wrw_ref, wmix_ref,
                lng_ref, lnb_ref, o_ref, *, alpha):
    y_da = jnp.dot(oda_ref[...], wda_ref[...], preferred_element_type=F32)
    y_rw = jnp.dot((yrw_ref[...] * gout_ref[...]).astype(BF16), wrw_ref[...], preferred_element_type=F32)
    mixed = jax.nn.sigmoid(ga_ref[...]) * y_da + jax.nn.sigmoid(gb_ref[...]) * y_rw
    z = alpha * x_ref[...] + jnp.dot(mixed.astype(BF16), wmix_ref[...], preferred_element_type=F32)
    o_ref[...] = _layer_norm_rows(z, lng_ref[...], lnb_ref[...])


def _mix(o_da, y_rw, g_out, rest, x, w_da, w_rw, w_mix, ln_g, ln_b, alpha, tm=512):
    t, d = x.shape
    rowb = lambda c: pl.BlockSpec((tm, d), lambda i: (i, c))
    full = lambda a: pl.BlockSpec(a.shape, lambda i: (0,) * a.ndim)
    return pl.pallas_call(
        functools.partial(_mix_kernel, alpha=alpha),
        grid=(t // tm,),
        in_specs=[rowb(0), rowb(0), rowb(0), rowb(3), rowb(4), rowb(0),
                  full(w_da), full(w_rw), full(w_mix), full(ln_g), full(ln_b)],
        out_specs=rowb(0),
        out_shape=jax.ShapeDtypeStruct((t, d), F32),
        compiler_params=_cparams(("parallel",)),
        name="mix_merge_ln1",
    )(o_da, y_rw, g_out, rest, rest, x, w_da, w_rw, w_mix, ln_g, ln_b)


def _mem_kv_kernel(mem_ref, g_ref, b_ref, w_ref, ck_ref, cv_ref):
    m = _layer_norm_rows(mem_ref[0], g_ref[...], b_ref[...])
    kv = jnp.dot(m.astype(BF16), w_ref[...], preferred_element_type=F32)
    ck_ref[0] = kv[:, :D_MODEL].astype(BF16)
    cv_ref[0] = kv[:, D_MODEL:].astype(BF16)


def _mem_kv(mem, g, b, w_ckv):
    bsz, m, d = mem.shape
    full = lambda a: pl.BlockSpec(a.shape, lambda i: (0,) * a.ndim)
    blk = pl.BlockSpec((1, m, d), lambda i: (i, 0, 0))
    return pl.pallas_call(
        _mem_kv_kernel,
        grid=(bsz,),
        in_specs=[blk, full(g), full(b), full(w_ckv)],
        out_specs=[blk, blk],
        out_shape=[jax.ShapeDtypeStruct((bsz, m, d), BF16)] * 2,
        compiler_params=_cparams(("parallel",)),
        name="mem_kv",
    )(mem, g, b, w_ckv)


def _cross_kernel(x_ref, ck_ref, cv_ref, wq_ref, wo_ref, lng_ref, lnb_ref, o_ref, *, alpha):
    x = x_ref[...]
    cq = jnp.dot(x.astype(BF16), wq_ref[...], preferred_element_type=F32)
    cq = (cq * (CA_HD ** -0.5)).astype(BF16)
    outs = []
    for h in range(CA_HEADS):
        sl = slice(h * CA_HD, (h + 1) * CA_HD)
        s = lax.dot_general(cq[:, sl], ck_ref[0, :, sl], NT_DIMS, preferred_element_type=F32)
        s = s - jnp.max(s, axis=-1, keepdims=True)
        p = jnp.exp(s)
        p = p / jnp.sum(p, axis=-1, keepdims=True)
        outs.append(jnp.dot(p.astype(BF16), cv_ref[0, :, sl], preferred_element_type=F32))
    co = jnp.dot(jnp.concatenate(outs, axis=1).astype(BF16), wo_ref[...], preferred_element_type=F32)
    o_ref[...] = _layer_norm_rows(alpha * x + co, lng_ref[...], lnb_ref[...])


def _cross_attention(x, ck, cv, w_cq, w_co, ln_g, ln_b, seq, alpha, tm=512):
    t, d = x.shape
    tiles_per_seq = seq // tm
    full = lambda a: pl.BlockSpec(a.shape, lambda i: (0,) * a.ndim)
    rowb = pl.BlockSpec((tm, d), lambda i: (i, 0))
    memb = pl.BlockSpec((1,) + ck.shape[1:], lambda i: (i // tiles_per_seq, 0, 0))
    return pl.pallas_call(
        functools.partial(_cross_kernel, alpha=alpha),
        grid=(t // tm,),
        in_specs=[rowb, memb, memb, full(w_cq), full(w_co), full(ln_g), full(ln_b)],
        out_specs=rowb,
        out_shape=jax.ShapeDtypeStruct((t, d), F32),
        compiler_params=_cparams(("parallel",)),
        name="cross_attention_ln2",
    )(x, ck, cv, w_cq, w_co, ln_g, ln_b)


def _ffn_kernel(x_ref, wg_ref, wu_ref, wo_ref, lng_ref, lnb_ref, o_ref, xb_ref, acc_ref, *, alpha):
    j = pl.program_id(1)

    @pl.when(j == 0)
    def _():
        xb_ref[...] = x_ref[...].astype(BF16)
        acc_ref[...] = jnp.zeros(acc_ref.shape, F32)

    xb = xb_ref[...]
    gate = jnp.dot(xb, wg_ref[...], preferred_element_type=F32)
    up = jnp.dot(xb, wu_ref[...], preferred_element_type=F32)
    h = (gate * jax.nn.sigmoid(gate) * up).astype(BF16)
    acc_ref[...] += jnp.dot(h, wo_ref[...], preferred_element_type=F32)

    @pl.when(j == pl.num_programs(1) - 1)
    def _():
        o_ref[...] = _layer_norm_rows(alpha * x_ref[...] + acc_ref[...], lng_ref[...], lnb_ref[...])


def _ffn(x, w_in, w_out, ln_g, ln_b, alpha, tm=512, tf=1408):
    t, d = x.shape
    nf = D_FF // tf
    full = lambda a: pl.BlockSpec(a.shape, lambda i, j: (0,) * a.ndim)
    rowb = pl.BlockSpec((tm, d), lambda i, j: (i, 0))
    return pl.pallas_call(
        functools.partial(_ffn_kernel, alpha=alpha),
        grid=(t // tm, nf),
        in_specs=[rowb,
                  pl.BlockSpec((d, tf), lambda i, j: (0, j)),
                  pl.BlockSpec((d, tf), lambda i, j: (0, nf + j)),
                  pl.BlockSpec((tf, d), lambda i, j: (j, 0)),
                  full(ln_g), full(ln_b)],
        out_specs=rowb,
        out_shape=jax.ShapeDtypeStruct((t, d), F32),
        scratch_shapes=[pltpu.VMEM((tm, d), BF16), pltpu.VMEM((tm, d), F32)],
        compiler_params=_cparams(("parallel", "arbitrary")),
        name="swiglu_ffn_ln3",
    )(x, w_in, w_in, w_out, ln_g, ln_b)


def _pad_cols(a, width):
    return jnp.pad(a, ((0, 0), (0, width - a.shape[1])))


def _pad_rows(a, height):
    return jnp.pad(a, ((0, height - a.shape[0]), (0, 0)))


def kernel(x, mem, rel_bias, w_in, shift_mu, lambda_q1, lambda_k1, lambda_q2, lambda_k2, da_subln_g, w_da_proj, rw_w0, rw_w2, rw_a0, rw_a2, rw_g2, rw_k_k, rw_k_a, rw_r_k, rw_lnx_g, rw_lnx_b, w_rw_proj, w_mix_out, ln1_g, ln1_b, mem_ln_g, mem_ln_b, w_cq, w_ckv, w_co, ln2_g, ln2_b, w_ffn_in, w_ffn_out, ln3_g, ln3_b):
    bsz, seq, d = x.shape
    depth = w_in.shape[0]
    alpha = (2.0 * depth) ** 0.25
    t = bsz * seq
    row = lambda a: a.reshape(1, -1).astype(F32)
    bias_tiles = _attn_bias_tiles(rel_bias)
    xf = x.reshape(t, d)

    qkv_w = 3 * D_MODEL
    rw0 = qkv_w
    lo0 = rw0 + 3 * D_MODEL
    gate0 = lo0 + DECAY_LORA + AAA_LORA + GATE_LORA

    for l in range(depth):
        w = w_in[l]
        w_qt = w[:, :D_MODEL].T.astype(BF16)
        w_k = w[:, D_MODEL:2 * D_MODEL].astype(BF16)
        w_vt = w[:, 2 * D_MODEL:qkv_w].T.astype(BF16)
        w_rest = jnp.concatenate([
            w[:, rw0:lo0], w[:, gate0:gate0 + 2 * D_MODEL],
            _pad_cols(w[:, lo0:lo0 + DECAY_LORA], 128),
            _pad_cols(w[:, lo0 + DECAY_LORA:lo0 + DECAY_LORA + AAA_LORA], 128),
            _pad_cols(w[:, lo0 + DECAY_LORA + AAA_LORA:gate0], 256)], axis=1).astype(BF16)
        mu = shift_mu[l].astype(F32)
        mu_rkv = mu[:3 * D_MODEL].reshape(3, D_MODEL)
        o1 = 3 * D_MODEL
        mu_lo = jnp.concatenate([
            _pad_cols(mu[None, o1:o1 + DECAY_LORA], 128),
            _pad_cols(mu[None, o1 + DECAY_LORA:o1 + DECAY_LORA + AAA_LORA], 128),
            _pad_cols(mu[None, o1 + DECAY_LORA + AAA_LORA:], 256)], axis=1)
        w2 = _pad_rows(rw_w2[l].astype(F32), 128)
        a2 = _pad_rows(rw_a2[l].astype(F32), 128)
        w2h, w2l = _split2(w2)
        a2h, a2l = _split2(a2)
        g2 = _pad_rows(rw_g2[l], 256).astype(BF16)

        qt, kda, vt = _qkv_proj(xf, w_qt, w_k, w_vt, bsz, seq)
        rest = _matmul(xf, w_rest, F32, 1024, 512, "in_proj_rest")

        lam_init = 0.8 - 0.6 * math.exp(-0.3 * l)
        lamv = jnp.stack([lambda_q1[l], lambda_k1[l], lambda_q2[l], lambda_k2[l]]).astype(F32)
        o_da = _diff_attention(qt, kda.reshape(bsz, seq, d), vt, lamv, bias_tiles, row(da_subln_g[l]), lam_init)

        r, lw, km, v, kk, a_ic, g_out = _rw_prep(
            rest, seq, mu_rkv, mu_lo, row(rw_w0[l]), w2h, w2l, row(rw_a0[l]), a2h, a2l, g2,
            row(rw_k_k[l]), row(rw_k_a[l]))
        sh = lambda a: a.reshape(bsz, seq, d)
        y_rw = _rw_recurrence(sh(r), sh(lw), sh(km), sh(v), sh(kk), sh(a_ic),
                              row(rw_r_k[l]), row(rw_lnx_g[l]), row(rw_lnx_b[l]))

        x1 = _mix(o_da.reshape(t, d), y_rw.reshape(t, d), g_out, rest, xf,
                  w_da_proj[l].astype(BF16), w_rw_proj[l].astype(BF16), w_mix_out[l].astype(BF16),
                  row(ln1_g[l]), row(ln1_b[l]), alpha)

        ck, cv = _mem_kv(mem, row(mem_ln_g[l]), row(mem_ln_b[l]), w_ckv[l].astype(BF16))
        x2 = _cross_attention(x1, ck, cv, w_cq[l].astype(BF16), w_co[l].astype(BF16),
                              row(ln2_g[l]), row(ln2_b[l]), seq, alpha)

        xf = _ffn(x2, w_ffn_in[l].astype(BF16), w_ffn_out[l].astype(BF16),
                  row(ln3_g[l]), row(ln3_b[l]), alpha)
    return xf.reshape(bsz, seq, d)
```

```python
import functools
import math

import jax
import jax.numpy as jnp
from jax import lax
from jax.experimental import pallas as pl
from jax.experimental.pallas import tpu as pltpu

F32 = jnp.float32
BF16 = jnp.bfloat16

D_MODEL = 1024
DA_HEADS = 8
DA_HD = 64
NUM_BUCKETS = 32
MAX_DISTANCE = 128
RW_HD = 64
RW_HEADS = D_MODEL // RW_HD
DECAY_LORA = 64
AAA_LORA = 64
GATE_LORA = 160
RW_LNX_EPS = 64e-5
MEM_TOKENS = 256
CA_HEADS = 4
CA_HD = D_MODEL // CA_HEADS
D_FF = 2816
LN_EPS = 1e-5
NEG_BIG = -1e30

ATT_TILE = 256
RW_CHUNK = 64
VMEM_LIMIT = 56 * 1024 * 1024

NT_DIMS = (((1,), (1,)), ((), ()))
TN_DIMS = (((0,), (0,)), ((), ()))


def _cparams(sem, vmem=VMEM_LIMIT):
    return pltpu.CompilerParams(dimension_semantics=sem, vmem_limit_bytes=vmem)


def _split2(a):
    hi = a.astype(BF16)
    lo = (a - hi.astype(F32)).astype(BF16)
    return hi, lo


def _dot3(a, b, dims=(((1,), (0,)), ((), ()))):
    ah, al = _split2(a)
    bh, bl = _split2(b)
    d = functools.partial(lax.dot_general, dimension_numbers=dims, preferred_element_type=F32)
    return d(ah, bh) + (d(ah, bl) + d(al, bh))


def _layer_norm_rows(z, g, b):
    mu = jnp.mean(z, axis=-1, keepdims=True)
    zc = z - mu
    var = jnp.mean(zc * zc, axis=-1, keepdims=True)
    return zc * lax.rsqrt(var + LN_EPS) * g + b


def _mm_kernel(x_ref, w_ref, o_ref, xb_ref):
    @pl.when(pl.program_id(1) == 0)
    def _():
        xb_ref[...] = x_ref[...].astype(BF16)

    o_ref[...] = jnp.dot(xb_ref[...], w_ref[...], preferred_element_type=F32).astype(o_ref.dtype)


def _matmul(x, w, out_dtype, tm, tn, name):
    t, k = x.shape
    n = w.shape[1]
    return pl.pallas_call(
        _mm_kernel,
        grid=(t // tm, n // tn),
        in_specs=[pl.BlockSpec((tm, k), lambda i, j: (i, 0)),
                  pl.BlockSpec((k, tn), lambda i, j: (0, j))],
        out_specs=pl.BlockSpec((tm, tn), lambda i, j: (i, j)),
        out_shape=jax.ShapeDtypeStruct((t, n), out_dtype),
        scratch_shapes=[pltpu.VMEM((tm, k), BF16)],
        compiler_params=_cparams(("parallel", "arbitrary")),
        name=name,
    )(x, w)


def _qkv_kernel(x_ref, wqt_ref, wk_ref, wvt_ref, qt_ref, k_ref, vt_ref):
    xb = x_ref[...].astype(BF16)
    k_ref[...] = jnp.dot(xb, wk_ref[...], preferred_element_type=F32).astype(BF16)
    qt_ref[0] = lax.dot_general(wqt_ref[...], xb, NT_DIMS, preferred_element_type=F32).astype(BF16)
    vt_ref[0] = lax.dot_general(wvt_ref[...], xb, NT_DIMS, preferred_element_type=F32).astype(BF16)


def _qkv_proj(x, wqt, wk, wvt, bsz, seq, tm=512):
    t, d = x.shape
    tiles_per_seq = seq // tm
    full = lambda a: pl.BlockSpec(a.shape, lambda i: (0,) * a.ndim)
    tblk = pl.BlockSpec((1, d, tm), lambda i: (i // tiles_per_seq, 0, i % tiles_per_seq))
    rowb = pl.BlockSpec((tm, d), lambda i: (i, 0))
    tshape = jax.ShapeDtypeStruct((bsz, d, seq), BF16)
    return pl.pallas_call(
        _qkv_kernel,
        grid=(t // tm,),
        in_specs=[rowb, full(wqt), full(wk), full(wvt)],
        out_specs=[tblk, rowb, tblk],
        out_shape=[tshape, jax.ShapeDtypeStruct((t, d), BF16), tshape],
        compiler_params=_cparams(("parallel",)),
        name="in_proj_qkv",
    )(x, wqt, wk, wvt)


def _t5_bucket(rel):
    n = jnp.maximum(rel, 0)
    max_exact = NUM_BUCKETS // 2
    nf = jnp.maximum(n, 1).astype(jnp.float32)
    large = max_exact + (jnp.log(nf / max_exact) / math.log(MAX_DISTANCE / max_exact)
                         * (NUM_BUCKETS - max_exact)).astype(jnp.int32)
    large = jnp.minimum(large, NUM_BUCKETS - 1)
    return jnp.where(n < max_exact, n, large)


def _attn_bias_tiles(rel_bias):
    ta = ATT_TILE
    tab = rel_bias.astype(F32) - rel_bias[NUM_BUCKETS - 1].astype(F32)[None, :]
    rel = jnp.arange(ta)[None, :] - jnp.arange(ta)[:, None]

    def lookup(r):
        onehot = (_t5_bucket(r)[:, :, None] == jnp.arange(NUM_BUCKETS)[None, None, :]).astype(F32)
        return jnp.einsum('kqb,bh->hkq', onehot, tab, precision=lax.Precision.HIGHEST)

    diag = jnp.where((rel >= 0)[None], lookup(rel), NEG_BIG)
    return jnp.stack([lookup(rel + ta), diag], axis=1)


def _attn_kernel(lamv_ref, qt_ref, k_ref, vt_ref, bias_ref, g_ref, o_ref, m_sc, l_sc, acc_sc, *, lam_init):
    ta = ATT_TILE
    qi = pl.program_id(2)

    lv = lamv_ref[...]
    lam = (jnp.exp(jnp.sum(lv[0:1] * lv[1:2], axis=-1, keepdims=True))
           - jnp.exp(jnp.sum(lv[2:3] * lv[3:4], axis=-1, keepdims=True)) + lam_init)

    qt = qt_ref[0] * jnp.asarray(DA_HD ** -0.5, BF16)
    rowi = lax.broadcasted_iota(jnp.int32, qt.shape, 0)
    zero = jnp.zeros_like(qt)
    qqt = jnp.concatenate([jnp.where(rowi < DA_HD, qt, zero), jnp.where(rowi >= DA_HD, qt, zero)], axis=1)

    m_sc[...] = jnp.full(m_sc.shape, NEG_BIG, F32)
    l_sc[...] = jnp.zeros(l_sc.shape, F32)
    acc_sc[...] = jnp.zeros(acc_sc.shape, F32)

    def step(j, bias, nk=ta):
        k0 = pl.multiple_of(j * nk, nk)
        k = k_ref[0, pl.ds(k0, nk), :]
        vt = vt_ref[0, :, pl.ds(k0, nk)]
        s = jnp.dot(k, qqt, preferred_element_type=F32)
        if bias is not None:
            s = s + jnp.concatenate([bias, bias], axis=1)
        m_prev = m_sc[...]
        m_new = jnp.maximum(m_prev, jnp.max(s, axis=0, keepdims=True))
        alpha = jnp.exp(m_prev - m_new)
        p = jnp.exp(s - m_new)
        l_sc[...] = alpha * l_sc[...] + jnp.sum(p, axis=0, keepdims=True)
        acc_sc[...] = alpha * acc_sc[...] + jnp.dot(vt, p.astype(BF16), preferred_element_type=F32)
        m_sc[...] = m_new

    nfar = jnp.maximum(qi - 1, 0)

    def far_body(j, carry):
        step(j, None, 2 * ta)
        return carry

    lax.fori_loop(0, nfar // 2, far_body, 0)

    @pl.when(nfar % 2 == 1)
    def _():
        step(nfar - 1, None)

    @pl.when(qi >= 1)
    def _():
        step(qi - 1, bias_ref[0, 0])

    step(qi, bias_ref[0, 1])

    acc = acc_sc[...] * (1.0 / l_sc[...])
    ot = acc[:, 0:ta] - lam * acc[:, ta:2 * ta]
    ms = jnp.mean(ot * ot, axis=0, keepdims=True)
    ot = ot * (lax.rsqrt(ms + LN_EPS) * (1.0 - lam_init))
    o_ref[0] = (ot.T * g_ref[...]).astype(o_ref.dtype)


def _diff_attention(qt, k, vt, lamv, bias_tiles, subln_g, lam_init):
    b, s, _ = k.shape
    ta = ATT_TILE
    h = DA_HEADS
    return pl.pallas_call(
        functools.partial(_attn_kernel, lam_init=lam_init),
        grid=(b, h, s // ta),
        in_specs=[
            pl.BlockSpec((4, DA_HD), lambda bi, hi, qi: (0, 0)),
            pl.BlockSpec((1, 128, ta), lambda bi, hi, qi: (bi, hi, qi)),
            pl.BlockSpec((1, s, 128), lambda bi, hi, qi: (bi, 0, hi)),
            pl.BlockSpec((1, 128, s), lambda bi, hi, qi: (bi, hi, 0)),
            pl.BlockSpec((1, 2, ta, ta), lambda bi, hi, qi: (hi, 0, 0, 0)),
            pl.BlockSpec((1, 128), lambda bi, hi, qi: (0, 0)),
        ],
        out_specs=pl.BlockSpec((1, ta, 128), lambda bi, hi, qi: (bi, qi, hi)),
        out_shape=jax.ShapeDtypeStruct((b, s, h * 128), BF16),
        scratch_shapes=[pltpu.VMEM((1, 2 * ta), F32), pltpu.VMEM((1, 2 * ta), F32),
                        pltpu.VMEM((128, 2 * ta), F32)],
        compiler_params=_cparams(("parallel", "parallel", "arbitrary")),
        name="diff_attention",
    )(lamv, qt, k, vt, bias_tiles, subln_g)


def _rw_prep_kernel(r_ref, k_ref, v_ref, lo_ref, rp_ref, kp_ref, vp_ref, lop_ref,
                    mu_ref, mulo_ref, w0_ref, w2h_ref, w2l_ref, a0_ref, a2h_ref, a2l_ref, g2_ref,
                    kk_ref, ka_ref,
                    r_o, lw_o, km_o, v_o, kk_o, a_o, g_o, *, tiles_per_seq):
    i = pl.program_id(0)
    keep = (i % tiles_per_seq != 0).astype(F32)

    def shifted(cur_ref, prev_ref, mu):
        p = cur_ref[...]
        row = lax.broadcasted_iota(jnp.int32, p.shape, 0)
        prev_last = prev_ref[7:8, :] * keep
        pm1 = jnp.where(row == 0, prev_last, pltpu.roll(p, 1, 0))
        return p + mu * (pm1 - p)

    r = shifted(r_ref, rp_ref, mu_ref[0:1])
    k = shifted(k_ref, kp_ref, mu_ref[1:2])
    v = shifted(v_ref, vp_ref, mu_ref[2:3])
    lo = shifted(lo_ref, lop_ref, mulo_ref[...])

    def dot3w(a, wh_ref, wl_ref):
        ah, al = _split2(a)
        wh = wh_ref[...]
        d = functools.partial(jnp.dot, preferred_element_type=F32)
        return d(ah, wh) + (d(ah, wl_ref[...]) + d(al, wh))

    zw = w0_ref[...] + dot3w(jnp.tanh(lo[:, 0:128]), w2h_ref, w2l_ref)
    nz = -zw
    softplus = jnp.maximum(nz, 0.0) + jnp.log(1.0 + jnp.exp(-jnp.abs(nz)))
    w_log = -softplus - 0.5
    lw_o[...] = -jnp.exp(w_log)
    a_ic = jax.nn.sigmoid(a0_ref[...] + dot3w(lo[:, 128:256], a2h_ref, a2l_ref))
    g_o[...] = jnp.dot(jax.nn.sigmoid(lo[:, 256:512]).astype(BF16), g2_ref[...], preferred_element_type=F32)
    r_o[...] = r
    v_o[...] = v
    kk_o[...] = k * kk_ref[...]
    km_o[...] = k * (1.0 + (a_ic - 1.0) * ka_ref[...])
    a_o[...] = a_ic


def _rw_prep(rest, seq, mu_rkv, mu_lo, w0, w2h, w2l, a0, a2h, a2l, g2, k_k, k_a, tm=256):
    t = rest.shape[0]
    d = D_MODEL
    cur = lambda c, w: pl.BlockSpec((tm, w), lambda i: (i, c))
    prev = lambda c, w: pl.BlockSpec((8, w), lambda i: (jnp.maximum(i * (tm // 8) - 1, 0), c))
    full = lambda a: pl.BlockSpec(a.shape, lambda i: (0,) * a.ndim)
    params = (mu_rkv, mu_lo, w0, w2h, w2l, a0, a2h, a2l, g2, k_k, k_a)
    out = jax.ShapeDtypeStruct((t, d), F32)
    return pl.pallas_call(
        functools.partial(_rw_prep_kernel, tiles_per_seq=seq // tm),
        grid=(t // tm,),
        in_specs=[cur(0, d), cur(1, d), cur(2, d), cur(10, 512),
                  prev(0, d), prev(1, d), prev(2, d), prev(10, 512)] + [full(a) for a in params],
        out_specs=[pl.BlockSpec((tm, d), lambda i: (i, 0))] * 7,
        out_shape=[out] * 7,
        compiler_params=_cparams(("parallel",)),
        name="rwkv_prep",
    )(rest, rest, rest, rest, rest, rest, rest, rest, *params)


def _dot(a, b, dims=(((1,), (0,)), ((), ()))):
    return lax.dot_general(a.astype(BF16), b.astype(BF16), dims, preferred_element_type=F32)


def _dot_split_lhs(a, b_bf16):
    ah, al = _split2(a)
    return (jnp.dot(ah, b_bf16, preferred_element_type=F32) + jnp.dot(al, b_bf16, preferred_element_type=F32))


def _pair_consts():
    c, n = RW_CHUNK, RW_HD
    lane = lax.broadcasted_iota(jnp.int32, (c, 2 * n), 1)
    row = lax.broadcasted_iota(jnp.int32, (2 * c, 2 * c), 0)
    col = lax.broadcasted_iota(jnp.int32, (2 * c, 2 * c), 1)
    same_head = (row // n) == (col // n)
    return lane < n, same_head.astype(BF16)


def _expand(x2, head_a):
    zero = jnp.zeros_like(x2)
    return jnp.concatenate([jnp.where(head_a, x2, zero), jnp.where(head_a, zero, x2)], axis=0)


def _compact(xm):
    return xm[0:RW_CHUNK] + xm[RW_CHUNK:2 * RW_CHUNK]


def _rw_intra_kernel(r_ref, lw_ref, km_ref, v_ref, kk_ref, a_ref, rk_ref,
                     pt_ref, qt_ref, rh_ref, y0_ref, bon_ref):
    c, n = RW_CHUNK, RW_HD
    head_a, seg_ones = _pair_consts()
    row = lax.broadcasted_iota(jnp.int32, (2 * c, 2 * c), 0)
    col = lax.broadcasted_iota(jnp.int32, (2 * c, 2 * c), 1)
    strict = (row % c) > (col % c)
    incl = (row % c) >= (col % c)
    trow = lax.broadcasted_iota(jnp.int32, (c, c), 0)
    tcol = lax.broadcasted_iota(jnp.int32, (c, c), 1)
    tri_ones = (trow >= tcol).astype(BF16)
    diag2 = lax.broadcasted_iota(jnp.int32, (c, 2 * n), 0) == (lax.broadcasted_iota(jnp.int32, (c, 2 * n), 1) % n)

    chunks = range(r_ref.shape[1] // c)
    sls = [slice(ci * c, (ci + 1) * c) for ci in chunks]
    d = functools.partial(jnp.dot, preferred_element_type=F32)

    def cumsum(lw2):
        l1 = lw2.astype(BF16)
        rem = lw2 - l1.astype(F32)
        l2 = rem.astype(BF16)
        l3 = (rem - l2.astype(F32)).astype(BF16)
        return d(tri_ones, l1) + (d(tri_ones, l2) + d(tri_ones, l3))

    cum = [cumsum(lw_ref[0, sl, :]) for sl in sls]
    nrm = [jnp.sqrt(_dot_split_lhs(kk_ref[0, sl, :] * kk_ref[0, sl, :], seg_ones)) for sl in sls]
    big, atm, vm, bdm, kdm = [], [], [], [], []
    for ci, sl in enumerate(sls):
        lw2, cm = lw_ref[0, sl, :], cum[ci]
        cum_last = cm[c - 1:c, :]
        g_inv = jnp.exp(-cm)
        g_rest = jnp.exp(cum_last - cm)
        kkn = kk_ref[0, sl, :] / jnp.maximum(nrm[ci], 1e-12)
        b2 = kkn * a_ref[0, sl, :]
        km2 = km_ref[0, sl, :]
        a_m = _expand(-kkn * jnp.exp(cm - lw2), head_a)
        r_m = _expand(r_ref[0, sl, :] * jnp.exp(cm), head_a)
        b_m = _expand(b2 * g_inv, head_a)
        k_m = _expand(km2 * g_inv, head_a)
        atm.append(a_m)
        bdm.append(_expand(b2 * g_rest, head_a))
        kdm.append(_expand(km2 * g_rest, head_a))
        vm.append(_expand(v_ref[0, sl, :], head_a))
        big.append(_dot(jnp.concatenate([a_m, r_m], axis=0), jnp.concatenate([b_m, k_m], axis=0), NT_DIMS))
    lmat = [jnp.where(strict, bg[0:128, 0:128], 0.0) for bg in big]
    m_rb = [jnp.where(incl, bg[128:256, 0:128], 0.0) for bg in big]
    mv = [_dot(jnp.concatenate([jnp.where(strict, bg[0:128, 128:256], 0.0),
                                jnp.where(incl, bg[128:256, 128:256], 0.0)], axis=0), vm[ci])
          for ci, bg in enumerate(big)]
    x = [jnp.concatenate([atm[ci], mv[ci][0:128]], axis=1) for ci in chunks]
    lp = lmat
    for it in range(6):
        x = [x[ci] + _dot(lp[ci], x[ci]) for ci in chunks]
        if it < 5:
            lp = [_dot(m, m) for m in lp]
    rb = [_dot(m_rb[ci], x[ci]) for ci in chunks]
    bx = [_dot(bdm[ci], x[ci], TN_DIMS) for ci in chunks]
    kv = [_dot(kdm[ci], vm[ci], TN_DIMS) for ci in chunks]
    for ci, sl in enumerate(sls):
        g_last = jnp.exp(cum[ci][c - 1:c, :])
        pt_ref[0, sl, :] = _compact(bx[ci][:, 0:128]) + jnp.where(diag2, g_last, 0.0)
        qt_ref[0, sl, :] = _compact(bx[ci][:, 128:256] + kv[ci])
        rh_ref[0, sl, :] = r_ref[0, sl, :] * jnp.exp(cum[ci]) + _compact(rb[ci][:, 0:128])
        y0_ref[0, sl, :] = _compact(rb[ci][:, 128:256] + mv[ci][128:256])
        bon_ref[0, sl, :] = (_dot_split_lhs(r_ref[0, sl, :] * km_ref[0, sl, :] * rk_ref[...], seg_ones)
                             * v_ref[0, sl, :])


def _rw_intra(r, lw, km, v, kk, a_ic, r_k, tb=512):
    b, s, d = r.shape
    blk = pl.BlockSpec((1, tb, 128), lambda bi, hp, ti: (bi, ti, hp))
    par = pl.BlockSpec((1, 128), lambda bi, hp, ti: (0, hp))
    out = jax.ShapeDtypeStruct((b, s, d), F32)
    return pl.pallas_call(
        _rw_intra_kernel,
        grid=(b, d // 128, s // tb),
        in_specs=[blk] * 6 + [par],
        out_specs=[blk] * 5,
        out_shape=[out] * 5,
        compiler_params=_cparams(("parallel", "parallel", "parallel")),
        name="rwkv_intra",
    )(r, lw, km, v, kk, a_ic, r_k)


def _rw_scan_kernel(pt_ref, qt_ref, rh_ref, y0_ref, y_ref, z_sc):
    c, n = RW_CHUNK, RW_HD
    pairs = range(z_sc.shape[0])
    head_a, _ = _pair_consts()
    lns = [slice(hp * 2 * n, (hp + 1) * 2 * n) for hp in pairs]

    @pl.when(pl.program_id(1) == 0)
    def _():
        z_sc[...] = jnp.zeros(z_sc.shape, F32)

    z = [z_sc[hp] for hp in pairs]
    for ci in range(pt_ref.shape[1] // c):
        sl = slice(ci * c, (ci + 1) * c)
        y = [_dot(rh_ref[0, sl, lns[hp]], z[hp]) for hp in pairs]
        z = [_dot3(_expand(pt_ref[0, sl, lns[hp]], head_a), z[hp]) + _expand(qt_ref[0, sl, lns[hp]], head_a)
             for hp in pairs]
        for hp in pairs:
            y_ref[0, sl, lns[hp]] = y[hp] + y0_ref[0, sl, lns[hp]]
    for hp in pairs:
        z_sc[hp] = z[hp]


def _rw_scan(pt, qt, rh, y0, tb=512):
    b, s, d = pt.shape
    blk = pl.BlockSpec((1, tb, d), lambda bi, ti: (bi, ti, 0))
    return pl.pallas_call(
        _rw_scan_kernel,
        grid=(b, s // tb),
        in_specs=[blk] * 4,
        out_specs=blk,
        out_shape=jax.ShapeDtypeStruct((b, s, d), F32),
        scratch_shapes=[pltpu.VMEM((d // 128, 128, 128), F32)],
        compiler_params=_cparams(("parallel", "arbitrary")),
        name="rwkv_scan",
    )(pt, qt, rh, y0)


def _mix_kernel(oda_ref, yrw_ref, bon_ref, gout_ref, ga_ref, gb_ref, x_ref, wda_ref, wrw_ref, wmix_ref,
                gng_ref, gnb_ref, lng_ref, lnb_ref, o_ref, *, alpha):
    y_da = jnp.dot(oda_ref[...], wda_ref[...], preferred_element_type=F32)
    _, seg_ones = _pair_consts()
    parts = []
    for hp in range(RW_HEADS // 2):
        ln = slice(hp * 2 * RW_HD, (hp + 1) * 2 * RW_HD)
        y = yrw_ref[:, ln]
        mu = _dot_split_lhs(y, seg_ones) * (1.0 / RW_HD)
        yc = y - mu
        var = _dot_split_lhs(yc * yc, seg_ones) * (1.0 / RW_HD)
        parts.append(yc * lax.rsqrt(var + RW_LNX_EPS))
    yn = jnp.concatenate(parts, axis=1) * gng_ref[...] + gnb_ref[...] + bon_ref[...]
    y_rw = jnp.dot((yn * gout_ref[...]).astype(BF16), wrw_ref[...], preferred_element_type=F32)
    mixed = jax.nn.sigmoid(ga_ref[...]) * y_da + jax.nn.sigmoid(gb_ref[...]) * y_rw
    z = alpha * x_ref[...] + jnp.dot(mixed.astype(BF16), wmix_ref[...], preferred_element_type=F32)
    o_ref[...] = _layer_norm_rows(z, lng_ref[...], lnb_ref[...])


def _mix(o_da, y_rw, bonus, g_out, rest, x, w_da, w_rw, w_mix, gn_g, gn_b, ln_g, ln_b, alpha, tm=512):
    t, d = x.shape
    rowb = lambda c: pl.BlockSpec((tm, d), lambda i: (i, c))
    full = lambda a: pl.BlockSpec(a.shape, lambda i: (0,) * a.ndim)
    return pl.pallas_call(
        functools.partial(_mix_kernel, alpha=alpha),
        grid=(t // tm,),
        in_specs=[rowb(0), rowb(0), rowb(0), rowb(0), rowb(3), rowb(4), rowb(0),
                  full(w_da), full(w_rw), full(w_mix), full(gn_g), full(gn_b), full(ln_g), full(ln_b)],
        out_specs=rowb(0),
        out_shape=jax.ShapeDtypeStruct((t, d), F32),
        compiler_params=_cparams(("parallel",)),
        name="mix_merge_ln1",
    )(o_da, y_rw, bonus, g_out, rest, rest, x, w_da, w_rw, w_mix, gn_g, gn_b, ln_g, ln_b)


def _mem_kv_kernel(mem_ref, g_ref, b_ref, w_ref, ck_ref, cv_ref):
    m = _layer_norm_rows(mem_ref[0], g_ref[...], b_ref[...])
    kv = jnp.dot(m.astype(BF16), w_ref[...], preferred_element_type=F32)
    ck_ref[0] = kv[:, :D_MODEL].astype(BF16)
    cv_ref[0] = kv[:, D_MODEL:].astype(BF16)


def _mem_kv(mem, g, b, w_ckv):
    bsz, m, d = mem.shape
    full = lambda a: pl.BlockSpec(a.shape, lambda i: (0,) * a.ndim)
    blk = pl.BlockSpec((1, m, d), lambda i: (i, 0, 0))
    return pl.pallas_call(
        _mem_kv_kernel,
        grid=(bsz,),
        in_specs=[blk, full(g), full(b), full(w_ckv)],
        out_specs=[blk, blk],
        out_shape=[jax.ShapeDtypeStruct((bsz, m, d), BF16)] * 2,
        compiler_params=_cparams(("parallel",)),
        name="mem_kv",
    )(mem, g, b, w_ckv)


def _cross_kernel(x_ref, ck_ref, cv_ref, wq_ref, wo_ref, lng_ref, lnb_ref, o_ref, *, alpha):
    x = x_ref[...]
    cq = jnp.dot(x.astype(BF16), wq_ref[...], preferred_element_type=F32)
    cq = (cq * (CA_HD ** -0.5)).astype(BF16)
    outs = []
    for h in range(CA_HEADS):
        sl = slice(h * CA_HD, (h + 1) * CA_HD)
        s = lax.dot_general(cq[:, sl], ck_ref[0, :, sl], NT_DIMS, preferred_element_type=F32)
        s = s - jnp.max(s, axis=-1, keepdims=True)
        p = jnp.exp(s)
        p = p / jnp.sum(p, axis=-1, keepdims=True)
        outs.append(jnp.dot(p.astype(BF16), cv_ref[0, :, sl], preferred_element_type=F32))
    co = jnp.dot(jnp.concatenate(outs, axis=1).astype(BF16), wo_ref[...], preferred_element_type=F32)
    o_ref[...] = _layer_norm_rows(alpha * x + co, lng_ref[...], lnb_ref[...])


def _cross_attention(x, ck, cv, w_cq, w_co, ln_g, ln_b, seq, alpha, tm=512):
    t, d = x.shape
    tiles_per_seq = seq // tm
    full = lambda a: pl.BlockSpec(a.shape, lambda i: (0,) * a.ndim)
    rowb = pl.BlockSpec((tm, d), lambda i: (i, 0))
    memb = pl.BlockSpec((1,) + ck.shape[1:], lambda i: (i // tiles_per_seq, 0, 0))
    return pl.pallas_call(
        functools.partial(_cross_kernel, alpha=alpha),
        grid=(t // tm,),
        in_specs=[rowb, memb, memb, full(w_cq), full(w_co), full(ln_g), full(ln_b)],
        out_specs=rowb,
        out_shape=jax.ShapeDtypeStruct((t, d), F32),
        compiler_params=_cparams(("parallel",)),
        name="cross_attention_ln2",
    )(x, ck, cv, w_cq, w_co, ln_g, ln_b)


def _ffn_kernel(x_ref, wg_ref, wu_ref, wo_ref, lng_ref, lnb_ref, o_ref, xb_ref, acc_ref, *, alpha):
    j = pl.program_id(1)

    @pl.when(j == 0)
    def _():
        xb_ref[...] = x_ref[...].astype(BF16)
        acc_ref[...] = jnp.zeros(acc_ref.shape, F32)

    xb = xb_ref[...]
    gate = jnp.dot(xb, wg_ref[...], preferred_element_type=F32)
    up = jnp.dot(xb, wu_ref[...], preferred_element_type=F32)
    h = (gate * jax.nn.sigmoid(gate) * up).astype(BF16)
    acc_ref[...] += jnp.dot(h, wo_ref[...], preferred_element_type=F32)

    @pl.when(j == pl.num_programs(1) - 1)
    def _():
        o_ref[...] = _layer_norm_rows(alpha * x_ref[...] + acc_ref[...], lng_ref[...], lnb_ref[...])


def _ffn(x, w_in, w_out, ln_g, ln_b, alpha, tm=512, tf=1408):
    t, d = x.shape
    nf = D_FF // tf
    full = lambda a: pl.BlockSpec(a.shape, lambda i, j: (0,) * a.ndim)
    rowb = pl.BlockSpec((tm, d), lambda i, j: (i, 0))
    return pl.pallas_call(
        functools.partial(_ffn_kernel, alpha=alpha),
        grid=(t // tm, nf),
        in_specs=[rowb,
                  pl.BlockSpec((d, tf), lambda i, j: (0, j)),
                  pl.BlockSpec((d, tf), lambda i, j: (0, nf + j)),
                  pl.BlockSpec((tf, d), lambda i, j: (j, 0)),
                  full(ln_g), full(ln_b)],
        out_specs=rowb,
        out_shape=jax.ShapeDtypeStruct((t, d), F32),
        scratch_shapes=[pltpu.VMEM((tm, d), BF16), pltpu.VMEM((tm, d), F32)],
        compiler_params=_cparams(("parallel", "arbitrary")),
        name="swiglu_ffn_ln3",
    )(x, w_in, w_in, w_out, ln_g, ln_b)


def _pad_cols(a, width):
    return jnp.pad(a, ((0, 0), (0, width - a.shape[1])))


def _pad_rows(a, height):
    return jnp.pad(a, ((0, height - a.shape[0]), (0, 0)))


def kernel(x, mem, rel_bias, w_in, shift_mu, lambda_q1, lambda_k1, lambda_q2, lambda_k2, da_subln_g, w_da_proj, rw_w0, rw_w2, rw_a0, rw_a2, rw_g2, rw_k_k, rw_k_a, rw_r_k, rw_lnx_g, rw_lnx_b, w_rw_proj, w_mix_out, ln1_g, ln1_b, mem_ln_g, mem_ln_b, w_cq, w_ckv, w_co, ln2_g, ln2_b, w_ffn_in, w_ffn_out, ln3_g, ln3_b):
    bsz, seq, d = x.shape
    depth = w_in.shape[0]
    alpha = (2.0 * depth) ** 0.25
    t = bsz * seq
    row = lambda a: a.reshape(1, -1).astype(F32)
    bias_tiles = _attn_bias_tiles(rel_bias)
    xf = x.reshape(t, d)

    qkv_w = 3 * D_MODEL
    rw0 = qkv_w
    lo0 = rw0 + 3 * D_MODEL
    gate0 = lo0 + DECAY_LORA + AAA_LORA + GATE_LORA

    for l in range(depth):
        w = w_in[l]
        w_qt = w[:, :D_MODEL].T.astype(BF16)
        w_k = w[:, D_MODEL:2 * D_MODEL].astype(BF16)
        w_vt = w[:, 2 * D_MODEL:qkv_w].T.astype(BF16)
        w_rest = jnp.concatenate([
            w[:, rw0:lo0], w[:, gate0:gate0 + 2 * D_MODEL],
            _pad_cols(w[:, lo0:lo0 + DECAY_LORA], 128),
            _pad_cols(w[:, lo0 + DECAY_LORA:lo0 + DECAY_LORA + AAA_LORA], 128),
            _pad_cols(w[:, lo0 + DECAY_LORA + AAA_LORA:gate0], 256)], axis=1).astype(BF16)
        mu = shift_mu[l].astype(F32)
        mu_rkv = mu[:3 * D_MODEL].reshape(3, D_MODEL)
        o1 = 3 * D_MODEL
        mu_lo = jnp.concatenate([
            _pad_cols(mu[None, o1:o1 + DECAY_LORA], 128),
            _pad_cols(mu[None, o1 + DECAY_LORA:o1 + DECAY_LORA + AAA_LORA], 128),
            _pad_cols(mu[None, o1 + DECAY_LORA + AAA_LORA:], 256)], axis=1)
        w2 = _pad_rows(rw_w2[l].astype(F32), 128)
        a2 = _pad_rows(rw_a2[l].astype(F32), 128)
        w2h, w2l = _split2(w2)
        a2h, a2l = _split2(a2)
        g2 = _pad_rows(rw_g2[l], 256).astype(BF16)

        qt, kda, vt = _qkv_proj(xf, w_qt, w_k, w_vt, bsz, seq)
        rest = _matmul(xf, w_rest, F32, 1024, 512, "in_proj_rest")

        lam_init = 0.8 - 0.6 * math.exp(-0.3 * l)
        lamv = jnp.stack([lambda_q1[l], lambda_k1[l], lambda_q2[l], lambda_k2[l]]).astype(F32)
        o_da = _diff_attention(qt, kda.reshape(bsz, seq, d), vt, lamv, bias_tiles, row(da_subln_g[l]), lam_init)

        r, lw, km, v, kk, a_ic, g_out = _rw_prep(
            rest, seq, mu_rkv, mu_lo, row(rw_w0[l]), w2h, w2l, row(rw_a0[l]), a2h, a2l, g2,
            row(rw_k_k[l]), row(rw_k_a[l]))
        sh = lambda a: a.reshape(bsz, seq, d)
        pt, qt_rw, rh, y0, bon = _rw_intra(sh(r), sh(lw), sh(km), sh(v), sh(kk), sh(a_ic), row(rw_r_k[l]))
        y_rw = _rw_scan(pt, qt_rw, rh, y0)

        x1 = _mix(o_da.reshape(t, d), y_rw.reshape(t, d), bon.reshape(t, d), g_out, rest, xf,
                  w_da_proj[l].astype(BF16), w_rw_proj[l].astype(BF16), w_mix_out[l].astype(BF16),
                  row(rw_lnx_g[l]), row(rw_lnx_b[l]), row(ln1_g[l]), row(ln1_b[l]), alpha)

        ck, cv = _mem_kv(mem, row(mem_ln_g[l]), row(mem_ln_b[l]), w_ckv[l].astype(BF16))
        x2 = _cross_attention(x1, ck, cv, w_cq[l].astype(BF16), w_co[l].astype(BF16),
                              row(ln2_g[l]), row(ln2_b[l]), seq, alpha)

        xf = _ffn(x2, w_ffn_in[l].astype(BF16), w_ffn_out[l].astype(BF16),
                  row(ln3_g[l]), row(ln3_b[l]), alpha)
    return xf.reshape(bsz, seq, d)
```

```python
import functools
import math

import jax
import jax.numpy as jnp
from jax import lax
from jax.experimental import pallas as pl
from jax.experimental.pallas import tpu as pltpu

F32 = jnp.float32
BF16 = jnp.bfloat16

D_MODEL = 1024
DA_HEADS = 8
DA_HD = 64
NUM_BUCKETS = 32
MAX_DISTANCE = 128
RW_HD = 64
RW_HEADS = D_MODEL // RW_HD
DECAY_LORA = 64
AAA_LORA = 64
GATE_LORA = 160
RW_LNX_EPS = 64e-5
MEM_TOKENS = 256
CA_HEADS = 4
CA_HD = D_MODEL // CA_HEADS
D_FF = 2816
LN_EPS = 1e-5
NEG_BIG = -1e30

LOG2E = math.log2(math.e)
ATT_TILE = 256
ATT_HEADS_PER_STEP = 2
RW_CHUNK = 64
VMEM_LIMIT = 56 * 1024 * 1024

NT_DIMS = (((1,), (1,)), ((), ()))
TN_DIMS = (((0,), (0,)), ((), ()))


def _cparams(sem, vmem=VMEM_LIMIT):
    return pltpu.CompilerParams(dimension_semantics=sem, vmem_limit_bytes=vmem)


def _split2(a):
    hi = a.astype(BF16)
    lo = (a - hi.astype(F32)).astype(BF16)
    return hi, lo


def _dot3(a, b, dims=(((1,), (0,)), ((), ()))):
    ah, al = _split2(a)
    bh, bl = _split2(b)
    d = functools.partial(lax.dot_general, dimension_numbers=dims, preferred_element_type=F32)
    return d(ah, bh) + (d(ah, bl) + d(al, bh))


def _layer_norm_rows(z, g, b):
    mu = jnp.mean(z, axis=-1, keepdims=True)
    zc = z - mu
    var = jnp.mean(zc * zc, axis=-1, keepdims=True)
    return zc * lax.rsqrt(var + LN_EPS) * g + b


def _mm_kernel(x_ref, w_ref, o_ref, xb_ref):
    @pl.when(pl.program_id(1) == 0)
    def _():
        xb_ref[...] = x_ref[...].astype(BF16)

    o_ref[...] = jnp.dot(xb_ref[...], w_ref[...], preferred_element_type=F32).astype(o_ref.dtype)


def _matmul(x, w, out_dtype, tm, tn, name):
    t, k = x.shape
    n = w.shape[1]
    return pl.pallas_call(
        _mm_kernel,
        grid=(t // tm, n // tn),
        in_specs=[pl.BlockSpec((tm, k), lambda i, j: (i, 0)),
                  pl.BlockSpec((k, tn), lambda i, j: (0, j))],
        out_specs=pl.BlockSpec((tm, tn), lambda i, j: (i, j)),
        out_shape=jax.ShapeDtypeStruct((t, n), out_dtype),
        scratch_shapes=[pltpu.VMEM((tm, k), BF16)],
        compiler_params=_cparams(("parallel", "arbitrary")),
        name=name,
    )(x, w)


def _qkv_kernel(x_ref, wqt_ref, wk_ref, wvt_ref, qt_ref, k_ref, vt_ref):
    xb = x_ref[...].astype(BF16)
    k_ref[...] = jnp.dot(xb, wk_ref[...], preferred_element_type=F32).astype(BF16)
    qt = lax.dot_general(wqt_ref[...], xb, NT_DIMS, preferred_element_type=F32)
    qt_ref[0] = (qt * (DA_HD ** -0.5 * LOG2E)).astype(BF16)
    vt_ref[0] = lax.dot_general(wvt_ref[...], xb, NT_DIMS, preferred_element_type=F32).astype(BF16)


def _qkv_proj(x, wqt, wk, wvt, bsz, seq, tm=512):
    t, d = x.shape
    tiles_per_seq = seq // tm
    full = lambda a: pl.BlockSpec(a.shape, lambda i: (0,) * a.ndim)
    tblk = pl.BlockSpec((1, d, tm), lambda i: (i // tiles_per_seq, 0, i % tiles_per_seq))
    rowb = pl.BlockSpec((tm, d), lambda i: (i, 0))
    tshape = jax.ShapeDtypeStruct((bsz, d, seq), BF16)
    return pl.pallas_call(
        _qkv_kernel,
        grid=(t // tm,),
        in_specs=[rowb, full(wqt), full(wk), full(wvt)],
        out_specs=[tblk, rowb, tblk],
        out_shape=[tshape, jax.ShapeDtypeStruct((t, d), BF16), tshape],
        compiler_params=_cparams(("parallel",)),
        name="in_proj_qkv",
    )(x, wqt, wk, wvt)


def _t5_bucket(rel):
    n = jnp.maximum(rel, 0)
    max_exact = NUM_BUCKETS // 2
    nf = jnp.maximum(n, 1).astype(jnp.float32)
    large = max_exact + (jnp.log(nf / max_exact) / math.log(MAX_DISTANCE / max_exact)
                         * (NUM_BUCKETS - max_exact)).astype(jnp.int32)
    large = jnp.minimum(large, NUM_BUCKETS - 1)
    return jnp.where(n < max_exact, n, large)


def _attn_bias_tiles(rel_bias):
    ta = ATT_TILE
    tab = rel_bias.astype(F32) - rel_bias[NUM_BUCKETS - 1].astype(F32)[None, :]
    rel = jnp.arange(ta)[None, :] - jnp.arange(ta)[:, None]

    def lookup(r):
        onehot = (_t5_bucket(r)[:, :, None] == jnp.arange(NUM_BUCKETS)[None, None, :]).astype(F32)
        return jnp.einsum('kqb,bh->hkq', onehot, tab, precision=lax.Precision.HIGHEST)

    diag = jnp.where((rel >= 0)[None], lookup(rel), NEG_BIG)
    return jnp.concatenate([lookup(rel + ta), diag], axis=1) * LOG2E


def _attn_kernel(lamv_ref, qt_ref, k_ref, vt_ref, bias_ref, g_ref, o_ref, m_sc, l_sc, acc_sc, *, lam_init):
    ta = ATT_TILE
    hw = 2 * DA_HD
    heads = range(ATT_HEADS_PER_STEP)
    hrows = [slice(hh * hw, (hh + 1) * hw) for hh in heads]
    qi = pl.program_id(2)

    lv = lamv_ref[...]
    lam = (jnp.exp(jnp.sum(lv[0:1] * lv[1:2], axis=-1, keepdims=True))
           - jnp.exp(jnp.sum(lv[2:3] * lv[3:4], axis=-1, keepdims=True)) + lam_init)

    rowi = lax.broadcasted_iota(jnp.int32, (hw, ta), 0)
    qqt = []
    for hh in heads:
        qt = qt_ref[0, hrows[hh], :]
        zero = jnp.zeros_like(qt)
        qqt.append(jnp.concatenate([jnp.where(rowi < DA_HD, qt, zero), jnp.where(rowi >= DA_HD, qt, zero)], axis=1))

    m_sc[...] = jnp.full(m_sc.shape, NEG_BIG, F32)
    l_sc[...] = jnp.zeros(l_sc.shape, F32)
    acc_sc[...] = jnp.zeros(acc_sc.shape, F32)

    def step(k0, nk, biased):
        k0 = pl.multiple_of(k0, ta)
        s = [jnp.dot(k_ref[0, pl.ds(k0, nk), hrows[hh]], qqt[hh], preferred_element_type=F32) for hh in heads]
        if biased:
            bias = [bias_ref[hh, 2 * ta - nk:2 * ta, :] for hh in heads]
            s = [s[hh] + jnp.concatenate([bias[hh], bias[hh]], axis=1) for hh in heads]
        m_prev = [m_sc[hh] for hh in heads]
        m_new = [jnp.maximum(m_prev[hh], jnp.max(s[hh], axis=0, keepdims=True)) for hh in heads]
        alpha = [jnp.exp2(m_prev[hh] - m_new[hh]) for hh in heads]
        p = [jnp.exp2(s[hh] - m_new[hh]) for hh in heads]
        pv = [jnp.dot(vt_ref[0, hrows[hh], pl.ds(k0, nk)], p[hh].astype(BF16), preferred_element_type=F32)
              for hh in heads]
        for hh in heads:
            l_sc[hh] = alpha[hh] * l_sc[hh] + jnp.sum(p[hh], axis=0, keepdims=True)
            acc_sc[hh] = alpha[hh] * acc_sc[hh] + pv[hh]
            m_sc[hh] = m_new[hh]

    nfar = jnp.maximum(qi - 1, 0)

    def far_body(j, carry):
        step(j * (2 * ta), 2 * ta, False)
        return carry

    lax.fori_loop(0, nfar // 2, far_body, 0)

    @pl.when(nfar % 2 == 1)
    def _():
        step((nfar - 1) * ta, ta, False)

    @pl.when(qi >= 1)
    def _():
        step((qi - 1) * ta, 2 * ta, True)

    @pl.when(qi == 0)
    def _():
        step(0, ta, True)

    for hh in heads:
        acc = acc_sc[hh] * (1.0 / l_sc[hh])
        ot = acc[:, 0:ta] - lam * acc[:, ta:2 * ta]
        ms = jnp.mean(ot * ot, axis=0, keepdims=True)
        ot = ot * (lax.rsqrt(ms + LN_EPS) * (1.0 - lam_init))
        o_ref[0, :, hrows[hh]] = (ot.T * g_ref[...]).astype(o_ref.dtype)


def _diff_attention(qt, k, vt, lamv, bias_tiles, subln_g, lam_init):
    b, s, _ = k.shape
    ta = ATT_TILE
    nh = ATT_HEADS_PER_STEP
    hw = 2 * DA_HD
    return pl.pallas_call(
        functools.partial(_attn_kernel, lam_init=lam_init),
        grid=(b, DA_HEADS // nh, s // ta),
        in_specs=[
            pl.BlockSpec((4, DA_HD), lambda bi, hi, qi: (0, 0)),
            pl.BlockSpec((1, nh * hw, ta), lambda bi, hi, qi: (bi, hi, qi)),
            pl.BlockSpec((1, s, nh * hw), lambda bi, hi, qi: (bi, 0, hi)),
            pl.BlockSpec((1, nh * hw, s), lambda bi, hi, qi: (bi, hi, 0)),
            pl.BlockSpec((nh, 2 * ta, ta), lambda bi, hi, qi: (hi, 0, 0)),
            pl.BlockSpec((1, hw), lambda bi, hi, qi: (0, 0)),
        ],
        out_specs=pl.BlockSpec((1, ta, nh * hw), lambda bi, hi, qi: (bi, qi, hi)),
        out_shape=jax.ShapeDtypeStruct((b, s, DA_HEADS * hw), BF16),
        scratch_shapes=[pltpu.VMEM((nh, 1, 2 * ta), F32), pltpu.VMEM((nh, 1, 2 * ta), F32),
                        pltpu.VMEM((nh, hw, 2 * ta), F32)],
        compiler_params=_cparams(("parallel", "parallel", "arbitrary")),
        name="diff_attention",
    )(lamv, qt, k, vt, bias_tiles, subln_g)


def _rw_prep_kernel(r_ref, k_ref, v_ref, lo_ref, rp_ref, kp_ref, vp_ref, lop_ref,
                    mu_ref, mulo_ref, w0_ref, w2h_ref, w2l_ref, a0_ref, a2h_ref, a2l_ref, g2_ref,
                    kk_ref, ka_ref,
                    r_o, lw_o, km_o, v_o, kk_o, a_o, g_o, *, tiles_per_seq):
    i = pl.program_id(0)
    keep = (i % tiles_per_seq != 0).astype(F32)

    def shifted(cur_ref, prev_ref, mu):
        p = cur_ref[...]
        row = lax.broadcasted_iota(jnp.int32, p.shape, 0)
        prev_last = prev_ref[7:8, :] * keep
        pm1 = jnp.where(row == 0, prev_last, pltpu.roll(p, 1, 0))
        return p + mu * (pm1 - p)

    r = shifted(r_ref, rp_ref, mu_ref[0:1])
    k = shifted(k_ref, kp_ref, mu_ref[1:2])
    v = shifted(v_ref, vp_ref, mu_ref[2:3])
    lo = shifted(lo_ref, lop_ref, mulo_ref[...])

    def dot3w(a, wh_ref, wl_ref):
        ah, al = _split2(a)
        wh = wh_ref[...]
        d = functools.partial(jnp.dot, preferred_element_type=F32)
        return d(ah, wh) + (d(ah, wl_ref[...]) + d(al, wh))

    zw = w0_ref[...] + dot3w(jnp.tanh(lo[:, 0:128]), w2h_ref, w2l_ref)
    nz = -zw
    softplus = jnp.maximum(nz, 0.0) + jnp.log(1.0 + jnp.exp(-jnp.abs(nz)))
    w_log = -softplus - 0.5
    lw_o[...] = -jnp.exp(w_log)
    a_ic = jax.nn.sigmoid(a0_ref[...] + dot3w(lo[:, 128:256], a2h_ref, a2l_ref))
    g_o[...] = jnp.dot(jax.nn.sigmoid(lo[:, 256:512]).astype(BF16), g2_ref[...], preferred_element_type=F32)
    r_o[...] = r
    v_o[...] = v
    kk_o[...] = k * kk_ref[...]
    km_o[...] = k * (1.0 + (a_ic - 1.0) * ka_ref[...])
    a_o[...] = a_ic


def _rw_prep(rest, seq, mu_rkv, mu_lo, w0, w2h, w2l, a0, a2h, a2l, g2, k_k, k_a, tm=256):
    t = rest.shape[0]
    d = D_MODEL
    cur = lambda c, w: pl.BlockSpec((tm, w), lambda i: (i, c))
    prev = lambda c, w: pl.BlockSpec((8, w), lambda i: (jnp.maximum(i * (tm // 8) - 1, 0), c))
    full = lambda a: pl.BlockSpec(a.shape, lambda i: (0,) * a.ndim)
    params = (mu_rkv, mu_lo, w0, w2h, w2l, a0, a2h, a2l, g2, k_k, k_a)
    out = jax.ShapeDtypeStruct((t, d), F32)
    return pl.pallas_call(
        functools.partial(_rw_prep_kernel, tiles_per_seq=seq // tm),
        grid=(t // tm,),
        in_specs=[cur(0, d), cur(1, d), cur(2, d), cur(10, 512),
                  prev(0, d), prev(1, d), prev(2, d), prev(10, 512)] + [full(a) for a in params],
        out_specs=[pl.BlockSpec((tm, d), lambda i: (i, 0))] * 7,
        out_shape=[out] * 7,
        compiler_params=_cparams(("parallel",)),
        name="rwkv_prep",
    )(rest, rest, rest, rest, rest, rest, rest, rest, *params)


def _dot(a, b, dims=(((1,), (0,)), ((), ()))):
    return lax.dot_general(a.astype(BF16), b.astype(BF16), dims, preferred_element_type=F32)


def _dot_split_lhs(a, b_bf16):
    ah, al = _split2(a)
    return (jnp.dot(ah, b_bf16, preferred_element_type=F32) + jnp.dot(al, b_bf16, preferred_element_type=F32))


def _pair_consts():
    c, n = RW_CHUNK, RW_HD
    lane = lax.broadcasted_iota(jnp.int32, (c, 2 * n), 1)
    row = lax.broadcasted_iota(jnp.int32, (2 * c, 2 * c), 0)
    col = lax.broadcasted_iota(jnp.int32, (2 * c, 2 * c), 1)
    same_head = (row // n) == (col // n)
    return lane < n, same_head.astype(BF16)


def _expand(x2, head_a):
    zero = jnp.zeros_like(x2)
    return jnp.concatenate([jnp.where(head_a, x2, zero), jnp.where(head_a, zero, x2)], axis=0)


def _compact(xm):
    return xm[0:RW_CHUNK] + xm[RW_CHUNK:2 * RW_CHUNK]


def _rw_intra_kernel(r_ref, lw_ref, km_ref, v_ref, kk_ref, a_ref, rk_ref,
                     pt_ref, qt_ref, rh_ref, y0_ref, bon_ref):
    c, n = RW_CHUNK, RW_HD
    head_a, seg_ones = _pair_consts()
    row = lax.broadcasted_iota(jnp.int32, (2 * c, 2 * c), 0)
    col = lax.broadcasted_iota(jnp.int32, (2 * c, 2 * c), 1)
    strict = (row % c) > (col % c)
    incl = (row % c) >= (col % c)
    trow = lax.broadcasted_iota(jnp.int32, (c, c), 0)
    tcol = lax.broadcasted_iota(jnp.int32, (c, c), 1)
    tri_ones = (trow >= tcol).astype(BF16)
    diag2 = lax.broadcasted_iota(jnp.int32, (c, 2 * n), 0) == (lax.broadcasted_iota(jnp.int32, (c, 2 * n), 1) % n)

    chunks = range(r_ref.shape[1] // c)
    sls = [slice(ci * c, (ci + 1) * c) for ci in chunks]
    d = functools.partial(jnp.dot, preferred_element_type=F32)

    def cumsum(lw2):
        l1 = lw2.astype(BF16)
        rem = lw2 - l1.astype(F32)
        l2 = rem.astype(BF16)
        l3 = (rem - l2.astype(F32)).astype(BF16)
        return d(tri_ones, l1) + (d(tri_ones, l2) + d(tri_ones, l3))

    cum = [cumsum(lw_ref[0, sl, :]) for sl in sls]
    nrm = [jnp.sqrt(_dot_split_lhs(kk_ref[0, sl, :] * kk_ref[0, sl, :], seg_ones)) for sl in sls]
    big, atm, vm, bdm, kdm = [], [], [], [], []
    for ci, sl in enumerate(sls):
        lw2, cm = lw_ref[0, sl, :], cum[ci]
        cum_last = cm[c - 1:c, :]
        g_inv = jnp.exp(-cm)
        g_rest = jnp.exp(cum_last - cm)
        kkn = kk_ref[0, sl, :] / jnp.maximum(nrm[ci], 1e-12)
        b2 = kkn * a_ref[0, sl, :]
        km2 = km_ref[0, sl, :]
        a_m = _expand(-kkn * jnp.exp(cm - lw2), head_a)
        r_m = _expand(r_ref[0, sl, :] * jnp.exp(cm), head_a)
        b_m = _expand(b2 * g_inv, head_a)
        k_m = _expand(km2 * g_inv, head_a)
        atm.append(a_m)
        bdm.append(_expand(b2 * g_rest, head_a))
        kdm.append(_expand(km2 * g_rest, head_a))
        vm.append(_expand(v_ref[0, sl, :], head_a))
        big.append(_dot(jnp.concatenate([a_m, r_m], axis=0), jnp.concatenate([b_m, k_m], axis=0), NT_DIMS))
    lmat = [jnp.where(strict, bg[0:128, 0:128], 0.0) for bg in big]
    m_rb = [jnp.where(incl, bg[128:256, 0:128], 0.0) for bg in big]
    mv = [_dot(jnp.concatenate([jnp.where(strict, bg[0:128, 128:256], 0.0),
                                jnp.where(incl, bg[128:256, 128:256], 0.0)], axis=0), vm[ci])
          for ci, bg in enumerate(big)]
    x = [jnp.concatenate([atm[ci], mv[ci][0:128]], axis=1) for ci in chunks]
    lp = lmat
    for it in range(6):
        x = [x[ci] + _dot(lp[ci], x[ci]) for ci in chunks]
        if it < 5:
            lp = [_dot(m, m) for m in lp]
    rb = [_dot(m_rb[ci], x[ci]) for ci in chunks]
    bx = [_dot(bdm[ci], x[ci], TN_DIMS) for ci in chunks]
    kv = [_dot(kdm[ci], vm[ci], TN_DIMS) for ci in chunks]
    for ci, sl in enumerate(sls):
        g_last = jnp.exp(cum[ci][c - 1:c, :])
        pt_ref[0, sl, :] = _compact(bx[ci][:, 0:128]) + jnp.where(diag2, g_last, 0.0)
        qt_ref[0, sl, :] = _compact(bx[ci][:, 128:256] + kv[ci])
        rh_ref[0, sl, :] = r_ref[0, sl, :] * jnp.exp(cum[ci]) + _compact(rb[ci][:, 0:128])
        y0_ref[0, sl, :] = _compact(rb[ci][:, 128:256] + mv[ci][128:256])
        bon_ref[0, sl, :] = (_dot_split_lhs(r_ref[0, sl, :] * km_ref[0, sl, :] * rk_ref[...], seg_ones)
                             * v_ref[0, sl, :])


def _rw_intra(r, lw, km, v, kk, a_ic, r_k, tb=512):
    b, s, d = r.shape
    blk = pl.BlockSpec((1, tb, 128), lambda bi, hp, ti: (bi, ti, hp))
    par = pl.BlockSpec((1, 128), lambda bi, hp, ti: (0, hp))
    out = jax.ShapeDtypeStruct((b, s, d), F32)
    return pl.pallas_call(
        _rw_intra_kernel,
        grid=(b, d // 128, s // tb),
        in_specs=[blk] * 6 + [par],
        out_specs=[blk] * 5,
        out_shape=[out] * 5,
        compiler_params=_cparams(("parallel", "parallel", "parallel")),
        name="rwkv_intra",
    )(r, lw, km, v, kk, a_ic, r_k)


def _rw_scan_kernel(pt_ref, qt_ref, rh_ref, y0_ref, y_ref, z_sc):
    c, n = RW_CHUNK, RW_HD
    pairs = range(z_sc.shape[0])
    head_a, _ = _pair_consts()
    lns = [slice(hp * 2 * n, (hp + 1) * 2 * n) for hp in pairs]

    @pl.when(pl.program_id(1) == 0)
    def _():
        z_sc[...] = jnp.zeros(z_sc.shape, F32)

    z = [z_sc[hp] for hp in pairs]
    for ci in range(pt_ref.shape[1] // c):
        sl = slice(ci * c, (ci + 1) * c)
        y = [_dot(rh_ref[0, sl, lns[hp]], z[hp]) for hp in pairs]
        z = [_dot3(_expand(pt_ref[0, sl, lns[hp]], head_a), z[hp]) + _expand(qt_ref[0, sl, lns[hp]], head_a)
             for hp in pairs]
        for hp in pairs:
            y_ref[0, sl, lns[hp]] = y[hp] + y0_ref[0, sl, lns[hp]]
    for hp in pairs:
        z_sc[hp] = z[hp]


def _rw_scan(pt, qt, rh, y0, tb=512):
    b, s, d = pt.shape
    blk = pl.BlockSpec((1, tb, d), lambda bi, ti: (bi, ti, 0))
    return pl.pallas_call(
        _rw_scan_kernel,
        grid=(b, s // tb),
        in_specs=[blk] * 4,
        out_specs=blk,
        out_shape=jax.ShapeDtypeStruct((b, s, d), F32),
        scratch_shapes=[pltpu.VMEM((d // 128, 128, 128), F32)],
        compiler_params=_cparams(("parallel", "arbitrary")),
        name="rwkv_scan",
    )(pt, qt, rh, y0)


def _mix_kernel(oda_ref, yrw_ref, bon_ref, gout_ref, ga_ref, gb_ref, x_ref, wda_ref, wrw_ref, wmix_ref,
                gng_ref, gnb_ref, lng_ref, lnb_ref, o_ref, *, alpha):
    y_da = jnp.dot(oda_ref[...], wda_ref[...], preferred_element_type=F32)
    _, seg_ones = _pair_consts()
    parts = []
    for hp in range(RW_HEADS // 2):
        ln = slice(hp * 2 * RW_HD, (hp + 1) * 2 * RW_HD)
        y = yrw_ref[:, ln]
        mu = _dot_split_lhs(y, seg_ones) * (1.0 / RW_HD)
        yc = y - mu
        var = _dot_split_lhs(yc * yc, seg_ones) * (1.0 / RW_HD)
        parts.append(yc * lax.rsqrt(var + RW_LNX_EPS))
    yn = jnp.concatenate(parts, axis=1) * gng_ref[...] + gnb_ref[...] + bon_ref[...]
    y_rw = jnp.dot((yn * gout_ref[...]).astype(BF16), wrw_ref[...], preferred_element_type=F32)
    mixed = jax.nn.sigmoid(ga_ref[...]) * y_da + jax.nn.sigmoid(gb_ref[...]) * y_rw
    z = alpha * x_ref[...] + jnp.dot(mixed.astype(BF16), wmix_ref[...], preferred_element_type=F32)
    o_ref[...] = _layer_norm_rows(z, lng_ref[...], lnb_ref[...])


def _mix(o_da, y_rw, bonus, g_out, rest, x, w_da, w_rw, w_mix, gn_g, gn_b, ln_g, ln_b, alpha, tm=512):
    t, d = x.shape
    rowb = lambda c: pl.BlockSpec((tm, d), lambda i: (i, c))
    full = lambda a: pl.BlockSpec(a.shape, lambda i: (0,) * a.ndim)
    return pl.pallas_call(
        functools.partial(_mix_kernel, alpha=alpha),
        grid=(t // tm,),
        in_specs=[rowb(0), rowb(0), rowb(0), rowb(0), rowb(3), rowb(4), rowb(0),
                  full(w_da), full(w_rw), full(w_mix), full(gn_g), full(gn_b), full(ln_g), full(ln_b)],
        out_specs=rowb(0),
        out_shape=jax.ShapeDtypeStruct((t, d), F32),
        compiler_params=_cparams(("parallel",)),
        name="mix_merge_ln1",
    )(o_da, y_rw, bonus, g_out, rest, rest, x, w_da, w_rw, w_mix, gn_g, gn_b, ln_g, ln_b)


def _mem_kv_kernel(mem_ref, g_ref, b_ref, w_ref, ck_ref, cv_ref):
    m = _layer_norm_rows(mem_ref[0], g_ref[...], b_ref[...])
    kv = jnp.dot(m.astype(BF16), w_ref[...], preferred_element_type=F32)
    ck_ref[0] = kv[:, :D_MODEL].astype(BF16)
    cv_ref[0] = kv[:, D_MODEL:].astype(BF16)


def _mem_kv(mem, g, b, w_ckv):
    bsz, m, d = mem.shape
    full = lambda a: pl.BlockSpec(a.shape, lambda i: (0,) * a.ndim)
    blk = pl.BlockSpec((1, m, d), lambda i: (i, 0, 0))
    return pl.pallas_call(
        _mem_kv_kernel,
        grid=(bsz,),
        in_specs=[blk, full(g), full(b), full(w_ckv)],
        out_specs=[blk, blk],
        out_shape=[jax.ShapeDtypeStruct((bsz, m, d), BF16)] * 2,
        compiler_params=_cparams(("parallel",)),
        name="mem_kv",
    )(mem, g, b, w_ckv)


def _cross_kernel(x_ref, ck_ref, cv_ref, wq_ref, wo_ref, lng_ref, lnb_ref, o_ref, *, alpha):
    x = x_ref[...]
    cq = jnp.dot(x.astype(BF16), wq_ref[...], preferred_element_type=F32)
    cq = (cq * (CA_HD ** -0.5)).astype(BF16)
    outs = []
    for h in range(CA_HEADS):
        sl = slice(h * CA_HD, (h + 1) * CA_HD)
        s = lax.dot_general(cq[:, sl], ck_ref[0, :, sl], NT_DIMS, preferred_element_type=F32)
        s = s - jnp.max(s, axis=-1, keepdims=True)
        p = jnp.exp(s)
        p = p / jnp.sum(p, axis=-1, keepdims=True)
        outs.append(jnp.dot(p.astype(BF16), cv_ref[0, :, sl], preferred_element_type=F32))
    co = jnp.dot(jnp.concatenate(outs, axis=1).astype(BF16), wo_ref[...], preferred_element_type=F32)
    o_ref[...] = _layer_norm_rows(alpha * x + co, lng_ref[...], lnb_ref[...])


def _cross_attention(x, ck, cv, w_cq, w_co, ln_g, ln_b, seq, alpha, tm=512):
    t, d = x.shape
    tiles_per_seq = seq // tm
    full = lambda a: pl.BlockSpec(a.shape, lambda i: (0,) * a.ndim)
    rowb = pl.BlockSpec((tm, d), lambda i: (i, 0))
    memb = pl.BlockSpec((1,) + ck.shape[1:], lambda i: (i // tiles_per_seq, 0, 0))
    return pl.pallas_call(
        functools.partial(_cross_kernel, alpha=alpha),
        grid=(t // tm,),
        in_specs=[rowb, memb, memb, full(w_cq), full(w_co), full(ln_g), full(ln_b)],
        out_specs=rowb,
        out_shape=jax.ShapeDtypeStruct((t, d), F32),
        compiler_params=_cparams(("parallel",)),
        name="cross_attention_ln2",
    )(x, ck, cv, w_cq, w_co, ln_g, ln_b)


def _ffn_kernel(x_ref, wg_ref, wu_ref, wo_ref, lng_ref, lnb_ref, o_ref, xb_ref, acc_ref, *, alpha):
    j = pl.program_id(1)

    @pl.when(j == 0)
    def _():
        xb_ref[...] = x_ref[...].astype(BF16)
        acc_ref[...] = jnp.zeros(acc_ref.shape, F32)

    xb = xb_ref[...]
    gate = jnp.dot(xb, wg_ref[...], preferred_element_type=F32)
    up = jnp.dot(xb, wu_ref[...], preferred_element_type=F32)
    h = (gate * jax.nn.sigmoid(gate) * up).astype(BF16)
    acc_ref[...] += jnp.dot(h, wo_ref[...], preferred_element_type=F32)

    @pl.when(j == pl.num_programs(1) - 1)
    def _():
        o_ref[...] = _layer_norm_rows(alpha * x_ref[...] + acc_ref[...], lng_ref[...], lnb_ref[...])


def _ffn(x, w_in, w_out, ln_g, ln_b, alpha, tm=512, tf=1408):
    t, d = x.shape
    nf = D_FF // tf
    full = lambda a: pl.BlockSpec(a.shape, lambda i, j: (0,) * a.ndim)
    rowb = pl.BlockSpec((tm, d), lambda i, j: (i, 0))
    return pl.pallas_call(
        functools.partial(_ffn_kernel, alpha=alpha),
        grid=(t // tm, nf),
        in_specs=[rowb,
                  pl.BlockSpec((d, tf), lambda i, j: (0, j)),
                  pl.BlockSpec((d, tf), lambda i, j: (0, nf + j)),
                  pl.BlockSpec((tf, d), lambda i, j: (j, 0)),
                  full(ln_g), full(ln_b)],
        out_specs=rowb,
        out_shape=jax.ShapeDtypeStruct((t, d), F32),
        scratch_shapes=[pltpu.VMEM((tm, d), BF16), pltpu.VMEM((tm, d), F32)],
        compiler_params=_cparams(("parallel", "arbitrary")),
        name="swiglu_ffn_ln3",
    )(x, w_in, w_in, w_out, ln_g, ln_b)


def _pad_cols(a, width):
    return jnp.pad(a, ((0, 0), (0, width - a.shape[1])))


def _pad_rows(a, height):
    return jnp.pad(a, ((0, height - a.shape[0]), (0, 0)))


def kernel(x, mem, rel_bias, w_in, shift_mu, lambda_q1, lambda_k1, lambda_q2, lambda_k2, da_subln_g, w_da_proj, rw_w0, rw_w2, rw_a0, rw_a2, rw_g2, rw_k_k, rw_k_a, rw_r_k, rw_lnx_g, rw_lnx_b, w_rw_proj, w_mix_out, ln1_g, ln1_b, mem_ln_g, mem_ln_b, w_cq, w_ckv, w_co, ln2_g, ln2_b, w_ffn_in, w_ffn_out, ln3_g, ln3_b):
    bsz, seq, d = x.shape
    depth = w_in.shape[0]
    alpha = (2.0 * depth) ** 0.25
    t = bsz * seq
    row = lambda a: a.reshape(1, -1).astype(F32)
    bias_tiles = _attn_bias_tiles(rel_bias)
    xf = x.reshape(t, d)

    qkv_w = 3 * D_MODEL
    rw0 = qkv_w
    lo0 = rw0 + 3 * D_MODEL
    gate0 = lo0 + DECAY_LORA + AAA_LORA + GATE_LORA

    for l in range(depth):
        w = w_in[l]
        w_qt = w[:, :D_MODEL].T.astype(BF16)
        w_k = w[:, D_MODEL:2 * D_MODEL].astype(BF16)
        w_vt = w[:, 2 * D_MODEL:qkv_w].T.astype(BF16)
        w_rest = jnp.concatenate([
            w[:, rw0:lo0], w[:, gate0:gate0 + 2 * D_MODEL],
            _pad_cols(w[:, lo0:lo0 + DECAY_LORA], 128),
            _pad_cols(w[:, lo0 + DECAY_LORA:lo0 + DECAY_LORA + AAA_LORA], 128),
            _pad_cols(w[:, lo0 + DECAY_LORA + AAA_LORA:gate0], 256)], axis=1).astype(BF16)
        mu = shift_mu[l].astype(F32)
        mu_rkv = mu[:3 * D_MODEL].reshape(3, D_MODEL)
        o1 = 3 * D_MODEL
        mu_lo = jnp.concatenate([
            _pad_cols(mu[None, o1:o1 + DECAY_LORA], 128),
            _pad_cols(mu[None, o1 + DECAY_LORA:o1 + DECAY_LORA + AAA_LORA], 128),
            _pad_cols(mu[None, o1 + DECAY_LORA + AAA_LORA:], 256)], axis=1)
        w2 = _pad_rows(rw_w2[l].astype(F32), 128)
        a2 = _pad_rows(rw_a2[l].astype(F32), 128)
        w2h, w2l = _split2(w2)
        a2h, a2l = _split2(a2)
        g2 = _pad_rows(rw_g2[l], 256).astype(BF16)

        qt, kda, vt = _qkv_proj(xf, w_qt, w_k, w_vt, bsz, seq)
        rest = _matmul(xf, w_rest, F32, 1024, 1408, "in_proj_rest")

        lam_init = 0.8 - 0.6 * math.exp(-0.3 * l)
        lamv = jnp.stack([lambda_q1[l], lambda_k1[l], lambda_q2[l], lambda_k2[l]]).astype(F32)
        o_da = _diff_attention(qt, kda.reshape(bsz, seq, d), vt, lamv, bias_tiles, row(da_subln_g[l]), lam_init)

        r, lw, km, v, kk, a_ic, g_out = _rw_prep(
            rest, seq, mu_rkv, mu_lo, row(rw_w0[l]), w2h, w2l, row(rw_a0[l]), a2h, a2l, g2,
            row(rw_k_k[l]), row(rw_k_a[l]))
        sh = lambda a: a.reshape(bsz, seq, d)
        pt, qt_rw, rh, y0, bon = _rw_intra(sh(r), sh(lw), sh(km), sh(v), sh(kk), sh(a_ic), row(rw_r_k[l]))
        y_rw = _rw_scan(pt, qt_rw, rh, y0)

        x1 = _mix(o_da.reshape(t, d), y_rw.reshape(t, d), bon.reshape(t, d), g_out, rest, xf,
                  w_da_proj[l].astype(BF16), w_rw_proj[l].astype(BF16), w_mix_out[l].astype(BF16),
                  row(rw_lnx_g[l]), row(rw_lnx_b[l]), row(ln1_g[l]), row(ln1_b[l]), alpha)

        ck, cv = _mem_kv(mem, row(mem_ln_g[l]), row(mem_ln_b[l]), w_ckv[l].astype(BF16))
        x2 = _cross_attention(x1, ck, cv, w_cq[l].astype(BF16), w_co[l].astype(BF16),
                              row(ln2_g[l]), row(ln2_b[l]), seq, alpha)

        xf = _ffn(x2, w_ffn_in[l].astype(BF16), w_ffn_out[l].astype(BF16),
                  row(ln3_g[l]), row(ln3_b[l]), alpha)
    return xf.reshape(bsz, seq, d)
```

```python
import functools
import math

import jax
import jax.numpy as jnp
from jax import lax
from jax.experimental import pallas as pl
from jax.experimental.pallas import tpu as pltpu

F32 = jnp.float32
BF16 = jnp.bfloat16

D_MODEL = 1024
DA_HEADS = 8
DA_HD = 64
NUM_BUCKETS = 32
MAX_DISTANCE = 128
RW_HD = 64
RW_HEADS = D_MODEL // RW_HD
DECAY_LORA = 64
AAA_LORA = 64
GATE_LORA = 160
RW_LNX_EPS = 64e-5
MEM_TOKENS = 256
CA_HEADS = 4
CA_HD = D_MODEL // CA_HEADS
D_FF = 2816
LN_EPS = 1e-5
NEG_BIG = -1e30

LOG2E = math.log2(math.e)
ATT_TILE = 256
ATT_HEADS_PER_STEP = 2
ATT_ONES_ROWS = 16
RW_CHUNK = 64
VMEM_LIMIT = 56 * 1024 * 1024

NT_DIMS = (((1,), (1,)), ((), ()))
TN_DIMS = (((0,), (0,)), ((), ()))


def _cparams(sem, vmem=VMEM_LIMIT):
    return pltpu.CompilerParams(dimension_semantics=sem, vmem_limit_bytes=vmem)


def _split2(a):
    hi = a.astype(BF16)
    lo = (a - hi.astype(F32)).astype(BF16)
    return hi, lo


def _dot3(a, b, dims=(((1,), (0,)), ((), ()))):
    ah, al = _split2(a)
    bh, bl = _split2(b)
    d = functools.partial(lax.dot_general, dimension_numbers=dims, preferred_element_type=F32)
    return d(ah, bh) + (d(ah, bl) + d(al, bh))


def _layer_norm_rows(z, g, b):
    mu = jnp.mean(z, axis=-1, keepdims=True)
    zc = z - mu
    var = jnp.mean(zc * zc, axis=-1, keepdims=True)
    return zc * lax.rsqrt(var + LN_EPS) * g + b


def _mm_kernel(x_ref, w_ref, o_ref, xb_ref):
    @pl.when(pl.program_id(1) == 0)
    def _():
        xb_ref[...] = x_ref[...].astype(BF16)

    o_ref[...] = jnp.dot(xb_ref[...], w_ref[...], preferred_element_type=F32).astype(o_ref.dtype)


def _matmul(x, w, out_dtype, tm, tn, name):
    t, k = x.shape
    n = w.shape[1]
    return pl.pallas_call(
        _mm_kernel,
        grid=(t // tm, n // tn),
        in_specs=[pl.BlockSpec((tm, k), lambda i, j: (i, 0)),
                  pl.BlockSpec((k, tn), lambda i, j: (0, j))],
        out_specs=pl.BlockSpec((tm, tn), lambda i, j: (i, j)),
        out_shape=jax.ShapeDtypeStruct((t, n), out_dtype),
        scratch_shapes=[pltpu.VMEM((tm, k), BF16)],
        compiler_params=_cparams(("parallel", "arbitrary")),
        name=name,
    )(x, w)


def _qkv_kernel(x_ref, wqt_ref, wk_ref, wvt_ref, qt_ref, k_ref, vt_ref):
    xb = x_ref[...].astype(BF16)
    k_ref[...] = jnp.dot(xb, wk_ref[...], preferred_element_type=F32).astype(BF16)
    qt = lax.dot_general(wqt_ref[...], xb, NT_DIMS, preferred_element_type=F32)
    qt_ref[0] = (qt * (DA_HD ** -0.5 * LOG2E)).astype(BF16)
    vt_ref[0] = lax.dot_general(wvt_ref[...], xb, NT_DIMS, preferred_element_type=F32).astype(BF16)


def _qkv_proj(x, wqt, wk, wvt, bsz, seq, tm=512):
    t, d = x.shape
    tiles_per_seq = seq // tm
    full = lambda a: pl.BlockSpec(a.shape, lambda i: (0,) * a.ndim)
    tblk = pl.BlockSpec((1, d, tm), lambda i: (i // tiles_per_seq, 0, i % tiles_per_seq))
    rowb = pl.BlockSpec((tm, d), lambda i: (i, 0))
    tshape = jax.ShapeDtypeStruct((bsz, d, seq), BF16)
    return pl.pallas_call(
        _qkv_kernel,
        grid=(t // tm,),
        in_specs=[rowb, full(wqt), full(wk), full(wvt)],
        out_specs=[tblk, rowb, tblk],
        out_shape=[tshape, jax.ShapeDtypeStruct((t, d), BF16), tshape],
        compiler_params=_cparams(("parallel",)),
        name="in_proj_qkv",
    )(x, wqt, wk, wvt)


def _t5_bucket(rel):
    n = jnp.maximum(rel, 0)
    max_exact = NUM_BUCKETS // 2
    nf = jnp.maximum(n, 1).astype(jnp.float32)
    large = max_exact + (jnp.log(nf / max_exact) / math.log(MAX_DISTANCE / max_exact)
                         * (NUM_BUCKETS - max_exact)).astype(jnp.int32)
    large = jnp.minimum(large, NUM_BUCKETS - 1)
    return jnp.where(n < max_exact, n, large)


def _attn_bias_tiles(rel_bias):
    ta = ATT_TILE
    tab = rel_bias.astype(F32) - rel_bias[NUM_BUCKETS - 1].astype(F32)[None, :]
    rel = jnp.arange(ta)[None, :] - jnp.arange(ta)[:, None]

    def lookup(r):
        onehot = (_t5_bucket(r)[:, :, None] == jnp.arange(NUM_BUCKETS)[None, None, :]).astype(F32)
        return jnp.einsum('kqb,bh->hkq', onehot, tab, precision=lax.Precision.HIGHEST)

    diag = jnp.where((rel >= 0)[None], lookup(rel), NEG_BIG)
    return jnp.concatenate([lookup(rel + ta), diag], axis=1) * LOG2E


def _col_max(s):
    while s.shape[0] > 8:
        half = s.shape[0] // 2
        s = jnp.maximum(s[:half], s[half:])
    return jnp.max(s, axis=0, keepdims=True)


def _attn_kernel(lamv_ref, qt_ref, k_ref, vt_ref, bias_ref, g_ref, o_ref, m_sc, acc_sc, sa_sc, sb_sc, *, lam_init):
    ta = ATT_TILE
    hw = 2 * DA_HD
    heads = range(ATT_HEADS_PER_STEP)
    hrows = [slice(hh * hw, (hh + 1) * hw) for hh in heads]
    qi = pl.program_id(2)

    lv = lamv_ref[...]
    lam = (jnp.exp(jnp.sum(lv[0:1] * lv[1:2], axis=-1, keepdims=True))
           - jnp.exp(jnp.sum(lv[2:3] * lv[3:4], axis=-1, keepdims=True)) + lam_init)

    rowi = lax.broadcasted_iota(jnp.int32, (hw, ta), 0)
    qqt = []
    for hh in heads:
        qt = qt_ref[0, hrows[hh], :]
        zero = jnp.zeros_like(qt)
        qqt.append(jnp.concatenate([jnp.where(rowi < DA_HD, qt, zero), jnp.where(rowi >= DA_HD, qt, zero)], axis=1))

    m_sc[...] = jnp.full(m_sc.shape, NEG_BIG, F32)
    acc_sc[...] = jnp.zeros(acc_sc.shape, F32)

    def scores(k0, nk):
        k0 = pl.multiple_of(k0, ta)
        return [jnp.dot(k_ref[0, pl.ds(k0, nk), hrows[hh]], qqt[hh], preferred_element_type=F32) for hh in heads]

    def absorb(s, k0, nk):
        k0 = pl.multiple_of(k0, ta)
        m_prev = [m_sc[hh] for hh in heads]
        m_new = [jnp.maximum(m_prev[hh], _col_max(s[hh])) for hh in heads]
        alpha = [jnp.exp2(m_prev[hh] - m_new[hh]) for hh in heads]
        p = [jnp.exp2((s[hh] - m_new[hh]).astype(BF16)) for hh in heads]
        ones = jnp.ones((ATT_ONES_ROWS, nk), BF16)
        pv = [jnp.dot(jnp.concatenate([vt_ref[0, hrows[hh], pl.ds(k0, nk)], ones], axis=0), p[hh],
                      preferred_element_type=F32) for hh in heads]
        for hh in heads:
            acc_sc[hh] = alpha[hh] * acc_sc[hh] + pv[hh]
            m_sc[hh] = m_new[hh]

    def step(k0, nk, biased):
        s = scores(k0, nk)
        if biased:
            bias = [bias_ref[hh, 2 * ta - nk:2 * ta, :] for hh in heads]
            s = [s[hh] + jnp.concatenate([bias[hh], bias[hh]], axis=1) for hh in heads]
        absorb(s, k0, nk)

    nfar = jnp.maximum(qi - 1, 0)
    blk = 2 * ta
    nblk = nfar // 2

    def fill(buf, b):
        s = scores(b * blk, blk)
        for hh in heads:
            buf[hh] = s[hh]

    def drain(buf, b):
        absorb([buf[hh] for hh in heads], b * blk, blk)

    @pl.when(nblk >= 1)
    def _():
        fill(sa_sc, 0)

    def pair_body(i, carry):
        fill(sb_sc, 2 * i + 1)
        drain(sa_sc, 2 * i)
        fill(sa_sc, 2 * i + 2)
        drain(sb_sc, 2 * i + 1)
        return carry

    lax.fori_loop(0, nblk // 2, pair_body, 0)

    @pl.when(nblk % 2 == 1)
    def _():
        drain(sa_sc, nblk - 1)

    @pl.when(nfar % 2 == 1)
    def _():
        step((nfar - 1) * ta, ta, False)

    @pl.when(qi >= 1)
    def _():
        step((qi - 1) * ta, 2 * ta, True)

    @pl.when(qi == 0)
    def _():
        step(0, ta, True)

    for hh in heads:
        acc = acc_sc[hh, 0:hw, :] * (1.0 / acc_sc[hh, hw:hw + 1, :])
        ot = acc[:, 0:ta] - lam * acc[:, ta:2 * ta]
        ms = jnp.mean(ot * ot, axis=0, keepdims=True)
        ot = ot * (lax.rsqrt(ms + LN_EPS) * (1.0 - lam_init))
        o_ref[0, :, hrows[hh]] = (ot.T * g_ref[...]).astype(o_ref.dtype)


def _diff_attention(qt, k, vt, lamv, bias_tiles, subln_g, lam_init):
    b, s, _ = k.shape
    ta = ATT_TILE
    nh = ATT_HEADS_PER_STEP
    hw = 2 * DA_HD
    return pl.pallas_call(
        functools.partial(_attn_kernel, lam_init=lam_init),
        grid=(b, DA_HEADS // nh, s // ta),
        in_specs=[
            pl.BlockSpec((4, DA_HD), lambda bi, hi, qi: (0, 0)),
            pl.BlockSpec((1, nh * hw, ta), lambda bi, hi, qi: (bi, hi, qi)),
            pl.BlockSpec((1, s, nh * hw), lambda bi, hi, qi: (bi, 0, hi)),
            pl.BlockSpec((1, nh * hw, s), lambda bi, hi, qi: (bi, hi, 0)),
            pl.BlockSpec((nh, 2 * ta, ta), lambda bi, hi, qi: (hi, 0, 0)),
            pl.BlockSpec((1, hw), lambda bi, hi, qi: (0, 0)),
        ],
        out_specs=pl.BlockSpec((1, ta, nh * hw), lambda bi, hi, qi: (bi, qi, hi)),
        out_shape=jax.ShapeDtypeStruct((b, s, DA_HEADS * hw), BF16),
        scratch_shapes=[pltpu.VMEM((nh, 1, 2 * ta), F32), pltpu.VMEM((nh, hw + ATT_ONES_ROWS, 2 * ta), F32),
                        pltpu.VMEM((nh, 2 * ta, 2 * ta), F32), pltpu.VMEM((nh, 2 * ta, 2 * ta), F32)],
        compiler_params=_cparams(("parallel", "parallel", "arbitrary")),
        name="diff_attention",
    )(lamv, qt, k, vt, bias_tiles, subln_g)


def _rw_prep_kernel(r_ref, k_ref, v_ref, lo_ref, rp_ref, kp_ref, vp_ref, lop_ref,
                    mu_ref, mulo_ref, w0_ref, w2h_ref, w2l_ref, a0_ref, a2h_ref, a2l_ref, g2_ref,
                    kk_ref, ka_ref,
                    r_o, lw_o, km_o, v_o, kk_o, a_o, g_o, *, tiles_per_seq):
    i = pl.program_id(0)
    keep = (i % tiles_per_seq != 0).astype(F32)

    def shifted(cur_ref, prev_ref, mu):
        p = cur_ref[...]
        row = lax.broadcasted_iota(jnp.int32, p.shape, 0)
        prev_last = prev_ref[7:8, :] * keep
        pm1 = jnp.where(row == 0, prev_last, pltpu.roll(p, 1, 0))
        return p + mu * (pm1 - p)

    r = shifted(r_ref, rp_ref, mu_ref[0:1])
    k = shifted(k_ref, kp_ref, mu_ref[1:2])
    v = shifted(v_ref, vp_ref, mu_ref[2:3])
    lo = shifted(lo_ref, lop_ref, mulo_ref[...])

    def dot3w(a, wh_ref, wl_ref):
        ah, al = _split2(a)
        wh = wh_ref[...]
        d = functools.partial(jnp.dot, preferred_element_type=F32)
        return d(ah, wh) + (d(ah, wl_ref[...]) + d(al, wh))

    zw = w0_ref[...] + dot3w(jnp.tanh(lo[:, 0:128]), w2h_ref, w2l_ref)
    nz = -zw
    softplus = jnp.maximum(nz, 0.0) + jnp.log(1.0 + jnp.exp(-jnp.abs(nz)))
    w_log = -softplus - 0.5
    lw_o[...] = -jnp.exp(w_log)
    a_ic = jax.nn.sigmoid(a0_ref[...] + dot3w(lo[:, 128:256], a2h_ref, a2l_ref))
    g_o[...] = jnp.dot(jax.nn.sigmoid(lo[:, 256:512]).astype(BF16), g2_ref[...], preferred_element_type=F32)
    r_o[...] = r
    v_o[...] = v
    kk_o[...] = k * kk_ref[...]
    km_o[...] = k * (1.0 + (a_ic - 1.0) * ka_ref[...])
    a_o[...] = a_ic


def _rw_prep(rest, seq, mu_rkv, mu_lo, w0, w2h, w2l, a0, a2h, a2l, g2, k_k, k_a, tm=256):
    t = rest.shape[0]
    d = D_MODEL
    cur = lambda c, w: pl.BlockSpec((tm, w), lambda i: (i, c))
    prev = lambda c, w: pl.BlockSpec((8, w), lambda i: (jnp.maximum(i * (tm // 8) - 1, 0), c))
    full = lambda a: pl.BlockSpec(a.shape, lambda i: (0,) * a.ndim)
    params = (mu_rkv, mu_lo, w0, w2h, w2l, a0, a2h, a2l, g2, k_k, k_a)
    out = jax.ShapeDtypeStruct((t, d), F32)
    return pl.pallas_call(
        functools.partial(_rw_prep_kernel, tiles_per_seq=seq // tm),
        grid=(t // tm,),
        in_specs=[cur(0, d), cur(1, d), cur(2, d), cur(10, 512),
                  prev(0, d), prev(1, d), prev(2, d), prev(10, 512)] + [full(a) for a in params],
        out_specs=[pl.BlockSpec((tm, d), lambda i: (i, 0))] * 7,
        out_shape=[out] * 7,
        compiler_params=_cparams(("parallel",)),
        name="rwkv_prep",
    )(rest, rest, rest, rest, rest, rest, rest, rest, *params)


def _dot(a, b, dims=(((1,), (0,)), ((), ()))):
    return lax.dot_general(a.astype(BF16), b.astype(BF16), dims, preferred_element_type=F32)


def _dot_split_lhs(a, b_bf16):
    ah, al = _split2(a)
    return (jnp.dot(ah, b_bf16, preferred_element_type=F32) + jnp.dot(al, b_bf16, preferred_element_type=F32))


def _pair_consts():
    c, n = RW_CHUNK, RW_HD
    lane = lax.broadcasted_iota(jnp.int32, (c, 2 * n), 1)
    row = lax.broadcasted_iota(jnp.int32, (2 * c, 2 * c), 0)
    col = lax.broadcasted_iota(jnp.int32, (2 * c, 2 * c), 1)
    same_head = (row // n) == (col // n)
    return lane < n, same_head.astype(BF16)


def _expand(x2, head_a):
    zero = jnp.zeros_like(x2)
    return jnp.concatenate([jnp.where(head_a, x2, zero), jnp.where(head_a, zero, x2)], axis=0)


def _compact(xm):
    return xm[0:RW_CHUNK] + xm[RW_CHUNK:2 * RW_CHUNK]


def _rw_intra_kernel(r_ref, lw_ref, km_ref, v_ref, kk_ref, a_ref, rk_ref,
                     pt_ref, qt_ref, rh_ref, y0_ref, bon_ref):
    c, n = RW_CHUNK, RW_HD
    head_a, seg_ones = _pair_consts()
    row = lax.broadcasted_iota(jnp.int32, (2 * c, 2 * c), 0)
    col = lax.broadcasted_iota(jnp.int32, (2 * c, 2 * c), 1)
    strict = (row % c) > (col % c)
    incl = (row % c) >= (col % c)
    trow = lax.broadcasted_iota(jnp.int32, (c, c), 0)
    tcol = lax.broadcasted_iota(jnp.int32, (c, c), 1)
    tri_ones = (trow >= tcol).astype(BF16)
    diag2 = lax.broadcasted_iota(jnp.int32, (c, 2 * n), 0) == (lax.broadcasted_iota(jnp.int32, (c, 2 * n), 1) % n)

    chunks = range(r_ref.shape[1] // c)
    sls = [slice(ci * c, (ci + 1) * c) for ci in chunks]
    d = functools.partial(jnp.dot, preferred_element_type=F32)

    def cumsum(lw2):
        l1 = lw2.astype(BF16)
        rem = lw2 - l1.astype(F32)
        l2 = rem.astype(BF16)
        l3 = (rem - l2.astype(F32)).astype(BF16)
        return d(tri_ones, l1) + (d(tri_ones, l2) + d(tri_ones, l3))

    cum = [cumsum(lw_ref[0, sl, :]) for sl in sls]
    nrm = [jnp.sqrt(_dot_split_lhs(kk_ref[0, sl, :] * kk_ref[0, sl, :], seg_ones)) for sl in sls]
    big, atm, vm, bdm, kdm = [], [], [], [], []
    for ci, sl in enumerate(sls):
        lw2, cm = lw_ref[0, sl, :], cum[ci]
        cum_last = cm[c - 1:c, :]
        g_inv = jnp.exp(-cm)
        g_rest = jnp.exp(cum_last - cm)
        kkn = kk_ref[0, sl, :] / jnp.maximum(nrm[ci], 1e-12)
        b2 = kkn * a_ref[0, sl, :]
        km2 = km_ref[0, sl, :]
        a_m = _expand(-kkn * jnp.exp(cm - lw2), head_a)
        r_m = _expand(r_ref[0, sl, :] * jnp.exp(cm), head_a)
        b_m = _expand(b2 * g_inv, head_a)
        k_m = _expand(km2 * g_inv, head_a)
        atm.append(a_m)
        bdm.append(_expand(b2 * g_rest, head_a))
        kdm.append(_expand(km2 * g_rest, head_a))
        vm.append(_expand(v_ref[0, sl, :], head_a))
        big.append(_dot(jnp.concatenate([a_m, r_m], axis=0), jnp.concatenate([b_m, k_m], axis=0), NT_DIMS))
    lmat = [jnp.where(strict, bg[0:128, 0:128], 0.0) for bg in big]
    m_rb = [jnp.where(incl, bg[128:256, 0:128], 0.0) for bg in big]
    mv = [_dot(jnp.concatenate([jnp.where(strict, bg[0:128, 128:256], 0.0),
                                jnp.where(incl, bg[128:256, 128:256], 0.0)], axis=0), vm[ci])
          for ci, bg in enumerate(big)]
    x = [jnp.concatenate([atm[ci], mv[ci][0:128]], axis=1) for ci in chunks]
    lp = lmat
    for it in range(6):
        x = [x[ci] + _dot(lp[ci], x[ci]) for ci in chunks]
        if it < 5:
            lp = [_dot(m, m) for m in lp]
    rb = [_dot(m_rb[ci], x[ci]) for ci in chunks]
    bx = [_dot(bdm[ci], x[ci], TN_DIMS) for ci in chunks]
    kv = [_dot(kdm[ci], vm[ci], TN_DIMS) for ci in chunks]
    for ci, sl in enumerate(sls):
        g_last = jnp.exp(cum[ci][c - 1:c, :])
        pt_ref[0, sl, :] = _compact(bx[ci][:, 0:128]) + jnp.where(diag2, g_last, 0.0)
        qt_ref[0, sl, :] = _compact(bx[ci][:, 128:256] + kv[ci])
        rh_ref[0, sl, :] = (r_ref[0, sl, :] * jnp.exp(cum[ci]) + _compact(rb[ci][:, 0:128])).astype(rh_ref.dtype)
        y0_ref[0, sl, :] = _compact(rb[ci][:, 128:256] + mv[ci][128:256])
        bon_ref[0, sl, :] = (_dot_split_lhs(r_ref[0, sl, :] * km_ref[0, sl, :] * rk_ref[...], seg_ones)
                             * v_ref[0, sl, :])


def _rw_intra(r, lw, km, v, kk, a_ic, r_k, tb=512):
    b, s, d = r.shape
    blk = pl.BlockSpec((1, tb, 128), lambda bi, hp, ti: (bi, ti, hp))
    par = pl.BlockSpec((1, 128), lambda bi, hp, ti: (0, hp))
    out = jax.ShapeDtypeStruct((b, s, d), F32)
    return pl.pallas_call(
        _rw_intra_kernel,
        grid=(b, d // 128, s // tb),
        in_specs=[blk] * 6 + [par],
        out_specs=[blk] * 5,
        out_shape=[out, out, jax.ShapeDtypeStruct((b, s, d), BF16), out, out],
        compiler_params=_cparams(("parallel", "parallel", "parallel")),
        name="rwkv_intra",
    )(r, lw, km, v, kk, a_ic, r_k)


def _rw_scan_kernel(pt_ref, qt_ref, rh_ref, y0_ref, y_ref, z_sc):
    c, n = RW_CHUNK, RW_HD
    pairs = range(z_sc.shape[0])
    head_a, _ = _pair_consts()
    lns = [slice(hp * 2 * n, (hp + 1) * 2 * n) for hp in pairs]

    @pl.when(pl.program_id(1) == 0)
    def _():
        z_sc[...] = jnp.zeros(z_sc.shape, F32)

    z = [z_sc[hp] for hp in pairs]
    for ci in range(pt_ref.shape[1] // c):
        sl = slice(ci * c, (ci + 1) * c)
        y = [_dot(rh_ref[0, sl, lns[hp]], z[hp]) for hp in pairs]
        z = [_dot3(_expand(pt_ref[0, sl, lns[hp]], head_a), z[hp]) + _expand(qt_ref[0, sl, lns[hp]], head_a)
             for hp in pairs]
        for hp in pairs:
            y_ref[0, sl, lns[hp]] = y[hp] + y0_ref[0, sl, lns[hp]]
    for hp in pairs:
        z_sc[hp] = z[hp]


def _rw_scan(pt, qt, rh, y0, tb=512):
    b, s, d = pt.shape
    blk = pl.BlockSpec((1, tb, d), lambda bi, ti: (bi, ti, 0))
    return pl.pallas_call(
        _rw_scan_kernel,
        grid=(b, s // tb),
        in_specs=[blk] * 4,
        out_specs=blk,
        out_shape=jax.ShapeDtypeStruct((b, s, d), F32),
        scratch_shapes=[pltpu.VMEM((d // 128, 128, 128), F32)],
        compiler_params=_cparams(("parallel", "arbitrary")),
        name="rwkv_scan",
    )(pt, qt, rh, y0)


def _mix_kernel(oda_ref, yrw_ref, bon_ref, gout_ref, ga_ref, gb_ref, x_ref, wda_ref, wrw_ref, wmix_ref,
                gng_ref, gnb_ref, lng_ref, lnb_ref, o_ref, *, alpha):
    y_da = jnp.dot(oda_ref[...], wda_ref[...], preferred_element_type=F32)
    _, seg_ones = _pair_consts()
    parts = []
    for hp in range(RW_HEADS // 2):
        ln = slice(hp * 2 * RW_HD, (hp + 1) * 2 * RW_HD)
        y = yrw_ref[:, ln]
        mu = _dot_split_lhs(y, seg_ones) * (1.0 / RW_HD)
        yc = y - mu
        var = _dot_split_lhs(yc * yc, seg_ones) * (1.0 / RW_HD)
        parts.append(yc * lax.rsqrt(var + RW_LNX_EPS))
    yn = jnp.concatenate(parts, axis=1) * gng_ref[...] + gnb_ref[...] + bon_ref[...]
    y_rw = jnp.dot((yn * gout_ref[...]).astype(BF16), wrw_ref[...], preferred_element_type=F32)
    mixed = jax.nn.sigmoid(ga_ref[...]) * y_da + jax.nn.sigmoid(gb_ref[...]) * y_rw
    z = alpha * x_ref[...] + jnp.dot(mixed.astype(BF16), wmix_ref[...], preferred_element_type=F32)
    o_ref[...] = _layer_norm_rows(z, lng_ref[...], lnb_ref[...])


def _mix(o_da, y_rw, bonus, g_out, rest, x, w_da, w_rw, w_mix, gn_g, gn_b, ln_g, ln_b, alpha, tm=512):
    t, d = x.shape
    rowb = lambda c: pl.BlockSpec((tm, d), lambda i: (i, c))
    full = lambda a: pl.BlockSpec(a.shape, lambda i: (0,) * a.ndim)
    return pl.pallas_call(
        functools.partial(_mix_kernel, alpha=alpha),
        grid=(t // tm,),
        in_specs=[rowb(0), rowb(0), rowb(0), rowb(0), rowb(3), rowb(4), rowb(0),
                  full(w_da), full(w_rw), full(w_mix), full(gn_g), full(gn_b), full(ln_g), full(ln_b)],
        out_specs=rowb(0),
        out_shape=jax.ShapeDtypeStruct((t, d), F32),
        compiler_params=_cparams(("parallel",)),
        name="mix_merge_ln1",
    )(o_da, y_rw, bonus, g_out, rest, rest, x, w_da, w_rw, w_mix, gn_g, gn_b, ln_g, ln_b)


def _mem_kv_kernel(mem_ref, g_ref, b_ref, w_ref, ck_ref, cv_ref):
    m = _layer_norm_rows(mem_ref[0], g_ref[...], b_ref[...])
    kv = jnp.dot(m.astype(BF16), w_ref[...], preferred_element_type=F32)
    ck_ref[0] = kv[:, :D_MODEL].astype(BF16)
    cv_ref[0] = kv[:, D_MODEL:].astype(BF16)


def _mem_kv(mem, g, b, w_ckv):
    bsz, m, d = mem.shape
    full = lambda a: pl.BlockSpec(a.shape, lambda i: (0,) * a.ndim)
    blk = pl.BlockSpec((1, m, d), lambda i: (i, 0, 0))
    return pl.pallas_call(
        _mem_kv_kernel,
        grid=(bsz,),
        in_specs=[blk, full(g), full(b), full(w_ckv)],
        out_specs=[blk, blk],
        out_shape=[jax.ShapeDtypeStruct((bsz, m, d), BF16)] * 2,
        compiler_params=_cparams(("parallel",)),
        name="mem_kv",
    )(mem, g, b, w_ckv)


def _cross_kernel(x_ref, ck_ref, cv_ref, wq_ref, wo_ref, lng_ref, lnb_ref, o_ref, *, alpha):
    x = x_ref[...]
    cq = jnp.dot(x.astype(BF16), wq_ref[...], preferred_element_type=F32)
    cq = (cq * (CA_HD ** -0.5)).astype(BF16)
    outs = []
    for h in range(CA_HEADS):
        sl = slice(h * CA_HD, (h + 1) * CA_HD)
        s = lax.dot_general(cq[:, sl], ck_ref[0, :, sl], NT_DIMS, preferred_element_type=F32)
        s = s - jnp.max(s, axis=-1, keepdims=True)
        p = jnp.exp(s)
        p = p / jnp.sum(p, axis=-1, keepdims=True)
        outs.append(jnp.dot(p.astype(BF16), cv_ref[0, :, sl], preferred_element_type=F32))
    co = jnp.dot(jnp.concatenate(outs, axis=1).astype(BF16), wo_ref[...], preferred_element_type=F32)
    o_ref[...] = _layer_norm_rows(alpha * x + co, lng_ref[...], lnb_ref[...])


def _cross_attention(x, ck, cv, w_cq, w_co, ln_g, ln_b, seq, alpha, tm=512):
    t, d = x.shape
    tiles_per_seq = seq // tm
    full = lambda a: pl.BlockSpec(a.shape, lambda i: (0,) * a.ndim)
    rowb = pl.BlockSpec((tm, d), lambda i: (i, 0))
    memb = pl.BlockSpec((1,) + ck.shape[1:], lambda i: (i // tiles_per_seq, 0, 0))
    return pl.pallas_call(
        functools.partial(_cross_kernel, alpha=alpha),
        grid=(t // tm,),
        in_specs=[rowb, memb, memb, full(w_cq), full(w_co), full(ln_g), full(ln_b)],
        out_specs=rowb,
        out_shape=jax.ShapeDtypeStruct((t, d), F32),
        compiler_params=_cparams(("parallel",)),
        name="cross_attention_ln2",
    )(x, ck, cv, w_cq, w_co, ln_g, ln_b)


def _ffn_kernel(x_ref, wg_ref, wu_ref, wo_ref, lng_ref, lnb_ref, o_ref, xb_ref, acc_ref, *, alpha):
    j = pl.program_id(1)

    @pl.when(j == 0)
    def _():
        xb_ref[...] = x_ref[...].astype(BF16)
        acc_ref[...] = jnp.zeros(acc_ref.shape, F32)

    xb = xb_ref[...]
    gate = jnp.dot(xb, wg_ref[...], preferred_element_type=F32)
    up = jnp.dot(xb, wu_ref[...], preferred_element_type=F32)
    h = (gate * jax.nn.sigmoid(gate) * up).astype(BF16)
    acc_ref[...] += jnp.dot(h, wo_ref[...], preferred_element_type=F32)

    @pl.when(j == pl.num_programs(1) - 1)
    def _():
        o_ref[...] = _layer_norm_rows(alpha * x_ref[...] + acc_ref[...], lng_ref[...], lnb_ref[...])


def _ffn(x, w_in, w_out, ln_g, ln_b, alpha, tm=512, tf=1408):
    t, d = x.shape
    nf = D_FF // tf
    full = lambda a: pl.BlockSpec(a.shape, lambda i, j: (0,) * a.ndim)
    rowb = pl.BlockSpec((tm, d), lambda i, j: (i, 0))
    return pl.pallas_call(
        functools.partial(_ffn_kernel, alpha=alpha),
        grid=(t // tm, nf),
        in_specs=[rowb,
                  pl.BlockSpec((d, tf), lambda i, j: (0, j)),
                  pl.BlockSpec((d, tf), lambda i, j: (0, nf + j)),
                  pl.BlockSpec((tf, d), lambda i, j: (j, 0)),
                  full(ln_g), full(ln_b)],
        out_specs=rowb,
        out_shape=jax.ShapeDtypeStruct((t, d), F32),
        scratch_shapes=[pltpu.VMEM((tm, d), BF16), pltpu.VMEM((tm, d), F32)],
        compiler_params=_cparams(("parallel", "arbitrary")),
        name="swiglu_ffn_ln3",
    )(x, w_in, w_in, w_out, ln_g, ln_b)


def _pad_cols(a, width):
    return jnp.pad(a, ((0, 0), (0, width - a.shape[1])))


def _pad_rows(a, height):
    return jnp.pad(a, ((0, height - a.shape[0]), (0, 0)))


def kernel(x, mem, rel_bias, w_in, shift_mu, lambda_q1, lambda_k1, lambda_q2, lambda_k2, da_subln_g, w_da_proj, rw_w0, rw_w2, rw_a0, rw_a2, rw_g2, rw_k_k, rw_k_a, rw_r_k, rw_lnx_g, rw_lnx_b, w_rw_proj, w_mix_out, ln1_g, ln1_b, mem_ln_g, mem_ln_b, w_cq, w_ckv, w_co, ln2_g, ln2_b, w_ffn_in, w_ffn_out, ln3_g, ln3_b):
    bsz, seq, d = x.shape
    depth = w_in.shape[0]
    alpha = (2.0 * depth) ** 0.25
    t = bsz * seq
    row = lambda a: a.reshape(1, -1).astype(F32)
    bias_tiles = _attn_bias_tiles(rel_bias)
    xf = x.reshape(t, d)

    qkv_w = 3 * D_MODEL
    rw0 = qkv_w
    lo0 = rw0 + 3 * D_MODEL
    gate0 = lo0 + DECAY_LORA + AAA_LORA + GATE_LORA

    for l in range(depth):
        w = w_in[l]
        w_qt = w[:, :D_MODEL].T.astype(BF16)
        w_k = w[:, D_MODEL:2 * D_MODEL].astype(BF16)
        w_vt = w[:, 2 * D_MODEL:qkv_w].T.astype(BF16)
        w_rest = jnp.concatenate([
            w[:, rw0:lo0], w[:, gate0:gate0 + 2 * D_MODEL],
            _pad_cols(w[:, lo0:lo0 + DECAY_LORA], 128),
            _pad_cols(w[:, lo0 + DECAY_LORA:lo0 + DECAY_LORA + AAA_LORA], 128),
            _pad_cols(w[:, lo0 + DECAY_LORA + AAA_LORA:gate0], 256)], axis=1).astype(BF16)
        mu = shift_mu[l].astype(F32)
        mu_rkv = mu[:3 * D_MODEL].reshape(3, D_MODEL)
        o1 = 3 * D_MODEL
        mu_lo = jnp.concatenate([
            _pad_cols(mu[None, o1:o1 + DECAY_LORA], 128),
            _pad_cols(mu[None, o1 + DECAY_LORA:o1 + DECAY_LORA + AAA_LORA], 128),
            _pad_cols(mu[None, o1 + DECAY_LORA + AAA_LORA:], 256)], axis=1)
        w2 = _pad_rows(rw_w2[l].astype(F32), 128)
        a2 = _pad_rows(rw_a2[l].astype(F32), 128)
        w2h, w2l = _split2(w2)
        a2h, a2l = _split2(a2)
        g2 = _pad_rows(rw_g2[l], 256).astype(BF16)

        qt, kda, vt = _qkv_proj(xf, w_qt, w_k, w_vt, bsz, seq)
        rest = _matmul(xf, w_rest, F32, 1024, 1408, "in_proj_rest")

        lam_init = 0.8 - 0.6 * math.exp(-0.3 * l)
        lamv = jnp.stack([lambda_q1[l], lambda_k1[l], lambda_q2[l], lambda_k2[l]]).astype(F32)
        o_da = _diff_attention(qt, kda.reshape(bsz, seq, d), vt, lamv, bias_tiles, row(da_subln_g[l]), lam_init)

        r, lw, km, v, kk, a_ic, g_out = _rw_prep(
            rest, seq, mu_rkv, mu_lo, row(rw_w0[l]), w2h, w2l, row(rw_a0[l]), a2h, a2l, g2,
            row(rw_k_k[l]), row(rw_k_a[l]))
        sh = lambda a: a.reshape(bsz, seq, d)
        pt, qt_rw, rh, y0, bon = _rw_intra(sh(r), sh(lw), sh(km), sh(v), sh(kk), sh(a_ic), row(rw_r_k[l]))
        y_rw = _rw_scan(pt, qt_rw, rh, y0)

        x1 = _mix(o_da.reshape(t, d), y_rw.reshape(t, d), bon.reshape(t, d), g_out, rest, xf,
                  w_da_proj[l].astype(BF16), w_rw_proj[l].astype(BF16), w_mix_out[l].astype(BF16),
                  row(rw_lnx_g[l]), row(rw_lnx_b[l]), row(ln1_g[l]), row(ln1_b[l]), alpha)

        ck, cv = _mem_kv(mem, row(mem_ln_g[l]), row(mem_ln_b[l]), w_ckv[l].astype(BF16))
        x2 = _cross_attention(x1, ck, cv, w_cq[l].astype(BF16), w_co[l].astype(BF16),
                              row(ln2_g[l]), row(ln2_b[l]), seq, alpha)

        xf = _ffn(x2, w_ffn_in[l].astype(BF16), w_ffn_out[l].astype(BF16),
                  row(ln3_g[l]), row(ln3_b[l]), alpha)
    return xf.reshape(bsz, seq, d)
```

```python
import functools
import math

import jax
import jax.numpy as jnp
from jax import lax
from jax.experimental import pallas as pl
from jax.experimental.pallas import tpu as pltpu

F32 = jnp.float32
BF16 = jnp.bfloat16

D_MODEL = 1024
DA_HEADS = 8
DA_HD = 64
NUM_BUCKETS = 32
MAX_DISTANCE = 128
RW_HD = 64
RW_HEADS = D_MODEL // RW_HD
DECAY_LORA = 64
AAA_LORA = 64
GATE_LORA = 160
RW_LNX_EPS = 64e-5
MEM_TOKENS = 256
CA_HEADS = 4
CA_HD = D_MODEL // CA_HEADS
D_FF = 2816
LN_EPS = 1e-5
NEG_BIG = -1e30

LOG2E = math.log2(math.e)
ATT_TILE = 256
ATT_HEADS_PER_STEP = 2
ATT_ONES_ROWS = 16
RW_CHUNK = 64
VMEM_LIMIT = 56 * 1024 * 1024

NT_DIMS = (((1,), (1,)), ((), ()))
TN_DIMS = (((0,), (0,)), ((), ()))


def _cparams(sem, vmem=VMEM_LIMIT):
    return pltpu.CompilerParams(dimension_semantics=sem, vmem_limit_bytes=vmem)


def _split2(a):
    hi = a.astype(BF16)
    lo = (a - hi.astype(F32)).astype(BF16)
    return hi, lo


def _dot3(a, b, dims=(((1,), (0,)), ((), ()))):
    ah, al = _split2(a)
    bh, bl = _split2(b)
    d = functools.partial(lax.dot_general, dimension_numbers=dims, preferred_element_type=F32)
    return d(ah, bh) + (d(ah, bl) + d(al, bh))


def _layer_norm_rows(z, g, b):
    mu = jnp.mean(z, axis=-1, keepdims=True)
    zc = z - mu
    var = jnp.mean(zc * zc, axis=-1, keepdims=True)
    return zc * lax.rsqrt(var + LN_EPS) * g + b


def _mm_kernel(x_ref, w_ref, o_ref, xb_ref):
    @pl.when(pl.program_id(1) == 0)
    def _():
        xb_ref[...] = x_ref[...].astype(BF16)

    o_ref[...] = jnp.dot(xb_ref[...], w_ref[...], preferred_element_type=F32).astype(o_ref.dtype)


def _matmul(x, w, out_dtype, tm, tn, name):
    t, k = x.shape
    n = w.shape[1]
    return pl.pallas_call(
        _mm_kernel,
        grid=(t // tm, n // tn),
        in_specs=[pl.BlockSpec((tm, k), lambda i, j: (i, 0)),
                  pl.BlockSpec((k, tn), lambda i, j: (0, j))],
        out_specs=pl.BlockSpec((tm, tn), lambda i, j: (i, j)),
        out_shape=jax.ShapeDtypeStruct((t, n), out_dtype),
        scratch_shapes=[pltpu.VMEM((tm, k), BF16)],
        compiler_params=_cparams(("parallel", "arbitrary")),
        name=name,
    )(x, w)


def _qkv_kernel(x_ref, wqt_ref, wk_ref, wvt_ref, qt_ref, k_ref, vt_ref):
    xb = x_ref[...].astype(BF16)
    k_ref[...] = jnp.dot(xb, wk_ref[...], preferred_element_type=F32).astype(BF16)
    qt = lax.dot_general(wqt_ref[...], xb, NT_DIMS, preferred_element_type=F32)
    qt_ref[0] = (qt * (DA_HD ** -0.5 * LOG2E)).astype(BF16)
    vt_ref[0] = lax.dot_general(wvt_ref[...], xb, NT_DIMS, preferred_element_type=F32).astype(BF16)


def _qkv_proj(x, wqt, wk, wvt, bsz, seq, tm=512):
    t, d = x.shape
    tiles_per_seq = seq // tm
    full = lambda a: pl.BlockSpec(a.shape, lambda i: (0,) * a.ndim)
    tblk = pl.BlockSpec((1, d, tm), lambda i: (i // tiles_per_seq, 0, i % tiles_per_seq))
    rowb = pl.BlockSpec((tm, d), lambda i: (i, 0))
    tshape = jax.ShapeDtypeStruct((bsz, d, seq), BF16)
    return pl.pallas_call(
        _qkv_kernel,
        grid=(t // tm,),
        in_specs=[rowb, full(wqt), full(wk), full(wvt)],
        out_specs=[tblk, rowb, tblk],
        out_shape=[tshape, jax.ShapeDtypeStruct((t, d), BF16), tshape],
        compiler_params=_cparams(("parallel",)),
        name="in_proj_qkv",
    )(x, wqt, wk, wvt)


def _t5_bucket(rel):
    n = jnp.maximum(rel, 0)
    max_exact = NUM_BUCKETS // 2
    nf = jnp.maximum(n, 1).astype(jnp.float32)
    large = max_exact + (jnp.log(nf / max_exact) / math.log(MAX_DISTANCE / max_exact)
                         * (NUM_BUCKETS - max_exact)).astype(jnp.int32)
    large = jnp.minimum(large, NUM_BUCKETS - 1)
    return jnp.where(n < max_exact, n, large)


def _attn_bias_tiles(rel_bias):
    ta = ATT_TILE
    tab = rel_bias.astype(F32) - rel_bias[NUM_BUCKETS - 1].astype(F32)[None, :]
    rel = jnp.arange(ta)[None, :] - jnp.arange(ta)[:, None]

    def lookup(r):
        onehot = (_t5_bucket(r)[:, :, None] == jnp.arange(NUM_BUCKETS)[None, None, :]).astype(F32)
        return jnp.einsum('kqb,bh->hkq', onehot, tab, precision=lax.Precision.HIGHEST)

    diag = jnp.where((rel >= 0)[None], lookup(rel), NEG_BIG)
    return jnp.concatenate([lookup(rel + ta), diag], axis=1) * LOG2E


def _col_max(s):
    while s.shape[0] > 8:
        half = s.shape[0] // 2
        s = jnp.maximum(s[:half], s[half:])
    return jnp.max(s, axis=0, keepdims=True)


def _attn_kernel(lamv_ref, qt_ref, k_ref, vt_ref, bias_ref, g_ref, o_ref, m_sc, acc_sc, sa_sc, sb_sc, *, lam_init):
    ta = ATT_TILE
    hw = 2 * DA_HD
    heads = range(ATT_HEADS_PER_STEP)
    hrows = [slice(hh * hw, (hh + 1) * hw) for hh in heads]
    qi = pl.program_id(2)

    lv = lamv_ref[...]
    lam = (jnp.exp(jnp.sum(lv[0:1] * lv[1:2], axis=-1, keepdims=True))
           - jnp.exp(jnp.sum(lv[2:3] * lv[3:4], axis=-1, keepdims=True)) + lam_init)

    rowi = lax.broadcasted_iota(jnp.int32, (hw, ta), 0)
    qqt = []
    for hh in heads:
        qt = qt_ref[0, hrows[hh], :]
        zero = jnp.zeros_like(qt)
        qqt.append(jnp.concatenate([jnp.where(rowi < DA_HD, qt, zero), jnp.where(rowi >= DA_HD, qt, zero)], axis=1))

    m_sc[...] = jnp.full(m_sc.shape, NEG_BIG, F32)
    acc_sc[...] = jnp.zeros(acc_sc.shape, F32)

    def scores(k0, nk):
        k0 = pl.multiple_of(k0, ta)
        return [jnp.dot(k_ref[0, pl.ds(k0, nk), hrows[hh]], qqt[hh], preferred_element_type=F32) for hh in heads]

    def absorb(s, k0, nk):
        k0 = pl.multiple_of(k0, ta)
        m_prev = [m_sc[hh] for hh in heads]
        m_new = [jnp.maximum(m_prev[hh], _col_max(s[hh])) for hh in heads]
        alpha = [jnp.exp2(m_prev[hh] - m_new[hh]) for hh in heads]
        p = [jnp.exp2((s[hh] - m_new[hh]).astype(BF16)) for hh in heads]
        ones = jnp.ones((ATT_ONES_ROWS, nk), BF16)
        pv = [jnp.dot(jnp.concatenate([vt_ref[0, hrows[hh], pl.ds(k0, nk)], ones], axis=0), p[hh],
                      preferred_element_type=F32) for hh in heads]
        for hh in heads:
            acc_sc[hh] = alpha[hh] * acc_sc[hh] + pv[hh]
            m_sc[hh] = m_new[hh]

    def step(k0, nk, biased):
        s = scores(k0, nk)
        if biased:
            bias = [bias_ref[hh, 2 * ta - nk:2 * ta, :] for hh in heads]
            s = [s[hh] + jnp.concatenate([bias[hh], bias[hh]], axis=1) for hh in heads]
        absorb(s, k0, nk)

    nfar = jnp.maximum(qi - 1, 0)
    blk = 2 * ta
    nblk = nfar // 2

    def fill(buf, b):
        s = scores(b * blk, blk)
        for hh in heads:
            buf[hh] = s[hh]

    def drain(buf, b):
        absorb([buf[hh] for hh in heads], b * blk, blk)

    @pl.when(nblk >= 1)
    def _():
        fill(sa_sc, 0)

    def pair_body(i, carry):
        fill(sb_sc, 2 * i + 1)
        drain(sa_sc, 2 * i)
        fill(sa_sc, 2 * i + 2)
        drain(sb_sc, 2 * i + 1)
        return carry

    lax.fori_loop(0, nblk // 2, pair_body, 0)

    @pl.when(nblk % 2 == 1)
    def _():
        drain(sa_sc, nblk - 1)

    @pl.when(nfar % 2 == 1)
    def _():
        step((nfar - 1) * ta, ta, False)

    @pl.when(qi >= 1)
    def _():
        step((qi - 1) * ta, 2 * ta, True)

    @pl.when(qi == 0)
    def _():
        step(0, ta, True)

    for hh in heads:
        acc = acc_sc[hh, 0:hw, :] * (1.0 / acc_sc[hh, hw:hw + 1, :])
        ot = acc[:, 0:ta] - lam * acc[:, ta:2 * ta]
        ms = jnp.mean(ot * ot, axis=0, keepdims=True)
        ot = ot * (lax.rsqrt(ms + LN_EPS) * (1.0 - lam_init))
        o_ref[0, :, hrows[hh]] = (ot.T * g_ref[...]).astype(o_ref.dtype)


def _diff_attention(qt, k, vt, lamv, bias_tiles, subln_g, lam_init):
    b, s, _ = k.shape
    ta = ATT_TILE
    nh = ATT_HEADS_PER_STEP
    hw = 2 * DA_HD
    return pl.pallas_call(
        functools.partial(_attn_kernel, lam_init=lam_init),
        grid=(b, DA_HEADS // nh, s // ta),
        in_specs=[
            pl.BlockSpec((4, DA_HD), lambda bi, hi, qi: (0, 0)),
            pl.BlockSpec((1, nh * hw, ta), lambda bi, hi, qi: (bi, hi, qi)),
            pl.BlockSpec((1, s, nh * hw), lambda bi, hi, qi: (bi, 0, hi)),
            pl.BlockSpec((1, nh * hw, s), lambda bi, hi, qi: (bi, hi, 0)),
            pl.BlockSpec((nh, 2 * ta, ta), lambda bi, hi, qi: (hi, 0, 0)),
            pl.BlockSpec((1, hw), lambda bi, hi, qi: (0, 0)),
        ],
        out_specs=pl.BlockSpec((1, ta, nh * hw), lambda bi, hi, qi: (bi, qi, hi)),
        out_shape=jax.ShapeDtypeStruct((b, s, DA_HEADS * hw), BF16),
        scratch_shapes=[pltpu.VMEM((nh, 1, 2 * ta), F32), pltpu.VMEM((nh, hw + ATT_ONES_ROWS, 2 * ta), F32),
                        pltpu.VMEM((nh, 2 * ta, 2 * ta), F32), pltpu.VMEM((nh, 2 * ta, 2 * ta), F32)],
        compiler_params=_cparams(("parallel", "parallel", "arbitrary")),
        name="diff_attention",
    )(lamv, qt, k, vt, bias_tiles, subln_g)


def _rw_prep_kernel(r_ref, k_ref, v_ref, lo_ref, rp_ref, kp_ref, vp_ref, lop_ref,
                    mu_ref, mulo_ref, w0_ref, w2h_ref, w2l_ref, a0_ref, a2h_ref, a2l_ref, g2_ref,
                    kk_ref, ka_ref,
                    r_o, lw_o, km_o, v_o, kk_o, a_o, g_o, *, tiles_per_seq):
    i = pl.program_id(0)
    keep = (i % tiles_per_seq != 0).astype(F32)

    def shifted(cur_ref, prev_ref, mu):
        p = cur_ref[...]
        row = lax.broadcasted_iota(jnp.int32, p.shape, 0)
        prev_last = prev_ref[7:8, :] * keep
        pm1 = jnp.where(row == 0, prev_last, pltpu.roll(p, 1, 0))
        return p + mu * (pm1 - p)

    r = shifted(r_ref, rp_ref, mu_ref[0:1])
    k = shifted(k_ref, kp_ref, mu_ref[1:2])
    v = shifted(v_ref, vp_ref, mu_ref[2:3])
    lo = shifted(lo_ref, lop_ref, mulo_ref[...])

    def dot3w(a, wh_ref, wl_ref):
        ah, al = _split2(a)
        wh = wh_ref[...]
        d = functools.partial(jnp.dot, preferred_element_type=F32)
        return d(ah, wh) + (d(ah, wl_ref[...]) + d(al, wh))

    zw = w0_ref[...] + dot3w(jnp.tanh(lo[:, 0:128]), w2h_ref, w2l_ref)
    nz = -zw
    softplus = jnp.maximum(nz, 0.0) + jnp.log(1.0 + jnp.exp(-jnp.abs(nz)))
    w_log = -softplus - 0.5
    lw_o[...] = -jnp.exp(w_log)
    a_ic = jax.nn.sigmoid(a0_ref[...] + dot3w(lo[:, 128:256], a2h_ref, a2l_ref))
    g_o[...] = jnp.dot(jax.nn.sigmoid(lo[:, 256:512]).astype(BF16), g2_ref[...], preferred_element_type=F32)
    r_o[...] = r
    v_o[...] = v
    kk_o[...] = k * kk_ref[...]
    km_o[...] = k * (1.0 + (a_ic - 1.0) * ka_ref[...])
    a_o[...] = a_ic


def _rw_prep(rest, seq, mu_rkv, mu_lo, w0, w2h, w2l, a0, a2h, a2l, g2, k_k, k_a, tm=256):
    t = rest.shape[0]
    d = D_MODEL
    cur = lambda c, w: pl.BlockSpec((tm, w), lambda i: (i, c))
    prev = lambda c, w: pl.BlockSpec((8, w), lambda i: (jnp.maximum(i * (tm // 8) - 1, 0), c))
    full = lambda a: pl.BlockSpec(a.shape, lambda i: (0,) * a.ndim)
    params = (mu_rkv, mu_lo, w0, w2h, w2l, a0, a2h, a2l, g2, k_k, k_a)
    out = jax.ShapeDtypeStruct((t, d), F32)
    return pl.pallas_call(
        functools.partial(_rw_prep_kernel, tiles_per_seq=seq // tm),
        grid=(t // tm,),
        in_specs=[cur(0, d), cur(1, d), cur(2, d), cur(10, 512),
                  prev(0, d), prev(1, d), prev(2, d), prev(10, 512)] + [full(a) for a in params],
        out_specs=[pl.BlockSpec((tm, d), lambda i: (i, 0))] * 7,
        out_shape=[out] * 7,
        compiler_params=_cparams(("parallel",)),
        name="rwkv_prep",
    )(rest, rest, rest, rest, rest, rest, rest, rest, *params)


def _dot(a, b, dims=(((1,), (0,)), ((), ()))):
    return lax.dot_general(a.astype(BF16), b.astype(BF16), dims, preferred_element_type=F32)


def _dot_split_lhs(a, b_bf16):
    ah, al = _split2(a)
    return (jnp.dot(ah, b_bf16, preferred_element_type=F32) + jnp.dot(al, b_bf16, preferred_element_type=F32))


def _pair_consts():
    c, n = RW_CHUNK, RW_HD
    lane = lax.broadcasted_iota(jnp.int32, (c, 2 * n), 1)
    row = lax.broadcasted_iota(jnp.int32, (2 * c, 2 * c), 0)
    col = lax.broadcasted_iota(jnp.int32, (2 * c, 2 * c), 1)
    same_head = (row // n) == (col // n)
    return lane < n, same_head.astype(BF16)


def _expand(x2, head_a):
    zero = jnp.zeros_like(x2)
    return jnp.concatenate([jnp.where(head_a, x2, zero), jnp.where(head_a, zero, x2)], axis=0)


def _compact(xm):
    return xm[0:RW_CHUNK] + xm[RW_CHUNK:2 * RW_CHUNK]


def _rw_intra_kernel(r_ref, lw_ref, km_ref, v_ref, kk_ref, a_ref, rk_ref,
                     pt_ref, qt_ref, rh_ref, y0_ref, bon_ref):
    c, n = RW_CHUNK, RW_HD
    head_a, seg_ones = _pair_consts()
    row = lax.broadcasted_iota(jnp.int32, (2 * c, 2 * c), 0)
    col = lax.broadcasted_iota(jnp.int32, (2 * c, 2 * c), 1)
    strict = (row % c) > (col % c)
    incl = (row % c) >= (col % c)
    trow = lax.broadcasted_iota(jnp.int32, (c, c), 0)
    tcol = lax.broadcasted_iota(jnp.int32, (c, c), 1)
    tri_ones = (trow >= tcol).astype(BF16)
    diag2 = lax.broadcasted_iota(jnp.int32, (c, 2 * n), 0) == (lax.broadcasted_iota(jnp.int32, (c, 2 * n), 1) % n)

    chunks = range(r_ref.shape[1] // c)
    sls = [slice(ci * c, (ci + 1) * c) for ci in chunks]
    d = functools.partial(jnp.dot, preferred_element_type=F32)

    def cumsum(lw2):
        l1 = lw2.astype(BF16)
        rem = lw2 - l1.astype(F32)
        l2 = rem.astype(BF16)
        l3 = (rem - l2.astype(F32)).astype(BF16)
        return d(tri_ones, l1) + (d(tri_ones, l2) + d(tri_ones, l3))

    cum = [cumsum(lw_ref[0, sl, :]) for sl in sls]
    nrm = [jnp.sqrt(_dot_split_lhs(kk_ref[0, sl, :] * kk_ref[0, sl, :], seg_ones)) for sl in sls]
    big, atm, vm, bdm, kdm = [], [], [], [], []
    for ci, sl in enumerate(sls):
        lw2, cm = lw_ref[0, sl, :], cum[ci]
        cum_last = cm[c - 1:c, :]
        g_inv = jnp.exp(-cm)
        g_rest = jnp.exp(cum_last - cm)
        kkn = kk_ref[0, sl, :] / jnp.maximum(nrm[ci], 1e-12)
        b2 = kkn * a_ref[0, sl, :]
        km2 = km_ref[0, sl, :]
        a_m = _expand(-kkn * jnp.exp(cm - lw2), head_a)
        r_m = _expand(r_ref[0, sl, :] * jnp.exp(cm), head_a)
        b_m = _expand(b2 * g_inv, head_a)
        k_m = _expand(km2 * g_inv, head_a)
        atm.append(a_m)
        bdm.append(_expand(b2 * g_rest, head_a))
        kdm.append(_expand(km2 * g_rest, head_a))
        vm.append(_expand(v_ref[0, sl, :], head_a))
        big.append(_dot(jnp.concatenate([a_m, r_m], axis=0), jnp.concatenate([b_m, k_m], axis=0), NT_DIMS))
    lmat = [jnp.where(strict, bg[0:128, 0:128], 0.0) for bg in big]
    m_rb = [jnp.where(incl, bg[128:256, 0:128], 0.0) for bg in big]
    mv = [_dot(jnp.concatenate([jnp.where(strict, bg[0:128, 128:256], 0.0),
                                jnp.where(incl, bg[128:256, 128:256], 0.0)], axis=0), vm[ci])
          for ci, bg in enumerate(big)]
    x = [jnp.concatenate([atm[ci], mv[ci][0:128]], axis=1) for ci in chunks]
    lp = lmat
    for it in range(6):
        x = [x[ci] + _dot(lp[ci], x[ci]) for ci in chunks]
        if it < 5:
            lp = [_dot(m, m) for m in lp]
    rb = [_dot(m_rb[ci], x[ci]) for ci in chunks]
    bx = [_dot(bdm[ci], x[ci], TN_DIMS) for ci in chunks]
    kv = [_dot(kdm[ci], vm[ci], TN_DIMS) for ci in chunks]
    for ci, sl in enumerate(sls):
        g_last = jnp.exp(cum[ci][c - 1:c, :])
        pt_ref[0, sl, :] = _compact(bx[ci][:, 0:128]) + jnp.where(diag2, g_last, 0.0)
        qt_ref[0, sl, :] = _compact(bx[ci][:, 128:256] + kv[ci])
        rh_ref[0, sl, :] = (r_ref[0, sl, :] * jnp.exp(cum[ci]) + _compact(rb[ci][:, 0:128])).astype(rh_ref.dtype)
        y0_ref[0, sl, :] = _compact(rb[ci][:, 128:256] + mv[ci][128:256])
        bon_ref[0, sl, :] = (_dot_split_lhs(r_ref[0, sl, :] * km_ref[0, sl, :] * rk_ref[...], seg_ones)
                             * v_ref[0, sl, :])


def _rw_intra(r, lw, km, v, kk, a_ic, r_k, tb=512):
    b, s, d = r.shape
    blk = pl.BlockSpec((1, tb, 128), lambda bi, hp, ti: (bi, ti, hp))
    par = pl.BlockSpec((1, 128), lambda bi, hp, ti: (0, hp))
    out = jax.ShapeDtypeStruct((b, s, d), F32)
    return pl.pallas_call(
        _rw_intra_kernel,
        grid=(b, d // 128, s // tb),
        in_specs=[blk] * 6 + [par],
        out_specs=[blk] * 5,
        out_shape=[out, out, jax.ShapeDtypeStruct((b, s, d), BF16), out, out],
        compiler_params=_cparams(("parallel", "parallel", "parallel")),
        name="rwkv_intra",
    )(r, lw, km, v, kk, a_ic, r_k)


def _rw_scan_kernel(pt_ref, qt_ref, rh_ref, y0_ref, y_ref, z_sc):
    c, n = RW_CHUNK, RW_HD
    pairs = range(z_sc.shape[0])
    head_a, _ = _pair_consts()
    lns = [slice(hp * 2 * n, (hp + 1) * 2 * n) for hp in pairs]

    @pl.when(pl.program_id(1) == 0)
    def _():
        z_sc[...] = jnp.zeros(z_sc.shape, F32)

    z = [z_sc[hp] for hp in pairs]
    for ci in range(pt_ref.shape[1] // c):
        sl = slice(ci * c, (ci + 1) * c)
        y = [_dot(rh_ref[0, sl, lns[hp]], z[hp]) for hp in pairs]
        z = [_dot3(_expand(pt_ref[0, sl, lns[hp]], head_a), z[hp]) + _expand(qt_ref[0, sl, lns[hp]], head_a)
             for hp in pairs]
        for hp in pairs:
            y_ref[0, sl, lns[hp]] = y[hp] + y0_ref[0, sl, lns[hp]]
    for hp in pairs:
        z_sc[hp] = z[hp]


def _rw_scan(pt, qt, rh, y0, tb=512):
    b, s, d = pt.shape
    blk = pl.BlockSpec((1, tb, d), lambda bi, ti: (bi, ti, 0))
    return pl.pallas_call(
        _rw_scan_kernel,
        grid=(b, s // tb),
        in_specs=[blk] * 4,
        out_specs=blk,
        out_shape=jax.ShapeDtypeStruct((b, s, d), F32),
        scratch_shapes=[pltpu.VMEM((d // 128, 128, 128), F32)],
        compiler_params=_cparams(("parallel", "arbitrary")),
        name="rwkv_scan",
    )(pt, qt, rh, y0)


def _rw_chunk_kernel(r_ref, k_ref, v_ref, lo_ref, mu_ref, mulo_ref, w0_ref, w2h_ref, w2l_ref,
                     a0_ref, a2h_ref, a2l_ref, g2_ref, kk_ref, ka_ref, rk_ref,
                     y_ref, bon_ref, g_ref,
                     z_sc, pt_sc, qt_sc, rh_sc, y0_sc, prev_sc, *, nchunks):
    c, n = RW_CHUNK, RW_HD
    pw = 2 * n
    pairs = range(z_sc.shape[0])
    lns = [slice(hp * pw, (hp + 1) * pw) for hp in pairs]
    head_a, seg_ones = _pair_consts()

    @pl.when(pl.program_id(1) == 0)
    def _():
        z_sc[...] = jnp.zeros(z_sc.shape, F32)
        pt_sc[...] = jnp.zeros(pt_sc.shape, F32)
        qt_sc[...] = jnp.zeros(qt_sc.shape, F32)
        rh_sc[...] = jnp.zeros(rh_sc.shape, rh_sc.dtype)
        y0_sc[...] = jnp.zeros(y0_sc.shape, F32)
        prev_sc[...] = jnp.zeros(prev_sc.shape, F32)

    z = [z_sc[hp] for hp in pairs]
    for hp in pairs:
        y_ref[0, :, lns[hp]] = _dot(rh_sc[:, lns[hp]], z[hp]) + y0_sc[:, lns[hp]]
    z_new = [_dot3(_expand(pt_sc[:, lns[hp]], head_a), z[hp]) + _expand(qt_sc[:, lns[hp]], head_a)
             for hp in pairs]
    for hp in pairs:
        z_sc[hp] = z_new[hp]

    row = lax.broadcasted_iota(jnp.int32, (2 * c, 2 * c), 0)
    col = lax.broadcasted_iota(jnp.int32, (2 * c, 2 * c), 1)
    strict = (row % c) > (col % c)
    incl = (row % c) >= (col % c)
    trow = lax.broadcasted_iota(jnp.int32, (c, c), 0)
    tcol = lax.broadcasted_iota(jnp.int32, (c, c), 1)
    tri_ones = (trow >= tcol).astype(BF16)
    diag2 = lax.broadcasted_iota(jnp.int32, (c, pw), 0) == (lax.broadcasted_iota(jnp.int32, (c, pw), 1) % n)
    d = functools.partial(jnp.dot, preferred_element_type=F32)

    def cumsum(lw2):
        l1 = lw2.astype(BF16)
        rem = lw2 - l1.astype(F32)
        l2 = rem.astype(BF16)
        l3 = (rem - l2.astype(F32)).astype(BF16)
        return d(tri_ones, l1) + (d(tri_ones, l2) + d(tri_ones, l3))

    ci = pl.program_id(1)
    trow1 = lax.broadcasted_iota(jnp.int32, (c, 1), 0)
    keep_prev = ci < nchunks - 1

    def shifted(cur_ref, slot, mu):
        p = cur_ref[0]
        w = p.shape[1]
        before = prev_sc[slot:slot + 1, 0:w]
        pm1 = jnp.where(trow1 == 0, before, pltpu.roll(p, 1, 0))
        prev_sc[slot:slot + 1, 0:w] = jnp.where(keep_prev, p[c - 1:c, :], before)
        return p + mu * (pm1 - p)

    r = shifted(r_ref, 0, mu_ref[0:1])
    k = shifted(k_ref, 1, mu_ref[1:2])
    v = shifted(v_ref, 2, mu_ref[2:3])
    lo = shifted(lo_ref, 3, mulo_ref[...])

    def dot3w(a, wh_ref, wl_ref):
        ah, al = _split2(a)
        wh = wh_ref[...]
        return d(ah, wh) + (d(ah, wl_ref[...]) + d(al, wh))

    nz = -(w0_ref[...] + dot3w(jnp.tanh(lo[:, 0:128]), w2h_ref, w2l_ref))
    softplus = jnp.maximum(nz, 0.0) + jnp.log(1.0 + jnp.exp(-jnp.abs(nz)))
    lw = -jnp.exp(-softplus - 0.5)
    a_ic = jax.nn.sigmoid(a0_ref[...] + dot3w(lo[:, 128:256], a2h_ref, a2l_ref))
    g_ref[0] = d(jax.nn.sigmoid(lo[:, 256:512]).astype(BF16), g2_ref[...])
    kk = k * kk_ref[...]
    km = k * (1.0 + (a_ic - 1.0) * ka_ref[...])

    cum = [cumsum(lw[:, ln]) for ln in lns]
    nrm = [jnp.sqrt(_dot_split_lhs(kk[:, ln] * kk[:, ln], seg_ones)) for ln in lns]
    big, atm, vm, bdm, kdm = [], [], [], [], []
    for hp, ln in enumerate(lns):
        lw2, cm = lw[:, ln], cum[hp]
        g_inv = jnp.exp(-cm)
        g_rest = jnp.exp(cm[c - 1:c, :] - cm)
        kkn = kk[:, ln] / jnp.maximum(nrm[hp], 1e-12)
        b2 = kkn * a_ic[:, ln]
        km2 = km[:, ln]
        a_m = _expand(-kkn * jnp.exp(cm - lw2), head_a)
        r_m = _expand(r[:, ln] * jnp.exp(cm), head_a)
        b_m = _expand(b2 * g_inv, head_a)
        k_m = _expand(km2 * g_inv, head_a)
        atm.append(a_m)
        bdm.append(_expand(b2 * g_rest, head_a))
        kdm.append(_expand(km2 * g_rest, head_a))
        vm.append(_expand(v[:, ln], head_a))
        big.append(_dot(jnp.concatenate([a_m, r_m], axis=0), jnp.concatenate([b_m, k_m], axis=0), NT_DIMS))
    lmat = [jnp.where(strict, bg[0:pw, 0:pw], 0.0) for bg in big]
    m_rb = [jnp.where(incl, bg[pw:2 * pw, 0:pw], 0.0) for bg in big]
    mv = [_dot(jnp.concatenate([jnp.where(strict, bg[0:pw, pw:2 * pw], 0.0),
                                jnp.where(incl, bg[pw:2 * pw, pw:2 * pw], 0.0)], axis=0), vm[hp])
          for hp, bg in enumerate(big)]
    x = [jnp.concatenate([atm[hp], mv[hp][0:pw]], axis=1) for hp in pairs]
    lp = lmat
    for it in range(6):
        x = [x[hp] + _dot(lp[hp], x[hp]) for hp in pairs]
        if it < 5:
            lp = [_dot(m, m) for m in lp]
    rb = [_dot(m_rb[hp], x[hp]) for hp in pairs]
    bx = [_dot(bdm[hp], x[hp], TN_DIMS) for hp in pairs]
    kv = [_dot(kdm[hp], vm[hp], TN_DIMS) for hp in pairs]
    for hp, ln in enumerate(lns):
        g_last = jnp.exp(cum[hp][c - 1:c, :])
        pt_sc[:, ln] = _compact(bx[hp][:, 0:pw]) + jnp.where(diag2, g_last, 0.0)
        qt_sc[:, ln] = _compact(bx[hp][:, pw:2 * pw] + kv[hp])
        rh_sc[:, ln] = (r[:, ln] * jnp.exp(cum[hp]) + _compact(rb[hp][:, 0:pw])).astype(rh_sc.dtype)
        y0_sc[:, ln] = _compact(rb[hp][:, pw:2 * pw] + mv[hp][pw:2 * pw])
        bon_ref[0, :, ln] = _dot_split_lhs(r[:, ln] * km[:, ln] * rk_ref[:, ln], seg_ones) * v[:, ln]


def _rw_chunks(rest, mu_rkv, mu_lo, w0, w2h, w2l, a0, a2h, a2l, g2, k_k, k_a, r_k):
    b, s, _ = rest.shape
    d = D_MODEL
    c = RW_CHUNK
    nc = s // c
    lora_w = mu_lo.shape[1]
    cur = lambda col, w: pl.BlockSpec((1, c, w), lambda bi, ci: (bi, jnp.minimum(ci, nc - 1), col))
    prev = pl.BlockSpec((1, c, d), lambda bi, ci: (bi, jnp.maximum(ci - 1, 0), 0))
    full = lambda a: pl.BlockSpec(a.shape, lambda bi, ci: (0,) * a.ndim)
    params = (mu_rkv, mu_lo, w0, w2h, w2l, a0, a2h, a2l, g2, k_k, k_a, r_k)
    out = jax.ShapeDtypeStruct((b, s, d), F32)
    lora_col = (3 * d + 2 * d) // lora_w
    return pl.pallas_call(
        functools.partial(_rw_chunk_kernel, nchunks=nc),
        grid=(b, nc + 1),
        in_specs=[cur(0, d), cur(1, d), cur(2, d), cur(lora_col, lora_w)] + [full(a) for a in params],
        out_specs=[prev, cur(0, d), cur(0, d)],
        out_shape=[out, out, out],
        scratch_shapes=[pltpu.VMEM((d // (2 * RW_HD), 2 * RW_HD, 2 * RW_HD), F32),
                        pltpu.VMEM((c, d), F32), pltpu.VMEM((c, d), F32),
                        pltpu.VMEM((c, d), BF16), pltpu.VMEM((c, d), F32),
                        pltpu.VMEM((4, d), F32)],
        compiler_params=_cparams(("parallel", "arbitrary")),
        name="rwkv_chunks",
    )(rest, rest, rest, rest, *params)


def _mix_kernel(oda_ref, yrw_ref, bon_ref, gout_ref, ga_ref, gb_ref, x_ref, wda_ref, wrw_ref, wmix_ref,
                gng_ref, gnb_ref, lng_ref, lnb_ref, o_ref, *, alpha):
    y_da = jnp.dot(oda_ref[...], wda_ref[...], preferred_element_type=F32)
    _, seg_ones = _pair_consts()
    parts = []
    for hp in range(RW_HEADS // 2):
        ln = slice(hp * 2 * RW_HD, (hp + 1) * 2 * RW_HD)
        y = yrw_ref[:, ln]
        mu = _dot_split_lhs(y, seg_ones) * (1.0 / RW_HD)
        yc = y - mu
        var = _dot_split_lhs(yc * yc, seg_ones) * (1.0 / RW_HD)
        parts.append(yc * lax.rsqrt(var + RW_LNX_EPS))
    yn = jnp.concatenate(parts, axis=1) * gng_ref[...] + gnb_ref[...] + bon_ref[...]
    y_rw = jnp.dot((yn * gout_ref[...]).astype(BF16), wrw_ref[...], preferred_element_type=F32)
    mixed = jax.nn.sigmoid(ga_ref[...]) * y_da + jax.nn.sigmoid(gb_ref[...]) * y_rw
    z = alpha * x_ref[...] + jnp.dot(mixed.astype(BF16), wmix_ref[...], preferred_element_type=F32)
    o_ref[...] = _layer_norm_rows(z, lng_ref[...], lnb_ref[...])


def _mix(o_da, y_rw, bonus, g_out, rest, x, w_da, w_rw, w_mix, gn_g, gn_b, ln_g, ln_b, alpha, tm=512):
    t, d = x.shape
    rowb = lambda c: pl.BlockSpec((tm, d), lambda i: (i, c))
    full = lambda a: pl.BlockSpec(a.shape, lambda i: (0,) * a.ndim)
    return pl.pallas_call(
        functools.partial(_mix_kernel, alpha=alpha),
        grid=(t // tm,),
        in_specs=[rowb(0), rowb(0), rowb(0), rowb(0), rowb(3), rowb(4), rowb(0),
                  full(w_da), full(w_rw), full(w_mix), full(gn_g), full(gn_b), full(ln_g), full(ln_b)],
        out_specs=rowb(0),
        out_shape=jax.ShapeDtypeStruct((t, d), F32),
        compiler_params=_cparams(("parallel",)),
        name="mix_merge_ln1",
    )(o_da, y_rw, bonus, g_out, rest, rest, x, w_da, w_rw, w_mix, gn_g, gn_b, ln_g, ln_b)


def _mem_kv_kernel(mem_ref, g_ref, b_ref, w_ref, ck_ref, cv_ref):
    m = _layer_norm_rows(mem_ref[0], g_ref[...], b_ref[...])
    kv = jnp.dot(m.astype(BF16), w_ref[...], preferred_element_type=F32)
    ck_ref[0] = kv[:, :D_MODEL].astype(BF16)
    cv_ref[0] = kv[:, D_MODEL:].astype(BF16)


def _mem_kv(mem, g, b, w_ckv):
    bsz, m, d = mem.shape
    full = lambda a: pl.BlockSpec(a.shape, lambda i: (0,) * a.ndim)
    blk = pl.BlockSpec((1, m, d), lambda i: (i, 0, 0))
    return pl.pallas_call(
        _mem_kv_kernel,
        grid=(bsz,),
        in_specs=[blk, full(g), full(b), full(w_ckv)],
        out_specs=[blk, blk],
        out_shape=[jax.ShapeDtypeStruct((bsz, m, d), BF16)] * 2,
        compiler_params=_cparams(("parallel",)),
        name="mem_kv",
    )(mem, g, b, w_ckv)


def _cross_kernel(x_ref, ck_ref, cv_ref, wq_ref, wo_ref, lng_ref, lnb_ref, o_ref, *, alpha):
    x = x_ref[...]
    cq = jnp.dot(x.astype(BF16), wq_ref[...], preferred_element_type=F32)
    cq = (cq * (CA_HD ** -0.5)).astype(BF16)
    outs = []
    for h in range(CA_HEADS):
        sl = slice(h * CA_HD, (h + 1) * CA_HD)
        s = lax.dot_general(cq[:, sl], ck_ref[0, :, sl], NT_DIMS, preferred_element_type=F32)
        s = s - jnp.max(s, axis=-1, keepdims=True)
        p = jnp.exp(s)
        p = p / jnp.sum(p, axis=-1, keepdims=True)
        outs.append(jnp.dot(p.astype(BF16), cv_ref[0, :, sl], preferred_element_type=F32))
    co = jnp.dot(jnp.concatenate(outs, axis=1).astype(BF16), wo_ref[...], preferred_element_type=F32)
    o_ref[...] = _layer_norm_rows(alpha * x + co, lng_ref[...], lnb_ref[...])


def _cross_attention(x, ck, cv, w_cq, w_co, ln_g, ln_b, seq, alpha, tm=512):
    t, d = x.shape
    tiles_per_seq = seq // tm
    full = lambda a: pl.BlockSpec(a.shape, lambda i: (0,) * a.ndim)
    rowb = pl.BlockSpec((tm, d), lambda i: (i, 0))
    memb = pl.BlockSpec((1,) + ck.shape[1:], lambda i: (i // tiles_per_seq, 0, 0))
    return pl.pallas_call(
        functools.partial(_cross_kernel, alpha=alpha),
        grid=(t // tm,),
        in_specs=[rowb, memb, memb, full(w_cq), full(w_co), full(ln_g), full(ln_b)],
        out_specs=rowb,
        out_shape=jax.ShapeDtypeStruct((t, d), F32),
        compiler_params=_cparams(("parallel",)),
        name="cross_attention_ln2",
    )(x, ck, cv, w_cq, w_co, ln_g, ln_b)


def _ffn_kernel(x_ref, wg_ref, wu_ref, wo_ref, lng_ref, lnb_ref, o_ref, xb_ref, acc_ref, *, alpha):
    j = pl.program_id(1)

    @pl.when(j == 0)
    def _():
        xb_ref[...] = x_ref[...].astype(BF16)
        acc_ref[...] = jnp.zeros(acc_ref.shape, F32)

    xb = xb_ref[...]
    gate = jnp.dot(xb, wg_ref[...], preferred_element_type=F32)
    up = jnp.dot(xb, wu_ref[...], preferred_element_type=F32)
    h = (gate * jax.nn.sigmoid(gate) * up).astype(BF16)
    acc_ref[...] += jnp.dot(h, wo_ref[...], preferred_element_type=F32)

    @pl.when(j == pl.num_programs(1) - 1)
    def _():
        o_ref[...] = _layer_norm_rows(alpha * x_ref[...] + acc_ref[...], lng_ref[...], lnb_ref[...])


def _ffn(x, w_in, w_out, ln_g, ln_b, alpha, tm=512, tf=1408):
    t, d = x.shape
    nf = D_FF // tf
    full = lambda a: pl.BlockSpec(a.shape, lambda i, j: (0,) * a.ndim)
    rowb = pl.BlockSpec((tm, d), lambda i, j: (i, 0))
    return pl.pallas_call(
        functools.partial(_ffn_kernel, alpha=alpha),
        grid=(t // tm, nf),
        in_specs=[rowb,
                  pl.BlockSpec((d, tf), lambda i, j: (0, j)),
                  pl.BlockSpec((d, tf), lambda i, j: (0, nf + j)),
                  pl.BlockSpec((tf, d), lambda i, j: (j, 0)),
                  full(ln_g), full(ln_b)],
        out_specs=rowb,
        out_shape=jax.ShapeDtypeStruct((t, d), F32),
        scratch_shapes=[pltpu.VMEM((tm, d), BF16), pltpu.VMEM((tm, d), F32)],
        compiler_params=_cparams(("parallel", "arbitrary")),
        name="swiglu_ffn_ln3",
    )(x, w_in, w_in, w_out, ln_g, ln_b)


def _pad_cols(a, width):
    return jnp.pad(a, ((0, 0), (0, width - a.shape[1])))


def _pad_rows(a, height):
    return jnp.pad(a, ((0, height - a.shape[0]), (0, 0)))


def kernel(x, mem, rel_bias, w_in, shift_mu, lambda_q1, lambda_k1, lambda_q2, lambda_k2, da_subln_g, w_da_proj, rw_w0, rw_w2, rw_a0, rw_a2, rw_g2, rw_k_k, rw_k_a, rw_r_k, rw_lnx_g, rw_lnx_b, w_rw_proj, w_mix_out, ln1_g, ln1_b, mem_ln_g, mem_ln_b, w_cq, w_ckv, w_co, ln2_g, ln2_b, w_ffn_in, w_ffn_out, ln3_g, ln3_b):
    bsz, seq, d = x.shape
    depth = w_in.shape[0]
    alpha = (2.0 * depth) ** 0.25
    t = bsz * seq
    row = lambda a: a.reshape(1, -1).astype(F32)
    bias_tiles = _attn_bias_tiles(rel_bias)
    xf = x.reshape(t, d)

    qkv_w = 3 * D_MODEL
    rw0 = qkv_w
    lo0 = rw0 + 3 * D_MODEL
    gate0 = lo0 + DECAY_LORA + AAA_LORA + GATE_LORA

    for l in range(depth):
        w = w_in[l]
        w_qt = w[:, :D_MODEL].T.astype(BF16)
        w_k = w[:, D_MODEL:2 * D_MODEL].astype(BF16)
        w_vt = w[:, 2 * D_MODEL:qkv_w].T.astype(BF16)
        w_rest = jnp.concatenate([
            w[:, rw0:lo0], w[:, gate0:gate0 + 2 * D_MODEL],
            _pad_cols(w[:, lo0:lo0 + DECAY_LORA], 128),
            _pad_cols(w[:, lo0 + DECAY_LORA:lo0 + DECAY_LORA + AAA_LORA], 128),
            _pad_cols(w[:, lo0 + DECAY_LORA + AAA_LORA:gate0], 256)], axis=1).astype(BF16)
        mu = shift_mu[l].astype(F32)
        mu_rkv = mu[:3 * D_MODEL].reshape(3, D_MODEL)
        o1 = 3 * D_MODEL
        mu_lo = jnp.concatenate([
            _pad_cols(mu[None, o1:o1 + DECAY_LORA], 128),
            _pad_cols(mu[None, o1 + DECAY_LORA:o1 + DECAY_LORA + AAA_LORA], 128),
            _pad_cols(mu[None, o1 + DECAY_LORA + AAA_LORA:], 256)], axis=1)
        w2 = _pad_rows(rw_w2[l].astype(F32), 128)
        a2 = _pad_rows(rw_a2[l].astype(F32), 128)
        w2h, w2l = _split2(w2)
        a2h, a2l = _split2(a2)
        g2 = _pad_rows(rw_g2[l], 256).astype(BF16)

        qt, kda, vt = _qkv_proj(xf, w_qt, w_k, w_vt, bsz, seq)
        rest = _matmul(xf, w_rest, F32, 1024, 1408, "in_proj_rest")

        lam_init = 0.8 - 0.6 * math.exp(-0.3 * l)
        lamv = jnp.stack([lambda_q1[l], lambda_k1[l], lambda_q2[l], lambda_k2[l]]).astype(F32)
        o_da = _diff_attention(qt, kda.reshape(bsz, seq, d), vt, lamv, bias_tiles, row(da_subln_g[l]), lam_init)

        y_rw, bon, g_out = _rw_chunks(
            rest.reshape(bsz, seq, rest.shape[1]), mu_rkv, mu_lo, row(rw_w0[l]), w2h, w2l, row(rw_a0[l]),
            a2h, a2l, g2, row(rw_k_k[l]), row(rw_k_a[l]), row(rw_r_k[l]))

        x1 = _mix(o_da.reshape(t, d), y_rw.reshape(t, d), bon.reshape(t, d), g_out.reshape(t, d), rest, xf,
                  w_da_proj[l].astype(BF16), w_rw_proj[l].astype(BF16), w_mix_out[l].astype(BF16),
                  row(rw_lnx_g[l]), row(rw_lnx_b[l]), row(ln1_g[l]), row(ln1_b[l]), alpha)

        ck, cv = _mem_kv(mem, row(mem_ln_g[l]), row(mem_ln_b[l]), w_ckv[l].astype(BF16))
        x2 = _cross_attention(x1, ck, cv, w_cq[l].astype(BF16), w_co[l].astype(BF16),
                              row(ln2_g[l]), row(ln2_b[l]), seq, alpha)

        xf = _ffn(x2, w_ffn_in[l].astype(BF16), w_ffn_out[l].astype(BF16),
                  row(ln3_g[l]), row(ln3_b[l]), alpha)
    return xf.reshape(bsz, seq, d)
```

```python
import functools
import math

import jax
import jax.numpy as jnp
from jax import lax
from jax.experimental import pallas as pl
from jax.experimental.pallas import tpu as pltpu

F32 = jnp.float32
BF16 = jnp.bfloat16

D_MODEL = 1024
DA_HEADS = 8
DA_HD = 64
NUM_BUCKETS = 32
MAX_DISTANCE = 128
RW_HD = 64
RW_HEADS = D_MODEL // RW_HD
DECAY_LORA = 64
AAA_LORA = 64
GATE_LORA = 160
RW_LNX_EPS = 64e-5
MEM_TOKENS = 256
CA_HEADS = 4
CA_HD = D_MODEL // CA_HEADS
D_FF = 2816
LN_EPS = 1e-5
NEG_BIG = -1e30

LOG2E = math.log2(math.e)
ATT_HEADS_PER_STEP = 2
ATT_QT = 512
ATT_KB = 256
ATT_BIAS_TYPES = 4
ATT_ROTATE = 3
ATT_ONES_ROWS = 16
RW_CHUNK = 64
VMEM_LIMIT = 56 * 1024 * 1024

NT_DIMS = (((1,), (1,)), ((), ()))
TN_DIMS = (((0,), (0,)), ((), ()))


def _cparams(sem, vmem=VMEM_LIMIT):
    return pltpu.CompilerParams(dimension_semantics=sem, vmem_limit_bytes=vmem)


def _split2(a):
    hi = a.astype(BF16)
    lo = (a - hi.astype(F32)).astype(BF16)
    return hi, lo


def _dot3(a, b, dims=(((1,), (0,)), ((), ()))):
    ah, al = _split2(a)
    bh, bl = _split2(b)
    d = functools.partial(lax.dot_general, dimension_numbers=dims, preferred_element_type=F32)
    return d(ah, bh) + (d(ah, bl) + d(al, bh))


def _layer_norm_rows(z, g, b):
    mu = jnp.mean(z, axis=-1, keepdims=True)
    zc = z - mu
    var = jnp.mean(zc * zc, axis=-1, keepdims=True)
    return zc * lax.rsqrt(var + LN_EPS) * g + b


def _mm_kernel(x_ref, w_ref, o_ref, xb_ref):
    @pl.when(pl.program_id(1) == 0)
    def _():
        xb_ref[...] = x_ref[...].astype(BF16)

    o_ref[...] = jnp.dot(xb_ref[...], w_ref[...], preferred_element_type=F32).astype(o_ref.dtype)


def _matmul(x, w, out_dtype, tm, tn, name):
    t, k = x.shape
    n = w.shape[1]
    return pl.pallas_call(
        _mm_kernel,
        grid=(t // tm, n // tn),
        in_specs=[pl.BlockSpec((tm, k), lambda i, j: (i, 0)),
                  pl.BlockSpec((k, tn), lambda i, j: (0, j))],
        out_specs=pl.BlockSpec((tm, tn), lambda i, j: (i, j)),
        out_shape=jax.ShapeDtypeStruct((t, n), out_dtype),
        scratch_shapes=[pltpu.VMEM((tm, k), BF16)],
        compiler_params=_cparams(("parallel", "arbitrary")),
        name=name,
    )(x, w)


def _qkv_kernel(x_ref, wqt_ref, wk_ref, wvt_ref, qt_ref, k_ref, vt_ref):
    xb = x_ref[...].astype(BF16)
    k_ref[...] = jnp.dot(xb, wk_ref[...], preferred_element_type=F32).astype(BF16)
    qt = lax.dot_general(wqt_ref[...], xb, NT_DIMS, preferred_element_type=F32)
    qt_ref[0] = (qt * (DA_HD ** -0.5 * LOG2E)).astype(BF16)
    vt_ref[0] = lax.dot_general(wvt_ref[...], xb, NT_DIMS, preferred_element_type=F32).astype(BF16)


def _qkv_proj(x, wqt, wk, wvt, bsz, seq, tm=512):
    t, d = x.shape
    tiles_per_seq = seq // tm
    full = lambda a: pl.BlockSpec(a.shape, lambda i: (0,) * a.ndim)
    tblk = pl.BlockSpec((1, d, tm), lambda i: (i // tiles_per_seq, 0, i % tiles_per_seq))
    rowb = pl.BlockSpec((tm, d), lambda i: (i, 0))
    tshape = jax.ShapeDtypeStruct((bsz, d, seq), BF16)
    return pl.pallas_call(
        _qkv_kernel,
        grid=(t // tm,),
        in_specs=[rowb, full(wqt), full(wk), full(wvt)],
        out_specs=[tblk, rowb, tblk],
        out_shape=[tshape, jax.ShapeDtypeStruct((t, d), BF16), tshape],
        compiler_params=_cparams(("parallel",)),
        name="in_proj_qkv",
    )(x, wqt, wk, wvt)


def _t5_bucket(rel):
    n = jnp.maximum(rel, 0)
    max_exact = NUM_BUCKETS // 2
    nf = jnp.maximum(n, 1).astype(jnp.float32)
    large = max_exact + jnp.floor(jnp.log(nf / max_exact) / math.log(MAX_DISTANCE / max_exact)
                                  * (NUM_BUCKETS - max_exact)).astype(jnp.int32)
    large = jnp.minimum(large, NUM_BUCKETS - 1)
    return jnp.where(n < max_exact, n, large)


def _col_max(s):
    while s.shape[0] > 8:
        half = s.shape[0] // 2
        s = jnp.maximum(s[:half], s[half:])
    return jnp.max(s, axis=0, keepdims=True)


def _attn_unit_tables(seq):
    kb_per_qt = ATT_QT // ATT_KB
    qts, kbs, types = [], [], []
    for qt in range(seq // ATT_QT):
        nkb = (qt + 1) * kb_per_qt
        for kb in range(nkb):
            from_end = nkb - 1 - kb
            qts.append(qt)
            kbs.append(kb)
            types.append(ATT_BIAS_TYPES - 1 - from_end if from_end < ATT_BIAS_TYPES - 1 else 0)
    assert len(qts) % ATT_ROTATE == 0
    nunits = len(qts)
    for _ in range(ATT_ROTATE - 1):
        qts.append(0)
        kbs.append(0)
        types.append(0)
    as_i32 = lambda v: jnp.asarray(v, jnp.int32)
    return nunits, as_i32(qts), as_i32(kbs), as_i32(types)


def _attn_flat_bias(rel_bias):
    tab = rel_bias.astype(F32) - rel_bias[NUM_BUCKETS - 1].astype(F32)[None, :]
    base = jnp.arange(ATT_QT)[None, :] - jnp.arange(ATT_KB)[:, None]

    def tile(key_offset):
        rel = base - key_offset
        onehot = (_t5_bucket(rel)[:, :, None] == jnp.arange(NUM_BUCKETS)[None, None, :]).astype(F32)
        vals = jnp.einsum('kqb,bh->hkq', onehot, tab, precision=lax.Precision.HIGHEST)
        return jnp.where((rel >= 0)[None], vals, NEG_BIG)

    zeros = jnp.zeros((rel_bias.shape[1], ATT_KB, ATT_QT), F32)
    return jnp.stack([zeros, tile(-ATT_KB), tile(0), tile(ATT_KB)], axis=1) * LOG2E


def _attn_flat_kernel(tq_ref, tk_ref, tt_ref, lamv_ref, qt_ref, k_ref, vt_ref, bias_ref, g_ref, o_ref,
                      m_sc, acc_sc, sa_sc, sb_sc, sc_sc, ma_sc, mb_sc, mc_sc, *, lam_init, nunits):
    hw = 2 * DA_HD
    heads = range(ATT_HEADS_PER_STEP)
    hrows = [slice(hh * hw, (hh + 1) * hw) for hh in heads]
    ntiles = m_sc.shape[0]

    lv = lamv_ref[...]
    lam = (jnp.exp(jnp.sum(lv[0:1] * lv[1:2], axis=-1, keepdims=True))
           - jnp.exp(jnp.sum(lv[2:3] * lv[3:4], axis=-1, keepdims=True)) + lam_init)

    m_sc[...] = jnp.full(m_sc.shape, NEG_BIG, F32)
    acc_sc[...] = jnp.zeros(acc_sc.shape, F32)
    rowi = lax.broadcasted_iota(jnp.int32, (hw, ATT_QT), 0)

    streams = [(hh, mp, slice(mp * ATT_QT, (mp + 1) * ATT_QT)) for hh in heads for mp in range(2)]

    def fill(buf, mbuf, u):
        q0 = pl.multiple_of(tq_ref[u] * ATT_QT, ATT_QT)
        k0 = pl.multiple_of(tk_ref[u] * ATT_KB, ATT_KB)
        ty = tt_ref[u]
        s = []
        for hh, mp, _ in streams:
            q = qt_ref[0, hrows[hh], pl.ds(q0, ATT_QT)]
            in_map = (rowi < DA_HD) if mp == 0 else (rowi >= DA_HD)
            s.append(jnp.dot(k_ref[0, pl.ds(k0, ATT_KB), hrows[hh]], jnp.where(in_map, q, jnp.zeros_like(q)),
                             preferred_element_type=F32))
        for i, (hh, mp, ln) in enumerate(streams):
            sb = s[i] + bias_ref[hh, ty]
            buf[hh, :, ln] = sb
            mbuf[hh, :, ln] = _col_max(sb)

    def drain(buf, mbuf, u):
        t = tq_ref[u]
        k0 = pl.multiple_of(tk_ref[u] * ATT_KB, ATT_KB)
        m_prev = [m_sc[t, hh, :, ln] for hh, _, ln in streams]
        m_new = [jnp.maximum(m_prev[i], mbuf[hh, :, ln]) for i, (hh, _, ln) in enumerate(streams)]
        alpha = [jnp.exp2(m_prev[i] - m_new[i]) for i in range(len(streams))]
        p = [jnp.exp2((buf[hh, :, ln] - m_new[i]).astype(BF16)) for i, (hh, _, ln) in enumerate(streams)]
        ones = jnp.ones((ATT_ONES_ROWS, ATT_KB), BF16)
        pv = [jnp.dot(jnp.concatenate([vt_ref[0, hrows[hh], pl.ds(k0, ATT_KB)], ones], axis=0), p[i],
                      preferred_element_type=F32) for i, (hh, _, _) in enumerate(streams)]
        for i, (hh, _, ln) in enumerate(streams):
            acc_sc[t, hh, :, ln] = alpha[i] * acc_sc[t, hh, :, ln] + pv[i]
            m_sc[t, hh, :, ln] = m_new[i]

    fill(sa_sc, ma_sc, 0)
    fill(sb_sc, mb_sc, 1)

    def trip(i, carry):
        u = i * ATT_ROTATE
        fill(sc_sc, mc_sc, u + 2)
        drain(sa_sc, ma_sc, u)
        fill(sa_sc, ma_sc, u + 3)
        drain(sb_sc, mb_sc, u + 1)
        fill(sb_sc, mb_sc, u + 4)
        drain(sc_sc, mc_sc, u + 2)
        return carry

    lax.fori_loop(0, nunits // ATT_ROTATE, trip, 0)

    def finish(t, carry):
        q0 = pl.multiple_of(t * ATT_QT, ATT_QT)
        for hh in heads:
            a = acc_sc[t, hh]
            acc = a[0:hw, :] * (1.0 / a[hw:hw + 1, :])
            ot = acc[:, 0:ATT_QT] - lam * acc[:, ATT_QT:2 * ATT_QT]
            ms = jnp.mean(ot * ot, axis=0, keepdims=True)
            ot = ot * (lax.rsqrt(ms + LN_EPS) * (1.0 - lam_init))
            o_ref[0, pl.ds(q0, ATT_QT), hrows[hh]] = (ot.T * g_ref[...]).astype(o_ref.dtype)
        return carry

    lax.fori_loop(0, ntiles, finish, 0)


def _diff_attention_flat(qt, k, vt, lamv, bias, subln_g, lam_init):
    b, s, _ = k.shape
    nh = ATT_HEADS_PER_STEP
    hw = 2 * DA_HD
    nunits, tq, tk, tt = _attn_unit_tables(s)
    score = pltpu.VMEM((nh, ATT_KB, 2 * ATT_QT), F32)
    smax = pltpu.VMEM((nh, 1, 2 * ATT_QT), F32)
    grid_spec = pltpu.PrefetchScalarGridSpec(
        num_scalar_prefetch=3,
        grid=(b, DA_HEADS // nh),
        in_specs=[
            pl.BlockSpec((4, DA_HD), lambda bi, hi, *_: (0, 0)),
            pl.BlockSpec((1, nh * hw, s), lambda bi, hi, *_: (bi, hi, 0)),
            pl.BlockSpec((1, s, nh * hw), lambda bi, hi, *_: (bi, 0, hi)),
            pl.BlockSpec((1, nh * hw, s), lambda bi, hi, *_: (bi, hi, 0)),
            pl.BlockSpec((nh, ATT_BIAS_TYPES, ATT_KB, ATT_QT), lambda bi, hi, *_: (hi, 0, 0, 0)),
            pl.BlockSpec((1, hw), lambda bi, hi, *_: (0, 0)),
        ],
        out_specs=pl.BlockSpec((1, s, nh * hw), lambda bi, hi, *_: (bi, 0, hi)),
        scratch_shapes=[pltpu.VMEM((s // ATT_QT, nh, 1, 2 * ATT_QT), F32),
                        pltpu.VMEM((s // ATT_QT, nh, hw + ATT_ONES_ROWS, 2 * ATT_QT), F32),
                        score, score, score, smax, smax, smax],
    )
    return pl.pallas_call(
        functools.partial(_attn_flat_kernel, lam_init=lam_init, nunits=nunits),
        grid_spec=grid_spec,
        out_shape=jax.ShapeDtypeStruct((b, s, DA_HEADS * hw), BF16),
        compiler_params=_cparams(("parallel", "parallel")),
        name="diff_attention",
    )(tq, tk, tt, lamv, qt, k, vt, bias, subln_g)


def _dot(a, b, dims=(((1,), (0,)), ((), ()))):
    return lax.dot_general(a.astype(BF16), b.astype(BF16), dims, preferred_element_type=F32)


def _dot_split_lhs(a, b_bf16):
    ah, al = _split2(a)
    return (jnp.dot(ah, b_bf16, preferred_element_type=F32) + jnp.dot(al, b_bf16, preferred_element_type=F32))


def _pair_consts():
    c, n = RW_CHUNK, RW_HD
    lane = lax.broadcasted_iota(jnp.int32, (c, 2 * n), 1)
    row = lax.broadcasted_iota(jnp.int32, (2 * c, 2 * c), 0)
    col = lax.broadcasted_iota(jnp.int32, (2 * c, 2 * c), 1)
    same_head = (row // n) == (col // n)
    return lane < n, same_head.astype(BF16)


def _expand(x2, head_a):
    zero = jnp.zeros_like(x2)
    return jnp.concatenate([jnp.where(head_a, x2, zero), jnp.where(head_a, zero, x2)], axis=0)


def _compact(xm):
    return xm[0:RW_CHUNK] + xm[RW_CHUNK:2 * RW_CHUNK]


def _rw_chunk_kernel(r_ref, k_ref, v_ref, lo_ref, mu_ref, mulo_ref, w0_ref, w2h_ref, w2l_ref,
                     a0_ref, a2h_ref, a2l_ref, g2_ref, kk_ref, ka_ref, rk_ref,
                     y_ref, bon_ref, g_ref,
                     z_sc, pt_sc, qt_sc, rh_sc, y0_sc, prev_sc, *, nchunks):
    c, n = RW_CHUNK, RW_HD
    pw = 2 * n
    pairs = range(z_sc.shape[0])
    lns = [slice(hp * pw, (hp + 1) * pw) for hp in pairs]
    head_a, seg_ones = _pair_consts()

    @pl.when(pl.program_id(1) == 0)
    def _():
        z_sc[...] = jnp.zeros(z_sc.shape, F32)
        pt_sc[...] = jnp.zeros(pt_sc.shape, F32)
        qt_sc[...] = jnp.zeros(qt_sc.shape, F32)
        rh_sc[...] = jnp.zeros(rh_sc.shape, rh_sc.dtype)
        y0_sc[...] = jnp.zeros(y0_sc.shape, F32)
        prev_sc[...] = jnp.zeros(prev_sc.shape, F32)

    z = [z_sc[hp] for hp in pairs]
    for hp in pairs:
        y_ref[0, :, lns[hp]] = _dot(rh_sc[:, lns[hp]], z[hp]) + y0_sc[:, lns[hp]]
    z_new = [_dot3(_expand(pt_sc[:, lns[hp]], head_a), z[hp]) + _expand(qt_sc[:, lns[hp]], head_a)
             for hp in pairs]
    for hp in pairs:
        z_sc[hp] = z_new[hp]

    row = lax.broadcasted_iota(jnp.int32, (2 * c, 2 * c), 0)
    col = lax.broadcasted_iota(jnp.int32, (2 * c, 2 * c), 1)
    strict = (row % c) > (col % c)
    incl = (row % c) >= (col % c)
    trow = lax.broadcasted_iota(jnp.int32, (c, c), 0)
    tcol = lax.broadcasted_iota(jnp.int32, (c, c), 1)
    tri_ones = (trow >= tcol).astype(BF16)
    diag2 = lax.broadcasted_iota(jnp.int32, (c, pw), 0) == (lax.broadcasted_iota(jnp.int32, (c, pw), 1) % n)
    d = functools.partial(jnp.dot, preferred_element_type=F32)

    def cumsum(lw2):
        l1 = lw2.astype(BF16)
        rem = lw2 - l1.astype(F32)
        l2 = rem.astype(BF16)
        l3 = (rem - l2.astype(F32)).astype(BF16)
        return d(tri_ones, l1) + (d(tri_ones, l2) + d(tri_ones, l3))

    ci = pl.program_id(1)
    trow1 = lax.broadcasted_iota(jnp.int32, (c, 1), 0)
    keep_prev = ci < nchunks - 1

    def shifted(cur_ref, slot, mu):
        p = cur_ref[0]
        w = p.shape[1]
        before = prev_sc[slot:slot + 1, 0:w]
        pm1 = jnp.where(trow1 == 0, before, pltpu.roll(p, 1, 0))
        prev_sc[slot:slot + 1, 0:w] = jnp.where(keep_prev, p[c - 1:c, :], before)
        return p + mu * (pm1 - p)

    r = shifted(r_ref, 0, mu_ref[0:1])
    k = shifted(k_ref, 1, mu_ref[1:2])
    v = shifted(v_ref, 2, mu_ref[2:3])
    lo = shifted(lo_ref, 3, mulo_ref[...])

    def dot3w(a, wh_ref, wl_ref):
        ah, al = _split2(a)
        wh = wh_ref[...]
        return d(ah, wh) + (d(ah, wl_ref[...]) + d(al, wh))

    nz = -(w0_ref[...] + dot3w(jnp.tanh(lo[:, 0:128]), w2h_ref, w2l_ref))
    softplus = jnp.maximum(nz, 0.0) + jnp.log(1.0 + jnp.exp(-jnp.abs(nz)))
    lw = -jnp.exp(-softplus - 0.5)
    a_ic = jax.nn.sigmoid(a0_ref[...] + dot3w(lo[:, 128:256], a2h_ref, a2l_ref))
    g_ref[0] = d(jax.nn.sigmoid(lo[:, 256:512]).astype(BF16), g2_ref[...])
    kk = k * kk_ref[...]
    km = k * (1.0 + (a_ic - 1.0) * ka_ref[...])

    cum = [cumsum(lw[:, ln]) for ln in lns]
    nrm = [jnp.sqrt(_dot_split_lhs(kk[:, ln] * kk[:, ln], seg_ones)) for ln in lns]
    big, atm, vm, bdm, kdm = [], [], [], [], []
    for hp, ln in enumerate(lns):
        lw2, cm = lw[:, ln], cum[hp]
        g_inv = jnp.exp(-cm)
        g_rest = jnp.exp(cm[c - 1:c, :] - cm)
        kkn = kk[:, ln] / jnp.maximum(nrm[hp], 1e-12)
        b2 = kkn * a_ic[:, ln]
        km2 = km[:, ln]
        a_m = _expand(-kkn * jnp.exp(cm - lw2), head_a)
        r_m = _expand(r[:, ln] * jnp.exp(cm), head_a)
        b_m = _expand(b2 * g_inv, head_a)
        k_m = _expand(km2 * g_inv, head_a)
        atm.append(a_m)
        bdm.append(_expand(b2 * g_rest, head_a))
        kdm.append(_expand(km2 * g_rest, head_a))
        vm.append(_expand(v[:, ln], head_a))
        big.append(_dot(jnp.concatenate([a_m, r_m], axis=0), jnp.concatenate([b_m, k_m], axis=0), NT_DIMS))
    lmat = [jnp.where(strict, bg[0:pw, 0:pw], 0.0) for bg in big]
    m_rb = [jnp.where(incl, bg[pw:2 * pw, 0:pw], 0.0) for bg in big]
    mv = [_dot(jnp.concatenate([jnp.where(strict, bg[0:pw, pw:2 * pw], 0.0),
                                jnp.where(incl, bg[pw:2 * pw, pw:2 * pw], 0.0)], axis=0), vm[hp])
          for hp, bg in enumerate(big)]
    x = [jnp.concatenate([atm[hp], mv[hp][0:pw]], axis=1) for hp in pairs]
    lp = lmat
    for it in range(6):
        x = [x[hp] + _dot(lp[hp], x[hp]) for hp in pairs]
        if it < 5:
            lp = [_dot(m, m) for m in lp]
    rb = [_dot(m_rb[hp], x[hp]) for hp in pairs]
    bx = [_dot(bdm[hp], x[hp], TN_DIMS) for hp in pairs]
    kv = [_dot(kdm[hp], vm[hp], TN_DIMS) for hp in pairs]
    for hp, ln in enumerate(lns):
        g_last = jnp.exp(cum[hp][c - 1:c, :])
        pt_sc[:, ln] = _compact(bx[hp][:, 0:pw]) + jnp.where(diag2, g_last, 0.0)
        qt_sc[:, ln] = _compact(bx[hp][:, pw:2 * pw] + kv[hp])
        rh_sc[:, ln] = (r[:, ln] * jnp.exp(cum[hp]) + _compact(rb[hp][:, 0:pw])).astype(rh_sc.dtype)
        y0_sc[:, ln] = _compact(rb[hp][:, pw:2 * pw] + mv[hp][pw:2 * pw])
        bon_ref[0, :, ln] = _dot_split_lhs(r[:, ln] * km[:, ln] * rk_ref[:, ln], seg_ones) * v[:, ln]


def _rw_chunks(rest, mu_rkv, mu_lo, w0, w2h, w2l, a0, a2h, a2l, g2, k_k, k_a, r_k):
    b, s, _ = rest.shape
    d = D_MODEL
    c = RW_CHUNK
    nc = s // c
    lora_w = mu_lo.shape[1]
    cur = lambda col, w: pl.BlockSpec((1, c, w), lambda bi, ci: (bi, jnp.minimum(ci, nc - 1), col))
    prev = pl.BlockSpec((1, c, d), lambda bi, ci: (bi, jnp.maximum(ci - 1, 0), 0))
    full = lambda a: pl.BlockSpec(a.shape, lambda bi, ci: (0,) * a.ndim)
    params = (mu_rkv, mu_lo, w0, w2h, w2l, a0, a2h, a2l, g2, k_k, k_a, r_k)
    out = jax.ShapeDtypeStruct((b, s, d), F32)
    lora_col = (3 * d + 2 * d) // lora_w
    return pl.pallas_call(
        functools.partial(_rw_chunk_kernel, nchunks=nc),
        grid=(b, nc + 1),
        in_specs=[cur(0, d), cur(1, d), cur(2, d), cur(lora_col, lora_w)] + [full(a) for a in params],
        out_specs=[prev, cur(0, d), cur(0, d)],
        out_shape=[out, out, out],
        scratch_shapes=[pltpu.VMEM((d // (2 * RW_HD), 2 * RW_HD, 2 * RW_HD), F32),
                        pltpu.VMEM((c, d), F32), pltpu.VMEM((c, d), F32),
                        pltpu.VMEM((c, d), BF16), pltpu.VMEM((c, d), F32),
                        pltpu.VMEM((4, d), F32)],
        compiler_params=_cparams(("parallel", "arbitrary")),
        name="rwkv_chunks",
    )(rest, rest, rest, rest, *params)


def _mix_kernel(oda_ref, yrw_ref, bon_ref, gout_ref, ga_ref, gb_ref, x_ref, wda_ref, wrw_ref, wmix_ref,
                gng_ref, gnb_ref, lng_ref, lnb_ref, o_ref, *, alpha):
    y_da = jnp.dot(oda_ref[...], wda_ref[...], preferred_element_type=F32)
    _, seg_ones = _pair_consts()
    parts = []
    for hp in range(RW_HEADS // 2):
        ln = slice(hp * 2 * RW_HD, (hp + 1) * 2 * RW_HD)
        y = yrw_ref[:, ln]
        mu = _dot_split_lhs(y, seg_ones) * (1.0 / RW_HD)
        yc = y - mu
        var = _dot_split_lhs(yc * yc, seg_ones) * (1.0 / RW_HD)
        parts.append(yc * lax.rsqrt(var + RW_LNX_EPS))
    yn = jnp.concatenate(parts, axis=1) * gng_ref[...] + gnb_ref[...] + bon_ref[...]
    y_rw = jnp.dot((yn * gout_ref[...]).astype(BF16), wrw_ref[...], preferred_element_type=F32)
    mixed = jax.nn.sigmoid(ga_ref[...]) * y_da + jax.nn.sigmoid(gb_ref[...]) * y_rw
    z = alpha * x_ref[...] + jnp.dot(mixed.astype(BF16), wmix_ref[...], preferred_element_type=F32)
    o_ref[...] = _layer_norm_rows(z, lng_ref[...], lnb_ref[...])


def _mix(o_da, y_rw, bonus, g_out, rest, x, w_da, w_rw, w_mix, gn_g, gn_b, ln_g, ln_b, alpha, tm=512):
    t, d = x.shape
    rowb = lambda c: pl.BlockSpec((tm, d), lambda i: (i, c))
    full = lambda a: pl.BlockSpec(a.shape, lambda i: (0,) * a.ndim)
    return pl.pallas_call(
        functools.partial(_mix_kernel, alpha=alpha),
        grid=(t // tm,),
        in_specs=[rowb(0), rowb(0), rowb(0), rowb(0), rowb(3), rowb(4), rowb(0),
                  full(w_da), full(w_rw), full(w_mix), full(gn_g), full(gn_b), full(ln_g), full(ln_b)],
        out_specs=rowb(0),
        out_shape=jax.ShapeDtypeStruct((t, d), F32),
        compiler_params=_cparams(("parallel",)),
        name="mix_merge_ln1",
    )(o_da, y_rw, bonus, g_out, rest, rest, x, w_da, w_rw, w_mix, gn_g, gn_b, ln_g, ln_b)


def _mem_kv_kernel(mem_ref, g_ref, b_ref, w_ref, ck_ref, cv_ref):
    m = _layer_norm_rows(mem_ref[0], g_ref[...], b_ref[...])
    kv = jnp.dot(m.astype(BF16), w_ref[...], preferred_element_type=F32)
    ck_ref[0] = kv[:, :D_MODEL].astype(BF16)
    cv_ref[0] = kv[:, D_MODEL:].astype(BF16)


def _mem_kv(mem, g, b, w_ckv):
    bsz, m, d = mem.shape
    full = lambda a: pl.BlockSpec(a.shape, lambda i: (0,) * a.ndim)
    blk = pl.BlockSpec((1, m, d), lambda i: (i, 0, 0))
    return pl.pallas_call(
        _mem_kv_kernel,
        grid=(bsz,),
        in_specs=[blk, full(g), full(b), full(w_ckv)],
        out_specs=[blk, blk],
        out_shape=[jax.ShapeDtypeStruct((bsz, m, d), BF16)] * 2,
        compiler_params=_cparams(("parallel",)),
        name="mem_kv",
    )(mem, g, b, w_ckv)


def _cross_kernel(x_ref, ck_ref, cv_ref, wq_ref, wo_ref, lng_ref, lnb_ref, o_ref, *, alpha):
    x = x_ref[...]
    cq = jnp.dot(x.astype(BF16), wq_ref[...], preferred_element_type=F32)
    cq = (cq * (CA_HD ** -0.5)).astype(BF16)
    outs = []
    for h in range(CA_HEADS):
        sl = slice(h * CA_HD, (h + 1) * CA_HD)
        s = lax.dot_general(cq[:, sl], ck_ref[0, :, sl], NT_DIMS, preferred_element_type=F32)
        s = s - jnp.max(s, axis=-1, keepdims=True)
        p = jnp.exp(s)
        p = p / jnp.sum(p, axis=-1, keepdims=True)
        outs.append(jnp.dot(p.astype(BF16), cv_ref[0, :, sl], preferred_element_type=F32))
    co = jnp.dot(jnp.concatenate(outs, axis=1).astype(BF16), wo_ref[...], preferred_element_type=F32)
    o_ref[...] = _layer_norm_rows(alpha * x + co, lng_ref[...], lnb_ref[...])


def _cross_attention(x, ck, cv, w_cq, w_co, ln_g, ln_b, seq, alpha, tm=512):
    t, d = x.shape
    tiles_per_seq = seq // tm
    full = lambda a: pl.BlockSpec(a.shape, lambda i: (0,) * a.ndim)
    rowb = pl.BlockSpec((tm, d), lambda i: (i, 0))
    memb = pl.BlockSpec((1,) + ck.shape[1:], lambda i: (i // tiles_per_seq, 0, 0))
    return pl.pallas_call(
        functools.partial(_cross_kernel, alpha=alpha),
        grid=(t // tm,),
        in_specs=[rowb, memb, memb, full(w_cq), full(w_co), full(ln_g), full(ln_b)],
        out_specs=rowb,
        out_shape=jax.ShapeDtypeStruct((t, d), F32),
        compiler_params=_cparams(("parallel",)),
        name="cross_attention_ln2",
    )(x, ck, cv, w_cq, w_co, ln_g, ln_b)


def _ffn_kernel(x_ref, wg_ref, wu_ref, wo_ref, lng_ref, lnb_ref, o_ref, xb_ref, acc_ref, *, alpha):
    j = pl.program_id(1)

    @pl.when(j == 0)
    def _():
        xb_ref[...] = x_ref[...].astype(BF16)
        acc_ref[...] = jnp.zeros(acc_ref.shape, F32)

    xb = xb_ref[...]
    gate = jnp.dot(xb, wg_ref[...], preferred_element_type=F32)
    up = jnp.dot(xb, wu_ref[...], preferred_element_type=F32)
    h = (gate * jax.nn.sigmoid(gate) * up).astype(BF16)
    acc_ref[...] += jnp.dot(h, wo_ref[...], preferred_element_type=F32)

    @pl.when(j == pl.num_programs(1) - 1)
    def _():
        o_ref[...] = _layer_norm_rows(alpha * x_ref[...] + acc_ref[...], lng_ref[...], lnb_ref[...])


def _ffn(x, w_in, w_out, ln_g, ln_b, alpha, tm=512, tf=1408):
    t, d = x.shape
    nf = D_FF // tf
    full = lambda a: pl.BlockSpec(a.shape, lambda i, j: (0,) * a.ndim)
    rowb = pl.BlockSpec((tm, d), lambda i, j: (i, 0))
    return pl.pallas_call(
        functools.partial(_ffn_kernel, alpha=alpha),
        grid=(t // tm, nf),
        in_specs=[rowb,
                  pl.BlockSpec((d, tf), lambda i, j: (0, j)),
                  pl.BlockSpec((d, tf), lambda i, j: (0, nf + j)),
                  pl.BlockSpec((tf, d), lambda i, j: (j, 0)),
                  full(ln_g), full(ln_b)],
        out_specs=rowb,
        out_shape=jax.ShapeDtypeStruct((t, d), F32),
        scratch_shapes=[pltpu.VMEM((tm, d), BF16), pltpu.VMEM((tm, d), F32)],
        compiler_params=_cparams(("parallel", "arbitrary")),
        name="swiglu_ffn_ln3",
    )(x, w_in, w_in, w_out, ln_g, ln_b)


def _pad_cols(a, width):
    return jnp.pad(a, ((0, 0), (0, width - a.shape[1])))


def _pad_rows(a, height):
    return jnp.pad(a, ((0, height - a.shape[0]), (0, 0)))


def kernel(x, mem, rel_bias, w_in, shift_mu, lambda_q1, lambda_k1, lambda_q2, lambda_k2, da_subln_g, w_da_proj, rw_w0, rw_w2, rw_a0, rw_a2, rw_g2, rw_k_k, rw_k_a, rw_r_k, rw_lnx_g, rw_lnx_b, w_rw_proj, w_mix_out, ln1_g, ln1_b, mem_ln_g, mem_ln_b, w_cq, w_ckv, w_co, ln2_g, ln2_b, w_ffn_in, w_ffn_out, ln3_g, ln3_b):
    bsz, seq, d = x.shape
    depth = w_in.shape[0]
    alpha = (2.0 * depth) ** 0.25
    t = bsz * seq
    row = lambda a: a.reshape(1, -1).astype(F32)
    bias_tiles = _attn_flat_bias(rel_bias)
    xf = x.reshape(t, d)

    qkv_w = 3 * D_MODEL
    rw0 = qkv_w
    lo0 = rw0 + 3 * D_MODEL
    gate0 = lo0 + DECAY_LORA + AAA_LORA + GATE_LORA

    for l in range(depth):
        w = w_in[l]
        w_qt = w[:, :D_MODEL].T.astype(BF16)
        w_k = w[:, D_MODEL:2 * D_MODEL].astype(BF16)
        w_vt = w[:, 2 * D_MODEL:qkv_w].T.astype(BF16)
        w_rest = jnp.concatenate([
            w[:, rw0:lo0], w[:, gate0:gate0 + 2 * D_MODEL],
            _pad_cols(w[:, lo0:lo0 + DECAY_LORA], 128),
            _pad_cols(w[:, lo0 + DECAY_LORA:lo0 + DECAY_LORA + AAA_LORA], 128),
            _pad_cols(w[:, lo0 + DECAY_LORA + AAA_LORA:gate0], 256)], axis=1).astype(BF16)
        mu = shift_mu[l].astype(F32)
        mu_rkv = mu[:3 * D_MODEL].reshape(3, D_MODEL)
        o1 = 3 * D_MODEL
        mu_lo = jnp.concatenate([
            _pad_cols(mu[None, o1:o1 + DECAY_LORA], 128),
            _pad_cols(mu[None, o1 + DECAY_LORA:o1 + DECAY_LORA + AAA_LORA], 128),
            _pad_cols(mu[None, o1 + DECAY_LORA + AAA_LORA:], 256)], axis=1)
        w2 = _pad_rows(rw_w2[l].astype(F32), 128)
        a2 = _pad_rows(rw_a2[l].astype(F32), 128)
        w2h, w2l = _split2(w2)
        a2h, a2l = _split2(a2)
        g2 = _pad_rows(rw_g2[l], 256).astype(BF16)

        qt, kda, vt = _qkv_proj(xf, w_qt, w_k, w_vt, bsz, seq)
        rest = _matmul(xf, w_rest, F32, 1024, 1408, "in_proj_rest")

        lam_init = 0.8 - 0.6 * math.exp(-0.3 * l)
        lamv = jnp.stack([lambda_q1[l], lambda_k1[l], lambda_q2[l], lambda_k2[l]]).astype(F32)
        o_da = _diff_attention_flat(qt, kda.reshape(bsz, seq, d), vt, lamv, bias_tiles, row(da_subln_g[l]),
                                    lam_init)

        y_rw, bon, g_out = _rw_chunks(
            rest.reshape(bsz, seq, rest.shape[1]), mu_rkv, mu_lo, row(rw_w0[l]), w2h, w2l, row(rw_a0[l]),
            a2h, a2l, g2, row(rw_k_k[l]), row(rw_k_a[l]), row(rw_r_k[l]))

        x1 = _mix(o_da.reshape(t, d), y_rw.reshape(t, d), bon.reshape(t, d), g_out.reshape(t, d), rest, xf,
                  w_da_proj[l].astype(BF16), w_rw_proj[l].astype(BF16), w_mix_out[l].astype(BF16),
                  row(rw_lnx_g[l]), row(rw_lnx_b[l]), row(ln1_g[l]), row(ln1_b[l]), alpha)

        ck, cv = _mem_kv(mem, row(mem_ln_g[l]), row(mem_ln_b[l]), w_ckv[l].astype(BF16))
        x2 = _cross_attention(x1, ck, cv, w_cq[l].astype(BF16), w_co[l].astype(BF16),
                              row(ln2_g[l]), row(ln2_b[l]), seq, alpha)

        xf = _ffn(x2, w_ffn_in[l].astype(BF16), w_ffn_out[l].astype(BF16),
                  row(ln3_g[l]), row(ln3_b[l]), alpha)
    return xf.reshape(bsz, seq, d)
```

```python
import functools
import math

import jax
import jax.numpy as jnp
from jax import lax
from jax.experimental import pallas as pl
from jax.experimental.pallas import tpu as pltpu

F32 = jnp.float32
BF16 = jnp.bfloat16

D_MODEL = 1024
DA_HEADS = 8
DA_HD = 64
NUM_BUCKETS = 32
MAX_DISTANCE = 128
RW_HD = 64
RW_HEADS = D_MODEL // RW_HD
DECAY_LORA = 64
AAA_LORA = 64
GATE_LORA = 160
RW_LNX_EPS = 64e-5
MEM_TOKENS = 256
CA_HEADS = 4
CA_HD = D_MODEL // CA_HEADS
D_FF = 2816
LN_EPS = 1e-5
NEG_BIG = -1e30

LOG2E = math.log2(math.e)
ATT_HEADS_PER_STEP = 2
ATT_QT = 512
ATT_KB = 256
ATT_BIAS_TYPES = 4
ATT_ROTATE = 3
ATT_ONES_ROWS = 16
RW_CHUNK = 64
VMEM_LIMIT = 56 * 1024 * 1024

NT_DIMS = (((1,), (1,)), ((), ()))
TN_DIMS = (((0,), (0,)), ((), ()))


def _cparams(sem, vmem=VMEM_LIMIT):
    return pltpu.CompilerParams(dimension_semantics=sem, vmem_limit_bytes=vmem)


def _split2(a):
    hi = a.astype(BF16)
    lo = (a - hi.astype(F32)).astype(BF16)
    return hi, lo


def _dot3(a, b, dims=(((1,), (0,)), ((), ()))):
    ah, al = _split2(a)
    bh, bl = _split2(b)
    d = functools.partial(lax.dot_general, dimension_numbers=dims, preferred_element_type=F32)
    return d(ah, bh) + (d(ah, bl) + d(al, bh))


def _layer_norm_rows(z, g, b):
    mu = jnp.mean(z, axis=-1, keepdims=True)
    zc = z - mu
    var = jnp.mean(zc * zc, axis=-1, keepdims=True)
    return zc * lax.rsqrt(var + LN_EPS) * g + b


def _mm_kernel(x_ref, w_ref, o_ref, xb_ref):
    @pl.when(pl.program_id(1) == 0)
    def _():
        xb_ref[...] = x_ref[...].astype(BF16)

    o_ref[...] = jnp.dot(xb_ref[...], w_ref[...], preferred_element_type=F32).astype(o_ref.dtype)


def _matmul(x, w, out_dtype, tm, tn, name):
    t, k = x.shape
    n = w.shape[1]
    return pl.pallas_call(
        _mm_kernel,
        grid=(t // tm, n // tn),
        in_specs=[pl.BlockSpec((tm, k), lambda i, j: (i, 0)),
                  pl.BlockSpec((k, tn), lambda i, j: (0, j))],
        out_specs=pl.BlockSpec((tm, tn), lambda i, j: (i, j)),
        out_shape=jax.ShapeDtypeStruct((t, n), out_dtype),
        scratch_shapes=[pltpu.VMEM((tm, k), BF16)],
        compiler_params=_cparams(("parallel", "arbitrary")),
        name=name,
    )(x, w)


def _qkv_kernel(x_ref, wqt_ref, wk_ref, wvt_ref, qt_ref, k_ref, vt_ref):
    xb = x_ref[...].astype(BF16)
    k_ref[...] = jnp.dot(xb, wk_ref[...], preferred_element_type=F32).astype(BF16)
    qt = lax.dot_general(wqt_ref[...], xb, NT_DIMS, preferred_element_type=F32)
    qt_ref[0] = (qt * (DA_HD ** -0.5 * LOG2E)).astype(BF16)
    vt_ref[0] = lax.dot_general(wvt_ref[...], xb, NT_DIMS, preferred_element_type=F32).astype(BF16)


def _qkv_proj(x, wqt, wk, wvt, bsz, seq, tm=512):
    t, d = x.shape
    tiles_per_seq = seq // tm
    full = lambda a: pl.BlockSpec(a.shape, lambda i: (0,) * a.ndim)
    tblk = pl.BlockSpec((1, d, tm), lambda i: (i // tiles_per_seq, 0, i % tiles_per_seq))
    rowb = pl.BlockSpec((tm, d), lambda i: (i, 0))
    tshape = jax.ShapeDtypeStruct((bsz, d, seq), BF16)
    return pl.pallas_call(
        _qkv_kernel,
        grid=(t // tm,),
        in_specs=[rowb, full(wqt), full(wk), full(wvt)],
        out_specs=[tblk, rowb, tblk],
        out_shape=[tshape, jax.ShapeDtypeStruct((t, d), BF16), tshape],
        compiler_params=_cparams(("parallel",)),
        name="in_proj_qkv",
    )(x, wqt, wk, wvt)


def _t5_bucket(rel):
    n = jnp.maximum(rel, 0)
    max_exact = NUM_BUCKETS // 2
    nf = jnp.maximum(n, 1).astype(jnp.float32)
    large = max_exact + jnp.floor(jnp.log(nf / max_exact) / math.log(MAX_DISTANCE / max_exact)
                                  * (NUM_BUCKETS - max_exact)).astype(jnp.int32)
    large = jnp.minimum(large, NUM_BUCKETS - 1)
    return jnp.where(n < max_exact, n, large)


def _col_max(s):
    while s.shape[0] > 8:
        half = s.shape[0] // 2
        s = jnp.maximum(s[:half], s[half:])
    return jnp.max(s, axis=0, keepdims=True)


def _attn_unit_tables(seq):
    kb_per_qt = ATT_QT // ATT_KB
    qts, kbs, types = [], [], []
    for qt in range(seq // ATT_QT):
        nkb = (qt + 1) * kb_per_qt
        for kb in range(nkb):
            from_end = nkb - 1 - kb
            qts.append(qt)
            kbs.append(kb)
            types.append(ATT_BIAS_TYPES - 1 - from_end if from_end < ATT_BIAS_TYPES - 1 else 0)
    assert len(qts) % ATT_ROTATE == 0
    nunits = len(qts)
    for _ in range(ATT_ROTATE - 1):
        qts.append(0)
        kbs.append(0)
        types.append(0)
    as_i32 = lambda v: jnp.asarray(v, jnp.int32)
    return nunits, as_i32(qts), as_i32(kbs), as_i32(types)


def _attn_flat_bias(rel_bias):
    tab = rel_bias.astype(F32) - rel_bias[NUM_BUCKETS - 1].astype(F32)[None, :]
    base = jnp.arange(ATT_QT)[None, :] - jnp.arange(ATT_KB)[:, None]

    def tile(key_offset):
        rel = base - key_offset
        onehot = (_t5_bucket(rel)[:, :, None] == jnp.arange(NUM_BUCKETS)[None, None, :]).astype(F32)
        vals = jnp.einsum('kqb,bh->hkq', onehot, tab, precision=lax.Precision.HIGHEST)
        return jnp.where((rel >= 0)[None], vals, NEG_BIG)

    zeros = jnp.zeros((rel_bias.shape[1], ATT_KB, ATT_QT), F32)
    return jnp.stack([zeros, tile(-ATT_KB), tile(0), tile(ATT_KB)], axis=1) * LOG2E


def _attn_flat_kernel(tq_ref, tk_ref, tt_ref, lamv_ref, qt_ref, k_ref, vt_ref, bias_ref, g_ref, o_ref,
                      m_sc, acc_sc, sa_sc, sb_sc, sc_sc, ma_sc, mb_sc, mc_sc, *, lam_init, nunits):
    hw = 2 * DA_HD
    heads = range(ATT_HEADS_PER_STEP)
    hrows = [slice(hh * hw, (hh + 1) * hw) for hh in heads]
    ntiles = m_sc.shape[0]

    lv = lamv_ref[...]
    lam = (jnp.exp(jnp.sum(lv[0:1] * lv[1:2], axis=-1, keepdims=True))
           - jnp.exp(jnp.sum(lv[2:3] * lv[3:4], axis=-1, keepdims=True)) + lam_init)

    m_sc[...] = jnp.full(m_sc.shape, NEG_BIG, F32)
    acc_sc[...] = jnp.zeros(acc_sc.shape, F32)
    rowi = lax.broadcasted_iota(jnp.int32, (hw, ATT_QT), 0)

    def fill(buf, mbuf, u):
        q0 = pl.multiple_of(tq_ref[u] * ATT_QT, ATT_QT)
        k0 = pl.multiple_of(tk_ref[u] * ATT_KB, ATT_KB)
        ty = tt_ref[u]
        s = []
        for hh in heads:
            q = qt_ref[0, hrows[hh], pl.ds(q0, ATT_QT)]
            zero = jnp.zeros_like(q)
            qq = jnp.concatenate([jnp.where(rowi < DA_HD, q, zero), jnp.where(rowi >= DA_HD, q, zero)], axis=1)
            s.append(jnp.dot(k_ref[0, pl.ds(k0, ATT_KB), hrows[hh]], qq, preferred_element_type=F32))
        for hh in heads:
            b = bias_ref[hh, ty]
            sb = s[hh] + jnp.concatenate([b, b], axis=1)
            buf[hh] = sb
            mbuf[hh] = _col_max(sb)

    def drain(buf, mbuf, u):
        t = tq_ref[u]
        k0 = pl.multiple_of(tk_ref[u] * ATT_KB, ATT_KB)
        m_prev = [m_sc[t, hh] for hh in heads]
        m_new = [jnp.maximum(m_prev[hh], mbuf[hh]) for hh in heads]
        alpha = [jnp.exp2(m_prev[hh] - m_new[hh]) for hh in heads]
        p = [jnp.exp2((buf[hh] - m_new[hh]).astype(BF16)) for hh in heads]
        ones = jnp.ones((ATT_ONES_ROWS, ATT_KB), BF16)
        pv = [jnp.dot(jnp.concatenate([vt_ref[0, hrows[hh], pl.ds(k0, ATT_KB)], ones], axis=0), p[hh],
                      preferred_element_type=F32) for hh in heads]
        for hh in heads:
            acc_sc[t, hh] = alpha[hh] * acc_sc[t, hh] + pv[hh]
            m_sc[t, hh] = m_new[hh]

    fill(sa_sc, ma_sc, 0)
    fill(sb_sc, mb_sc, 1)

    def trip(i, carry):
        u = i * ATT_ROTATE
        fill(sc_sc, mc_sc, u + 2)
        drain(sa_sc, ma_sc, u)
        fill(sa_sc, ma_sc, u + 3)
        drain(sb_sc, mb_sc, u + 1)
        fill(sb_sc, mb_sc, u + 4)
        drain(sc_sc, mc_sc, u + 2)
        return carry

    lax.fori_loop(0, nunits // ATT_ROTATE, trip, 0)

    def finish(t, carry):
        q0 = pl.multiple_of(t * ATT_QT, ATT_QT)
        for hh in heads:
            a = acc_sc[t, hh]
            acc = a[0:hw, :] * (1.0 / a[hw:hw + 1, :])
            ot = acc[:, 0:ATT_QT] - lam * acc[:, ATT_QT:2 * ATT_QT]
            ms = jnp.mean(ot * ot, axis=0, keepdims=True)
            ot = ot * (lax.rsqrt(ms + LN_EPS) * (1.0 - lam_init))
            o_ref[0, pl.ds(q0, ATT_QT), hrows[hh]] = (ot.T * g_ref[...]).astype(o_ref.dtype)
        return carry

    lax.fori_loop(0, ntiles, finish, 0)


def _diff_attention_flat(qt, k, vt, lamv, bias, subln_g, lam_init):
    b, s, _ = k.shape
    nh = ATT_HEADS_PER_STEP
    hw = 2 * DA_HD
    nunits, tq, tk, tt = _attn_unit_tables(s)
    score = pltpu.VMEM((nh, ATT_KB, 2 * ATT_QT), F32)
    smax = pltpu.VMEM((nh, 1, 2 * ATT_QT), F32)
    grid_spec = pltpu.PrefetchScalarGridSpec(
        num_scalar_prefetch=3,
        grid=(b, DA_HEADS // nh),
        in_specs=[
            pl.BlockSpec((4, DA_HD), lambda bi, hi, *_: (0, 0)),
            pl.BlockSpec((1, nh * hw, s), lambda bi, hi, *_: (bi, hi, 0)),
            pl.BlockSpec((1, s, nh * hw), lambda bi, hi, *_: (bi, 0, hi)),
            pl.BlockSpec((1, nh * hw, s), lambda bi, hi, *_: (bi, hi, 0)),
            pl.BlockSpec((nh, ATT_BIAS_TYPES, ATT_KB, ATT_QT), lambda bi, hi, *_: (hi, 0, 0, 0)),
            pl.BlockSpec((1, hw), lambda bi, hi, *_: (0, 0)),
        ],
        out_specs=pl.BlockSpec((1, s, nh * hw), lambda bi, hi, *_: (bi, 0, hi)),
        scratch_shapes=[pltpu.VMEM((s // ATT_QT, nh, 1, 2 * ATT_QT), F32),
                        pltpu.VMEM((s // ATT_QT, nh, hw + ATT_ONES_ROWS, 2 * ATT_QT), F32),
                        score, score, score, smax, smax, smax],
    )
    return pl.pallas_call(
        functools.partial(_attn_flat_kernel, lam_init=lam_init, nunits=nunits),
        grid_spec=grid_spec,
        out_shape=jax.ShapeDtypeStruct((b, s, DA_HEADS * hw), BF16),
        compiler_params=_cparams(("parallel", "parallel")),
        name="diff_attention",
    )(tq, tk, tt, lamv, qt, k, vt, bias, subln_g)


def _dot(a, b, dims=(((1,), (0,)), ((), ()))):
    return lax.dot_general(a.astype(BF16), b.astype(BF16), dims, preferred_element_type=F32)


def _dot_split_lhs(a, b_bf16):
    ah, al = _split2(a)
    return (jnp.dot(ah, b_bf16, preferred_element_type=F32) + jnp.dot(al, b_bf16, preferred_element_type=F32))


def _dot_split_lhs_rhs(a_bf16, b):
    bh, bl = _split2(b)
    return (jnp.dot(a_bf16, bh, preferred_element_type=F32) + jnp.dot(a_bf16, bl, preferred_element_type=F32))


def _pair_consts():
    c, n = RW_CHUNK, RW_HD
    lane = lax.broadcasted_iota(jnp.int32, (c, 2 * n), 1)
    row = lax.broadcasted_iota(jnp.int32, (2 * c, 2 * c), 0)
    col = lax.broadcasted_iota(jnp.int32, (2 * c, 2 * c), 1)
    same_head = (row // n) == (col // n)
    return lane < n, same_head.astype(BF16)


def _expand(x2, head_a):
    zero = jnp.zeros_like(x2)
    return jnp.concatenate([jnp.where(head_a, x2, zero), jnp.where(head_a, zero, x2)], axis=0)


def _compact(xm):
    return xm[0:RW_CHUNK] + xm[RW_CHUNK:2 * RW_CHUNK]


def _rw_chunk_kernel(r_ref, k_ref, v_ref, lo_ref, mu_ref, mulo_ref, w0_ref, w2_ref,
                     a0_ref, a2_ref, g2_ref, kk_ref, ka_ref, rk_ref,
                     y_ref, bon_ref, g_ref,
                     z_sc, pt_sc, qt_sc, rh_sc, y0_sc, prev_sc, *, nchunks):
    c, n = RW_CHUNK, RW_HD
    pw = 2 * n
    pairs = range(z_sc.shape[0])
    lns = [slice(hp * pw, (hp + 1) * pw) for hp in pairs]
    head_a, seg_ones = _pair_consts()

    @pl.when(pl.program_id(1) == 0)
    def _():
        z_sc[...] = jnp.zeros(z_sc.shape, F32)
        pt_sc[...] = jnp.zeros(pt_sc.shape, F32)
        qt_sc[...] = jnp.zeros(qt_sc.shape, F32)
        rh_sc[...] = jnp.zeros(rh_sc.shape, rh_sc.dtype)
        y0_sc[...] = jnp.zeros(y0_sc.shape, F32)
        prev_sc[...] = jnp.zeros(prev_sc.shape, F32)

    z = [z_sc[hp] for hp in pairs]
    for hp in pairs:
        y_ref[0, :, lns[hp]] = _dot(rh_sc[:, lns[hp]], z[hp]) + y0_sc[:, lns[hp]]
    z_new = [_dot3(_expand(pt_sc[:, lns[hp]], head_a), z[hp]) + _expand(qt_sc[:, lns[hp]], head_a)
             for hp in pairs]
    for hp in pairs:
        z_sc[hp] = z_new[hp]

    row = lax.broadcasted_iota(jnp.int32, (2 * c, 2 * c), 0)
    col = lax.broadcasted_iota(jnp.int32, (2 * c, 2 * c), 1)
    strict = (row % c) > (col % c)
    incl = (row % c) >= (col % c)
    trow = lax.broadcasted_iota(jnp.int32, (c, c), 0)
    tcol = lax.broadcasted_iota(jnp.int32, (c, c), 1)
    tri_ones = (trow >= tcol).astype(BF16)
    diag2 = lax.broadcasted_iota(jnp.int32, (c, pw), 0) == (lax.broadcasted_iota(jnp.int32, (c, pw), 1) % n)
    d = functools.partial(jnp.dot, preferred_element_type=F32)

    def cumsum(lw2):
        return _dot_split_lhs_rhs(tri_ones, lw2)

    ci = pl.program_id(1)
    trow1 = lax.broadcasted_iota(jnp.int32, (c, 1), 0)
    keep_prev = ci < nchunks - 1

    def shifted(cur_ref, slot, mu):
        p = cur_ref[0]
        w = p.shape[1]
        before = prev_sc[slot:slot + 1, 0:w]
        pm1 = jnp.where(trow1 == 0, before, pltpu.roll(p, 1, 0))
        prev_sc[slot:slot + 1, 0:w] = jnp.where(keep_prev, p[c - 1:c, :], before)
        return p + mu * (pm1 - p)

    r = shifted(r_ref, 0, mu_ref[0:1])
    k = shifted(k_ref, 1, mu_ref[1:2])
    v = shifted(v_ref, 2, mu_ref[2:3])
    lo = shifted(lo_ref, 3, mulo_ref[...])

    nz = -(w0_ref[...] + d(jnp.tanh(lo[:, 0:128]).astype(BF16), w2_ref[...]))
    softplus = jnp.maximum(nz, 0.0) + jnp.log(1.0 + jnp.exp(-jnp.abs(nz)))
    lw = -jnp.exp(-softplus - 0.5)
    a_ic = jax.nn.sigmoid(a0_ref[...] + d(lo[:, 128:256].astype(BF16), a2_ref[...]))
    g_ref[0] = d(jax.nn.sigmoid(lo[:, 256:512]).astype(BF16), g2_ref[...])
    kk = k * kk_ref[...]
    km = k * (1.0 + (a_ic - 1.0) * ka_ref[...])

    cum = [cumsum(lw[:, ln]) for ln in lns]
    nrm = [jnp.sqrt(_dot(kk[:, ln] * kk[:, ln], seg_ones)) for ln in lns]
    big, atm, vm, bdm, kdm = [], [], [], [], []
    for hp, ln in enumerate(lns):
        lw2, cm = lw[:, ln], cum[hp]
        g_inv = jnp.exp(-cm)
        g_rest = jnp.exp(cm[c - 1:c, :] - cm)
        kkn = kk[:, ln] / jnp.maximum(nrm[hp], 1e-12)
        b2 = kkn * a_ic[:, ln]
        km2 = km[:, ln]
        a_m = _expand(-kkn * jnp.exp(cm - lw2), head_a)
        r_m = _expand(r[:, ln] * jnp.exp(cm), head_a)
        b_m = _expand(b2 * g_inv, head_a)
        k_m = _expand(km2 * g_inv, head_a)
        atm.append(a_m)
        bdm.append(_expand(b2 * g_rest, head_a))
        kdm.append(_expand(km2 * g_rest, head_a))
        vm.append(_expand(v[:, ln], head_a))
        big.append(_dot(jnp.concatenate([a_m, r_m], axis=0), jnp.concatenate([b_m, k_m], axis=0), NT_DIMS))
    lmat = [jnp.where(strict, bg[0:pw, 0:pw], 0.0) for bg in big]
    m_rb = [jnp.where(incl, bg[pw:2 * pw, 0:pw], 0.0) for bg in big]
    mv = [_dot(jnp.concatenate([jnp.where(strict, bg[0:pw, pw:2 * pw], 0.0),
                                jnp.where(incl, bg[pw:2 * pw, pw:2 * pw], 0.0)], axis=0), vm[hp])
          for hp, bg in enumerate(big)]
    x = [jnp.concatenate([atm[hp], mv[hp][0:pw]], axis=1) for hp in pairs]
    lp = lmat
    for it in range(6):
        x = [x[hp] + _dot(lp[hp], x[hp]) for hp in pairs]
        if it < 5:
            lp = [_dot(m, m) for m in lp]
    rb = [_dot(m_rb[hp], x[hp]) for hp in pairs]
    bx = [_dot(bdm[hp], x[hp], TN_DIMS) for hp in pairs]
    kv = [_dot(kdm[hp], vm[hp], TN_DIMS) for hp in pairs]
    for hp, ln in enumerate(lns):
        g_last = jnp.exp(cum[hp][c - 1:c, :])
        pt_sc[:, ln] = _compact(bx[hp][:, 0:pw]) + jnp.where(diag2, g_last, 0.0)
        qt_sc[:, ln] = _compact(bx[hp][:, pw:2 * pw] + kv[hp])
        rh_sc[:, ln] = (r[:, ln] * jnp.exp(cum[hp]) + _compact(rb[hp][:, 0:pw])).astype(rh_sc.dtype)
        y0_sc[:, ln] = _compact(rb[hp][:, pw:2 * pw] + mv[hp][pw:2 * pw])
        bon_ref[0, :, ln] = _dot(r[:, ln] * km[:, ln] * rk_ref[:, ln], seg_ones) * v[:, ln]


def _rw_chunks(rest, mu_rkv, mu_lo, w0, w2, a0, a2, g2, k_k, k_a, r_k):
    b, s, _ = rest.shape
    d = D_MODEL
    c = RW_CHUNK
    nc = s // c
    lora_w = mu_lo.shape[1]
    cur = lambda col, w: pl.BlockSpec((1, c, w), lambda bi, ci: (bi, jnp.minimum(ci, nc - 1), col))
    prev = pl.BlockSpec((1, c, d), lambda bi, ci: (bi, jnp.maximum(ci - 1, 0), 0))
    full = lambda a: pl.BlockSpec(a.shape, lambda bi, ci: (0,) * a.ndim)
    params = (mu_rkv, mu_lo, w0, w2, a0, a2, g2, k_k, k_a, r_k)
    out = jax.ShapeDtypeStruct((b, s, d), F32)
    lora_col = (3 * d + 2 * d) // lora_w
    return pl.pallas_call(
        functools.partial(_rw_chunk_kernel, nchunks=nc),
        grid=(b, nc + 1),
        in_specs=[cur(0, d), cur(1, d), cur(2, d), cur(lora_col, lora_w)] + [full(a) for a in params],
        out_specs=[prev, cur(0, d), cur(0, d)],
        out_shape=[out, out, out],
        scratch_shapes=[pltpu.VMEM((d // (2 * RW_HD), 2 * RW_HD, 2 * RW_HD), F32),
                        pltpu.VMEM((c, d), F32), pltpu.VMEM((c, d), F32),
                        pltpu.VMEM((c, d), BF16), pltpu.VMEM((c, d), F32),
                        pltpu.VMEM((4, d), F32)],
        compiler_params=_cparams(("parallel", "arbitrary")),
        name="rwkv_chunks",
    )(rest, rest, rest, rest, *params)


def _mix_kernel(oda_ref, yrw_ref, bon_ref, gout_ref, ga_ref, gb_ref, x_ref, wda_ref, wrw_ref, wmix_ref,
                gng_ref, gnb_ref, lng_ref, lnb_ref, o_ref, *, alpha):
    y_da = jnp.dot(oda_ref[...], wda_ref[...], preferred_element_type=F32)
    _, seg_ones = _pair_consts()
    parts = []
    for hp in range(RW_HEADS // 2):
        ln = slice(hp * 2 * RW_HD, (hp + 1) * 2 * RW_HD)
        y = yrw_ref[:, ln]
        mu = _dot_split_lhs(y, seg_ones) * (1.0 / RW_HD)
        yc = y - mu
        var = _dot_split_lhs(yc * yc, seg_ones) * (1.0 / RW_HD)
        parts.append(yc * lax.rsqrt(var + RW_LNX_EPS))
    yn = jnp.concatenate(parts, axis=1) * gng_ref[...] + gnb_ref[...] + bon_ref[...]
    y_rw = jnp.dot((yn * gout_ref[...]).astype(BF16), wrw_ref[...], preferred_element_type=F32)
    mixed = jax.nn.sigmoid(ga_ref[...]) * y_da + jax.nn.sigmoid(gb_ref[...]) * y_rw
    z = alpha * x_ref[...] + jnp.dot(mixed.astype(BF16), wmix_ref[...], preferred_element_type=F32)
    o_ref[...] = _layer_norm_rows(z, lng_ref[...], lnb_ref[...])


def _mix(o_da, y_rw, bonus, g_out, rest, x, w_da, w_rw, w_mix, gn_g, gn_b, ln_g, ln_b, alpha, tm=512):
    t, d = x.shape
    rowb = lambda c: pl.BlockSpec((tm, d), lambda i: (i, c))
    full = lambda a: pl.BlockSpec(a.shape, lambda i: (0,) * a.ndim)
    return pl.pallas_call(
        functools.partial(_mix_kernel, alpha=alpha),
        grid=(t // tm,),
        in_specs=[rowb(0), rowb(0), rowb(0), rowb(0), rowb(3), rowb(4), rowb(0),
                  full(w_da), full(w_rw), full(w_mix), full(gn_g), full(gn_b), full(ln_g), full(ln_b)],
        out_specs=rowb(0),
        out_shape=jax.ShapeDtypeStruct((t, d), F32),
        compiler_params=_cparams(("parallel",)),
        name="mix_merge_ln1",
    )(o_da, y_rw, bonus, g_out, rest, rest, x, w_da, w_rw, w_mix, gn_g, gn_b, ln_g, ln_b)


def _mem_kv_kernel(mem_ref, g_ref, b_ref, w_ref, ck_ref, cv_ref):
    m = _layer_norm_rows(mem_ref[0], g_ref[...], b_ref[...])
    kv = jnp.dot(m.astype(BF16), w_ref[...], preferred_element_type=F32)
    ck_ref[0] = kv[:, :D_MODEL].astype(BF16)
    cv_ref[0] = kv[:, D_MODEL:].astype(BF16)


def _mem_kv(mem, g, b, w_ckv):
    bsz, m, d = mem.shape
    full = lambda a: pl.BlockSpec(a.shape, lambda i: (0,) * a.ndim)
    blk = pl.BlockSpec((1, m, d), lambda i: (i, 0, 0))
    return pl.pallas_call(
        _mem_kv_kernel,
        grid=(bsz,),
        in_specs=[blk, full(g), full(b), full(w_ckv)],
        out_specs=[blk, blk],
        out_shape=[jax.ShapeDtypeStruct((bsz, m, d), BF16)] * 2,
        compiler_params=_cparams(("parallel",)),
        name="mem_kv",
    )(mem, g, b, w_ckv)


def _cross_kernel(x_ref, ck_ref, cv_ref, wq_ref, wo_ref, lng_ref, lnb_ref, o_ref, *, alpha):
    x = x_ref[...]
    cq = jnp.dot(x.astype(BF16), wq_ref[...], preferred_element_type=F32)
    cq = (cq * (CA_HD ** -0.5)).astype(BF16)
    outs = []
    for h in range(CA_HEADS):
        sl = slice(h * CA_HD, (h + 1) * CA_HD)
        s = lax.dot_general(cq[:, sl], ck_ref[0, :, sl], NT_DIMS, preferred_element_type=F32)
        s = s - jnp.max(s, axis=-1, keepdims=True)
        p = jnp.exp(s)
        p = p / jnp.sum(p, axis=-1, keepdims=True)
        outs.append(jnp.dot(p.astype(BF16), cv_ref[0, :, sl], preferred_element_type=F32))
    co = jnp.dot(jnp.concatenate(outs, axis=1).astype(BF16), wo_ref[...], preferred_element_type=F32)
    o_ref[...] = _layer_norm_rows(alpha * x + co, lng_ref[...], lnb_ref[...])


def _cross_attention(x, ck, cv, w_cq, w_co, ln_g, ln_b, seq, alpha, tm=512):
    t, d = x.shape
    tiles_per_seq = seq // tm
    full = lambda a: pl.BlockSpec(a.shape, lambda i: (0,) * a.ndim)
    rowb = pl.BlockSpec((tm, d), lambda i: (i, 0))
    memb = pl.BlockSpec((1,) + ck.shape[1:], lambda i: (i // tiles_per_seq, 0, 0))
    return pl.pallas_call(
        functools.partial(_cross_kernel, alpha=alpha),
        grid=(t // tm,),
        in_specs=[rowb, memb, memb, full(w_cq), full(w_co), full(ln_g), full(ln_b)],
        out_specs=rowb,
        out_shape=jax.ShapeDtypeStruct((t, d), F32),
        compiler_params=_cparams(("parallel",)),
        name="cross_attention_ln2",
    )(x, ck, cv, w_cq, w_co, ln_g, ln_b)


def _ffn_kernel(x_ref, wg_ref, wu_ref, wo_ref, lng_ref, lnb_ref, o_ref, xb_ref, acc_ref, *, alpha):
    j = pl.program_id(1)

    @pl.when(j == 0)
    def _():
        xb_ref[...] = x_ref[...].astype(BF16)
        acc_ref[...] = jnp.zeros(acc_ref.shape, F32)

    xb = xb_ref[...]
    gate = jnp.dot(xb, wg_ref[...], preferred_element_type=F32)
    up = jnp.dot(xb, wu_ref[...], preferred_element_type=F32)
    h = (gate * jax.nn.sigmoid(gate) * up).astype(BF16)
    acc_ref[...] += jnp.dot(h, wo_ref[...], preferred_element_type=F32)

    @pl.when(j == pl.num_programs(1) - 1)
    def _():
        o_ref[...] = _layer_norm_rows(alpha * x_ref[...] + acc_ref[...], lng_ref[...], lnb_ref[...])


def _ffn(x, w_in, w_out, ln_g, ln_b, alpha, tm=512, tf=1408):
    t, d = x.shape
    nf = D_FF // tf
    full = lambda a: pl.BlockSpec(a.shape, lambda i, j: (0,) * a.ndim)
    rowb = pl.BlockSpec((tm, d), lambda i, j: (i, 0))
    return pl.pallas_call(
        functools.partial(_ffn_kernel, alpha=alpha),
        grid=(t // tm, nf),
        in_specs=[rowb,
                  pl.BlockSpec((d, tf), lambda i, j: (0, j)),
                  pl.BlockSpec((d, tf), lambda i, j: (0, nf + j)),
                  pl.BlockSpec((tf, d), lambda i, j: (j, 0)),
                  full(ln_g), full(ln_b)],
        out_specs=rowb,
        out_shape=jax.ShapeDtypeStruct((t, d), F32),
        scratch_shapes=[pltpu.VMEM((tm, d), BF16), pltpu.VMEM((tm, d), F32)],
        compiler_params=_cparams(("parallel", "arbitrary")),
        name="swiglu_ffn_ln3",
    )(x, w_in, w_in, w_out, ln_g, ln_b)


def _pad_cols(a, width):
    return jnp.pad(a, ((0, 0), (0, width - a.shape[1])))


def _pad_rows(a, height):
    return jnp.pad(a, ((0, height - a.shape[0]), (0, 0)))


def kernel(x, mem, rel_bias, w_in, shift_mu, lambda_q1, lambda_k1, lambda_q2, lambda_k2, da_subln_g, w_da_proj, rw_w0, rw_w2, rw_a0, rw_a2, rw_g2, rw_k_k, rw_k_a, rw_r_k, rw_lnx_g, rw_lnx_b, w_rw_proj, w_mix_out, ln1_g, ln1_b, mem_ln_g, mem_ln_b, w_cq, w_ckv, w_co, ln2_g, ln2_b, w_ffn_in, w_ffn_out, ln3_g, ln3_b):
    bsz, seq, d = x.shape
    depth = w_in.shape[0]
    alpha = (2.0 * depth) ** 0.25
    t = bsz * seq
    row = lambda a: a.reshape(1, -1).astype(F32)
    bias_tiles = _attn_flat_bias(rel_bias)
    xf = x.reshape(t, d)

    qkv_w = 3 * D_MODEL
    rw0 = qkv_w
    lo0 = rw0 + 3 * D_MODEL
    gate0 = lo0 + DECAY_LORA + AAA_LORA + GATE_LORA

    for l in range(depth):
        w = w_in[l]
        w_qt = w[:, :D_MODEL].T.astype(BF16)
        w_k = w[:, D_MODEL:2 * D_MODEL].astype(BF16)
        w_vt = w[:, 2 * D_MODEL:qkv_w].T.astype(BF16)
        w_rest = jnp.concatenate([
            w[:, rw0:lo0], w[:, gate0:gate0 + 2 * D_MODEL],
            _pad_cols(w[:, lo0:lo0 + DECAY_LORA], 128),
            _pad_cols(w[:, lo0 + DECAY_LORA:lo0 + DECAY_LORA + AAA_LORA], 128),
            _pad_cols(w[:, lo0 + DECAY_LORA + AAA_LORA:gate0], 256)], axis=1).astype(BF16)
        mu = shift_mu[l].astype(F32)
        mu_rkv = mu[:3 * D_MODEL].reshape(3, D_MODEL)
        o1 = 3 * D_MODEL
        mu_lo = jnp.concatenate([
            _pad_cols(mu[None, o1:o1 + DECAY_LORA], 128),
            _pad_cols(mu[None, o1 + DECAY_LORA:o1 + DECAY_LORA + AAA_LORA], 128),
            _pad_cols(mu[None, o1 + DECAY_LORA + AAA_LORA:], 256)], axis=1)
        w2 = _pad_rows(rw_w2[l], 128).astype(BF16)
        a2 = _pad_rows(rw_a2[l], 128).astype(BF16)
        g2 = _pad_rows(rw_g2[l], 256).astype(BF16)

        qt, kda, vt = _qkv_proj(xf, w_qt, w_k, w_vt, bsz, seq)
        rest = _matmul(xf, w_rest, F32, 1024, 1408, "in_proj_rest")

        lam_init = 0.8 - 0.6 * math.exp(-0.3 * l)
        lamv = jnp.stack([lambda_q1[l], lambda_k1[l], lambda_q2[l], lambda_k2[l]]).astype(F32)
        o_da = _diff_attention_flat(qt, kda.reshape(bsz, seq, d), vt, lamv, bias_tiles, row(da_subln_g[l]),
                                    lam_init)

        y_rw, bon, g_out = _rw_chunks(
            rest.reshape(bsz, seq, rest.shape[1]), mu_rkv, mu_lo, row(rw_w0[l]), w2, row(rw_a0[l]),
            a2, g2, row(rw_k_k[l]), row(rw_k_a[l]), row(rw_r_k[l]))

        x1 = _mix(o_da.reshape(t, d), y_rw.reshape(t, d), bon.reshape(t, d), g_out.reshape(t, d), rest, xf,
                  w_da_proj[l].astype(BF16), w_rw_proj[l].astype(BF16), w_mix_out[l].astype(BF16),
                  row(rw_lnx_g[l]), row(rw_lnx_b[l]), row(ln1_g[l]), row(ln1_b[l]), alpha)

        ck, cv = _mem_kv(mem, row(mem_ln_g[l]), row(mem_ln_b[l]), w_ckv[l].astype(BF16))
        x2 = _cross_attention(x1, ck, cv, w_cq[l].astype(BF16), w_co[l].astype(BF16),
                              row(ln2_g[l]), row(ln2_b[l]), seq, alpha)

        xf = _ffn(x2, w_ffn_in[l].astype(BF16), w_ffn_out[l].astype(BF16),
                  row(ln3_g[l]), row(ln3_b[l]), alpha)
    return xf.reshape(bsz, seq, d)
```

```python
import functools
import math

import jax
import jax.numpy as jnp
from jax import lax
from jax.experimental import pallas as pl
from jax.experimental.pallas import tpu as pltpu

F32 = jnp.float32
BF16 = jnp.bfloat16

D_MODEL = 1024
DA_HEADS = 8
DA_HD = 64
NUM_BUCKETS = 32
MAX_DISTANCE = 128
RW_HD = 64
RW_HEADS = D_MODEL // RW_HD
DECAY_LORA = 64
AAA_LORA = 64
GATE_LORA = 160
RW_LNX_EPS = 64e-5
MEM_TOKENS = 256
CA_HEADS = 4
CA_HD = D_MODEL // CA_HEADS
D_FF = 2816
LN_EPS = 1e-5
NEG_BIG = -1e30

LOG2E = math.log2(math.e)
ATT_HEADS_PER_STEP = 2
ATT_QT = 512
ATT_KB = 256
ATT_BIAS_TYPES = 4
ATT_ROTATE = 3
ATT_ONES_ROWS = 16
RW_CHUNK = 64
VMEM_LIMIT = 56 * 1024 * 1024

NT_DIMS = (((1,), (1,)), ((), ()))
TN_DIMS = (((0,), (0,)), ((), ()))


def _cparams(sem, vmem=VMEM_LIMIT):
    return pltpu.CompilerParams(dimension_semantics=sem, vmem_limit_bytes=vmem)


def _split2(a):
    hi = a.astype(BF16)
    lo = (a - hi.astype(F32)).astype(BF16)
    return hi, lo


def _dot3(a, b, dims=(((1,), (0,)), ((), ()))):
    ah, al = _split2(a)
    bh, bl = _split2(b)
    d = functools.partial(lax.dot_general, dimension_numbers=dims, preferred_element_type=F32)
    return d(ah, bh) + (d(ah, bl) + d(al, bh))


def _layer_norm_rows(z, g, b):
    mu = jnp.mean(z, axis=-1, keepdims=True)
    zc = z - mu
    var = jnp.mean(zc * zc, axis=-1, keepdims=True)
    return zc * lax.rsqrt(var + LN_EPS) * g + b


def _resident(a):
    return pl.BlockSpec(a.shape, lambda *_: (0,) * a.ndim, pipeline_mode=pl.Buffered(1))


def _mm_kernel(x_ref, w_ref, o_ref, *, tn):
    xb = x_ref[...].astype(BF16)
    for j in range(w_ref.shape[1] // tn):
        cols = slice(j * tn, (j + 1) * tn)
        o_ref[:, cols] = jnp.dot(xb, w_ref[:, cols], preferred_element_type=F32).astype(o_ref.dtype)


def _matmul(x, w, out_dtype, tm, tn, name):
    t, k = x.shape
    n = w.shape[1]
    return pl.pallas_call(
        functools.partial(_mm_kernel, tn=tn),
        grid=(t // tm,),
        in_specs=[pl.BlockSpec((tm, k), lambda i: (i, 0)), _resident(w)],
        out_specs=pl.BlockSpec((tm, n), lambda i: (i, 0)),
        out_shape=jax.ShapeDtypeStruct((t, n), out_dtype),
        compiler_params=_cparams(("parallel",)),
        name=name,
    )(x, w)


def _qkv_kernel(x_ref, wqt_ref, wk_ref, wvt_ref, qt_ref, k_ref, vt_ref):
    xb = x_ref[...].astype(BF16)
    k_ref[...] = jnp.dot(xb, wk_ref[...], preferred_element_type=F32).astype(BF16)
    qt = lax.dot_general(wqt_ref[...], xb, NT_DIMS, preferred_element_type=F32)
    qt_ref[0] = (qt * (DA_HD ** -0.5 * LOG2E)).astype(BF16)
    vt_ref[0] = lax.dot_general(wvt_ref[...], xb, NT_DIMS, preferred_element_type=F32).astype(BF16)


def _qkv_proj(x, wqt, wk, wvt, bsz, seq, tm=512):
    t, d = x.shape
    tiles_per_seq = seq // tm
    full = lambda a: pl.BlockSpec(a.shape, lambda i: (0,) * a.ndim)
    tblk = pl.BlockSpec((1, d, tm), lambda i: (i // tiles_per_seq, 0, i % tiles_per_seq))
    rowb = pl.BlockSpec((tm, d), lambda i: (i, 0))
    tshape = jax.ShapeDtypeStruct((bsz, d, seq), BF16)
    return pl.pallas_call(
        _qkv_kernel,
        grid=(t // tm,),
        in_specs=[rowb, full(wqt), full(wk), full(wvt)],
        out_specs=[tblk, rowb, tblk],
        out_shape=[tshape, jax.ShapeDtypeStruct((t, d), BF16), tshape],
        compiler_params=_cparams(("parallel",)),
        name="in_proj_qkv",
    )(x, wqt, wk, wvt)


def _t5_bucket(rel):
    n = jnp.maximum(rel, 0)
    max_exact = NUM_BUCKETS // 2
    nf = jnp.maximum(n, 1).astype(jnp.float32)
    large = max_exact + jnp.floor(jnp.log(nf / max_exact) / math.log(MAX_DISTANCE / max_exact)
                                  * (NUM_BUCKETS - max_exact)).astype(jnp.int32)
    large = jnp.minimum(large, NUM_BUCKETS - 1)
    return jnp.where(n < max_exact, n, large)


def _col_max(s):
    while s.shape[0] > 8:
        half = s.shape[0] // 2
        s = jnp.maximum(s[:half], s[half:])
    return jnp.max(s, axis=0, keepdims=True)


def _attn_unit_tables(seq):
    kb_per_qt = ATT_QT // ATT_KB
    qts, kbs, types = [], [], []
    for qt in range(seq // ATT_QT):
        nkb = (qt + 1) * kb_per_qt
        for kb in range(nkb):
            from_end = nkb - 1 - kb
            qts.append(qt)
            kbs.append(kb)
            types.append(ATT_BIAS_TYPES - 1 - from_end if from_end < ATT_BIAS_TYPES - 1 else 0)
    assert len(qts) % ATT_ROTATE == 0
    nunits = len(qts)
    for _ in range(ATT_ROTATE - 1):
        qts.append(0)
        kbs.append(0)
        types.append(0)
    as_i32 = lambda v: jnp.asarray(v, jnp.int32)
    return nunits, as_i32(qts), as_i32(kbs), as_i32(types)


def _attn_flat_bias(rel_bias):
    tab = rel_bias.astype(F32) - rel_bias[NUM_BUCKETS - 1].astype(F32)[None, :]
    base = jnp.arange(ATT_QT)[None, :] - jnp.arange(ATT_KB)[:, None]

    def tile(key_offset):
        rel = base - key_offset
        onehot = (_t5_bucket(rel)[:, :, None] == jnp.arange(NUM_BUCKETS)[None, None, :]).astype(F32)
        vals = jnp.einsum('kqb,bh->hkq', onehot, tab, precision=lax.Precision.HIGHEST)
        return jnp.where((rel >= 0)[None], vals, NEG_BIG)

    zeros = jnp.zeros((rel_bias.shape[1], ATT_KB, ATT_QT), F32)
    return jnp.stack([zeros, tile(-ATT_KB), tile(0), tile(ATT_KB)], axis=1) * LOG2E


def _attn_flat_kernel(tq_ref, tk_ref, tt_ref, lamv_ref, qt_ref, k_ref, vt_ref, bias_ref, g_ref, o_ref,
                      m_sc, acc_sc, sa_sc, sb_sc, sc_sc, ma_sc, mb_sc, mc_sc, *, lam_init, nunits):
    hw = 2 * DA_HD
    heads = range(ATT_HEADS_PER_STEP)
    hrows = [slice(hh * hw, (hh + 1) * hw) for hh in heads]
    ntiles = m_sc.shape[0]

    lv = lamv_ref[...]
    lam = (jnp.exp(jnp.sum(lv[0:1] * lv[1:2], axis=-1, keepdims=True))
           - jnp.exp(jnp.sum(lv[2:3] * lv[3:4], axis=-1, keepdims=True)) + lam_init)

    m_sc[...] = jnp.full(m_sc.shape, NEG_BIG, F32)
    acc_sc[...] = jnp.zeros(acc_sc.shape, F32)
    rowi = lax.broadcasted_iota(jnp.int32, (hw, ATT_QT), 0)

    def fill(buf, mbuf, u):
        q0 = pl.multiple_of(tq_ref[u] * ATT_QT, ATT_QT)
        k0 = pl.multiple_of(tk_ref[u] * ATT_KB, ATT_KB)
        ty = tt_ref[u]
        s = []
        for hh in heads:
            q = qt_ref[0, hrows[hh], pl.ds(q0, ATT_QT)]
            zero = jnp.zeros_like(q)
            qq = jnp.concatenate([jnp.where(rowi < DA_HD, q, zero), jnp.where(rowi >= DA_HD, q, zero)], axis=1)
            s.append(jnp.dot(k_ref[0, pl.ds(k0, ATT_KB), hrows[hh]], qq, preferred_element_type=F32))
        for hh in heads:
            b = bias_ref[hh, ty]
            sb = s[hh] + jnp.concatenate([b, b], axis=1)
            buf[hh] = sb
            mbuf[hh] = _col_max(sb)

    def drain(buf, mbuf, u):
        t = tq_ref[u]
        k0 = pl.multiple_of(tk_ref[u] * ATT_KB, ATT_KB)
        m_prev = [m_sc[t, hh] for hh in heads]
        m_new = [jnp.maximum(m_prev[hh], mbuf[hh]) for hh in heads]
        alpha = [jnp.exp2(m_prev[hh] - m_new[hh]) for hh in heads]
        p = [jnp.exp2((buf[hh] - m_new[hh]).astype(BF16)) for hh in heads]
        ones = jnp.ones((ATT_ONES_ROWS, ATT_KB), BF16)
        pv = [jnp.dot(jnp.concatenate([vt_ref[0, hrows[hh], pl.ds(k0, ATT_KB)], ones], axis=0), p[hh],
                      preferred_element_type=F32) for hh in heads]
        for hh in heads:
            acc_sc[t, hh] = alpha[hh] * acc_sc[t, hh] + pv[hh]
            m_sc[t, hh] = m_new[hh]

    fill(sa_sc, ma_sc, 0)
    fill(sb_sc, mb_sc, 1)

    def trip(i, carry):
        u = i * ATT_ROTATE
        fill(sc_sc, mc_sc, u + 2)
        drain(sa_sc, ma_sc, u)
        fill(sa_sc, ma_sc, u + 3)
        drain(sb_sc, mb_sc, u + 1)
        fill(sb_sc, mb_sc, u + 4)
        drain(sc_sc, mc_sc, u + 2)
        return carry

    lax.fori_loop(0, nunits // ATT_ROTATE, trip, 0)

    def finish(t, carry):
        q0 = pl.multiple_of(t * ATT_QT, ATT_QT)
        for hh in heads:
            a = acc_sc[t, hh]
            acc = a[0:hw, :] * (1.0 / a[hw:hw + 1, :])
            ot = acc[:, 0:ATT_QT] - lam * acc[:, ATT_QT:2 * ATT_QT]
            ms = jnp.mean(ot * ot, axis=0, keepdims=True)
            ot = ot * (lax.rsqrt(ms + LN_EPS) * (1.0 - lam_init))
            o_ref[0, pl.ds(q0, ATT_QT), hrows[hh]] = (ot.T * g_ref[...]).astype(o_ref.dtype)
        return carry

    lax.fori_loop(0, ntiles, finish, 0)


def _diff_attention_flat(qt, k, vt, lamv, bias, subln_g, lam_init):
    b, s, _ = k.shape
    nh = ATT_HEADS_PER_STEP
    hw = 2 * DA_HD
    nunits, tq, tk, tt = _attn_unit_tables(s)
    score = pltpu.VMEM((nh, ATT_KB, 2 * ATT_QT), F32)
    smax = pltpu.VMEM((nh, 1, 2 * ATT_QT), F32)
    grid_spec = pltpu.PrefetchScalarGridSpec(
        num_scalar_prefetch=3,
        grid=(b, DA_HEADS // nh),
        in_specs=[
            pl.BlockSpec((4, DA_HD), lambda bi, hi, *_: (0, 0)),
            pl.BlockSpec((1, nh * hw, s), lambda bi, hi, *_: (bi, hi, 0)),
            pl.BlockSpec((1, s, nh * hw), lambda bi, hi, *_: (bi, 0, hi)),
            pl.BlockSpec((1, nh * hw, s), lambda bi, hi, *_: (bi, hi, 0)),
            pl.BlockSpec((nh, ATT_BIAS_TYPES, ATT_KB, ATT_QT), lambda bi, hi, *_: (hi, 0, 0, 0)),
            pl.BlockSpec((1, hw), lambda bi, hi, *_: (0, 0)),
        ],
        out_specs=pl.BlockSpec((1, s, nh * hw), lambda bi, hi, *_: (bi, 0, hi)),
        scratch_shapes=[pltpu.VMEM((s // ATT_QT, nh, 1, 2 * ATT_QT), F32),
                        pltpu.VMEM((s // ATT_QT, nh, hw + ATT_ONES_ROWS, 2 * ATT_QT), F32),
                        score, score, score, smax, smax, smax],
    )
    return pl.pallas_call(
        functools.partial(_attn_flat_kernel, lam_init=lam_init, nunits=nunits),
        grid_spec=grid_spec,
        out_shape=jax.ShapeDtypeStruct((b, s, DA_HEADS * hw), BF16),
        compiler_params=_cparams(("parallel", "parallel")),
        name="diff_attention",
    )(tq, tk, tt, lamv, qt, k, vt, bias, subln_g)


def _dot(a, b, dims=(((1,), (0,)), ((), ()))):
    return lax.dot_general(a.astype(BF16), b.astype(BF16), dims, preferred_element_type=F32)


def _dot_split_lhs(a, b_bf16):
    ah, al = _split2(a)
    return (jnp.dot(ah, b_bf16, preferred_element_type=F32) + jnp.dot(al, b_bf16, preferred_element_type=F32))


def _dot_split_lhs_rhs(a_bf16, b):
    bh, bl = _split2(b)
    return (jnp.dot(a_bf16, bh, preferred_element_type=F32) + jnp.dot(a_bf16, bl, preferred_element_type=F32))


def _pair_consts():
    c, n = RW_CHUNK, RW_HD
    lane = lax.broadcasted_iota(jnp.int32, (c, 2 * n), 1)
    row = lax.broadcasted_iota(jnp.int32, (2 * c, 2 * c), 0)
    col = lax.broadcasted_iota(jnp.int32, (2 * c, 2 * c), 1)
    same_head = (row // n) == (col // n)
    return lane < n, same_head.astype(BF16)


def _expand(x2, head_a):
    zero = jnp.zeros_like(x2)
    return jnp.concatenate([jnp.where(head_a, x2, zero), jnp.where(head_a, zero, x2)], axis=0)


def _compact(xm):
    return xm[0:RW_CHUNK] + xm[RW_CHUNK:2 * RW_CHUNK]


def _rw_chunk_kernel(r_ref, k_ref, v_ref, lo_ref, mu_ref, mulo_ref, w0_ref, w2_ref,
                     a0_ref, a2_ref, g2_ref, kk_ref, ka_ref, rk_ref,
                     y_ref, bon_ref, g_ref,
                     z_sc, pt_sc, qt_sc, rh_sc, y0_sc, prev_sc, *, nchunks):
    c, n = RW_CHUNK, RW_HD
    pw = 2 * n
    pairs = range(z_sc.shape[0])
    lns = [slice(hp * pw, (hp + 1) * pw) for hp in pairs]
    head_a, seg_ones = _pair_consts()

    @pl.when(pl.program_id(1) == 0)
    def _():
        z_sc[...] = jnp.zeros(z_sc.shape, F32)
        pt_sc[...] = jnp.zeros(pt_sc.shape, F32)
        qt_sc[...] = jnp.zeros(qt_sc.shape, F32)
        rh_sc[...] = jnp.zeros(rh_sc.shape, rh_sc.dtype)
        y0_sc[...] = jnp.zeros(y0_sc.shape, F32)
        prev_sc[...] = jnp.zeros(prev_sc.shape, F32)

    z = [z_sc[hp] for hp in pairs]
    for hp in pairs:
        y_ref[0, :, lns[hp]] = _dot(rh_sc[:, lns[hp]], z[hp]) + y0_sc[:, lns[hp]]
    z_new = [_dot3(_expand(pt_sc[:, lns[hp]], head_a), z[hp]) + _expand(qt_sc[:, lns[hp]], head_a)
             for hp in pairs]
    for hp in pairs:
        z_sc[hp] = z_new[hp]

    row = lax.broadcasted_iota(jnp.int32, (2 * c, 2 * c), 0)
    col = lax.broadcasted_iota(jnp.int32, (2 * c, 2 * c), 1)
    strict = (row % c) > (col % c)
    incl = (row % c) >= (col % c)
    trow = lax.broadcasted_iota(jnp.int32, (c, c), 0)
    tcol = lax.broadcasted_iota(jnp.int32, (c, c), 1)
    tri_ones = (trow >= tcol).astype(BF16)
    diag2 = lax.broadcasted_iota(jnp.int32, (c, pw), 0) == (lax.broadcasted_iota(jnp.int32, (c, pw), 1) % n)
    d = functools.partial(jnp.dot, preferred_element_type=F32)

    def cumsum(lw2):
        return _dot_split_lhs_rhs(tri_ones, lw2)

    ci = pl.program_id(1)
    trow1 = lax.broadcasted_iota(jnp.int32, (c, 1), 0)
    keep_prev = ci < nchunks - 1

    def shifted(cur_ref, slot, mu):
        p = cur_ref[0]
        w = p.shape[1]
        before = prev_sc[slot:slot + 1, 0:w]
        pm1 = jnp.where(trow1 == 0, before, pltpu.roll(p, 1, 0))
        prev_sc[slot:slot + 1, 0:w] = jnp.where(keep_prev, p[c - 1:c, :], before)
        return p + mu * (pm1 - p)

    r = shifted(r_ref, 0, mu_ref[0:1])
    k = shifted(k_ref, 1, mu_ref[1:2])
    v = shifted(v_ref, 2, mu_ref[2:3])
    lo = shifted(lo_ref, 3, mulo_ref[...])

    nz = -(w0_ref[...] + d(jnp.tanh(lo[:, 0:128]).astype(BF16), w2_ref[...]))
    softplus = jnp.maximum(nz, 0.0) + jnp.log(1.0 + jnp.exp(-jnp.abs(nz)))
    lw = -jnp.exp(-softplus - 0.5)
    a_ic = jax.nn.sigmoid(a0_ref[...] + d(lo[:, 128:256].astype(BF16), a2_ref[...]))
    g_ref[0] = d(jax.nn.sigmoid(lo[:, 256:512]).astype(BF16), g2_ref[...])
    kk = k * kk_ref[...]
    km = k * (1.0 + (a_ic - 1.0) * ka_ref[...])

    cum = [cumsum(lw[:, ln]) for ln in lns]
    nrm = [jnp.sqrt(_dot(kk[:, ln] * kk[:, ln], seg_ones)) for ln in lns]
    big, atm, vm, bdm, kdm = [], [], [], [], []
    for hp, ln in enumerate(lns):
        lw2, cm = lw[:, ln], cum[hp]
        g_inv = jnp.exp(-cm)
        g_rest = jnp.exp(cm[c - 1:c, :] - cm)
        kkn = kk[:, ln] / jnp.maximum(nrm[hp], 1e-12)
        b2 = kkn * a_ic[:, ln]
        km2 = km[:, ln]
        a_m = _expand(-kkn * jnp.exp(cm - lw2), head_a)
        r_m = _expand(r[:, ln] * jnp.exp(cm), head_a)
        b_m = _expand(b2 * g_inv, head_a)
        k_m = _expand(km2 * g_inv, head_a)
        atm.append(a_m)
        bdm.append(_expand(b2 * g_rest, head_a))
        kdm.append(_expand(km2 * g_rest, head_a))
        vm.append(_expand(v[:, ln], head_a))
        big.append(_dot(jnp.concatenate([a_m, r_m], axis=0), jnp.concatenate([b_m, k_m], axis=0), NT_DIMS))
    lmat = [jnp.where(strict, bg[0:pw, 0:pw], 0.0) for bg in big]
    m_rb = [jnp.where(incl, bg[pw:2 * pw, 0:pw], 0.0) for bg in big]
    mv = [_dot(jnp.concatenate([jnp.where(strict, bg[0:pw, pw:2 * pw], 0.0),
                                jnp.where(incl, bg[pw:2 * pw, pw:2 * pw], 0.0)], axis=0), vm[hp])
          for hp, bg in enumerate(big)]
    x = [jnp.concatenate([atm[hp], mv[hp][0:pw]], axis=1) for hp in pairs]
    lp = lmat
    for it in range(6):
        x = [x[hp] + _dot(lp[hp], x[hp]) for hp in pairs]
        if it < 5:
            lp = [_dot(m, m) for m in lp]
    rb = [_dot(m_rb[hp], x[hp]) for hp in pairs]
    bx = [_dot(bdm[hp], x[hp], TN_DIMS) for hp in pairs]
    kv = [_dot(kdm[hp], vm[hp], TN_DIMS) for hp in pairs]
    for hp, ln in enumerate(lns):
        g_last = jnp.exp(cum[hp][c - 1:c, :])
        pt_sc[:, ln] = _compact(bx[hp][:, 0:pw]) + jnp.where(diag2, g_last, 0.0)
        qt_sc[:, ln] = _compact(bx[hp][:, pw:2 * pw] + kv[hp])
        rh_sc[:, ln] = (r[:, ln] * jnp.exp(cum[hp]) + _compact(rb[hp][:, 0:pw])).astype(rh_sc.dtype)
        y0_sc[:, ln] = _compact(rb[hp][:, pw:2 * pw] + mv[hp][pw:2 * pw])
        bon_ref[0, :, ln] = _dot(r[:, ln] * km[:, ln] * rk_ref[:, ln], seg_ones) * v[:, ln]


def _rw_chunks(rest, mu_rkv, mu_lo, w0, w2, a0, a2, g2, k_k, k_a, r_k):
    b, s, _ = rest.shape
    d = D_MODEL
    c = RW_CHUNK
    nc = s // c
    lora_w = mu_lo.shape[1]
    cur = lambda col, w: pl.BlockSpec((1, c, w), lambda bi, ci: (bi, jnp.minimum(ci, nc - 1), col))
    prev = pl.BlockSpec((1, c, d), lambda bi, ci: (bi, jnp.maximum(ci - 1, 0), 0))
    full = lambda a: pl.BlockSpec(a.shape, lambda bi, ci: (0,) * a.ndim)
    params = (mu_rkv, mu_lo, w0, w2, a0, a2, g2, k_k, k_a, r_k)
    out = jax.ShapeDtypeStruct((b, s, d), F32)
    lora_col = (3 * d + 2 * d) // lora_w
    return pl.pallas_call(
        functools.partial(_rw_chunk_kernel, nchunks=nc),
        grid=(b, nc + 1),
        in_specs=[cur(0, d), cur(1, d), cur(2, d), cur(lora_col, lora_w)] + [full(a) for a in params],
        out_specs=[prev, cur(0, d), cur(0, d)],
        out_shape=[out, out, out],
        scratch_shapes=[pltpu.VMEM((d // (2 * RW_HD), 2 * RW_HD, 2 * RW_HD), F32),
                        pltpu.VMEM((c, d), F32), pltpu.VMEM((c, d), F32),
                        pltpu.VMEM((c, d), BF16), pltpu.VMEM((c, d), F32),
                        pltpu.VMEM((4, d), F32)],
        compiler_params=_cparams(("parallel", "arbitrary")),
        name="rwkv_chunks",
    )(rest, rest, rest, rest, *params)


def _mix_kernel(oda_ref, yrw_ref, bon_ref, gout_ref, ga_ref, gb_ref, x_ref, wda_ref, wrw_ref, wmix_ref,
                gng_ref, gnb_ref, lng_ref, lnb_ref, o_ref, *, alpha):
    y_da = jnp.dot(oda_ref[...], wda_ref[...], preferred_element_type=F32)
    _, seg_ones = _pair_consts()
    parts = []
    for hp in range(RW_HEADS // 2):
        ln = slice(hp * 2 * RW_HD, (hp + 1) * 2 * RW_HD)
        y = yrw_ref[:, ln]
        mu = _dot_split_lhs(y, seg_ones) * (1.0 / RW_HD)
        yc = y - mu
        var = _dot_split_lhs(yc * yc, seg_ones) * (1.0 / RW_HD)
        parts.append(yc * lax.rsqrt(var + RW_LNX_EPS))
    yn = jnp.concatenate(parts, axis=1) * gng_ref[...] + gnb_ref[...] + bon_ref[...]
    y_rw = jnp.dot((yn * gout_ref[...]).astype(BF16), wrw_ref[...], preferred_element_type=F32)
    mixed = jax.nn.sigmoid(ga_ref[...]) * y_da + jax.nn.sigmoid(gb_ref[...]) * y_rw
    z = alpha * x_ref[...] + jnp.dot(mixed.astype(BF16), wmix_ref[...], preferred_element_type=F32)
    o_ref[...] = _layer_norm_rows(z, lng_ref[...], lnb_ref[...])


def _mix(o_da, y_rw, bonus, g_out, rest, x, w_da, w_rw, w_mix, gn_g, gn_b, ln_g, ln_b, alpha, tm=512):
    t, d = x.shape
    rowb = lambda c: pl.BlockSpec((tm, d), lambda i: (i, c))
    full = lambda a: pl.BlockSpec(a.shape, lambda i: (0,) * a.ndim)
    return pl.pallas_call(
        functools.partial(_mix_kernel, alpha=alpha),
        grid=(t // tm,),
        in_specs=[rowb(0), rowb(0), rowb(0), rowb(0), rowb(3), rowb(4), rowb(0),
                  full(w_da), full(w_rw), full(w_mix), full(gn_g), full(gn_b), full(ln_g), full(ln_b)],
        out_specs=rowb(0),
        out_shape=jax.ShapeDtypeStruct((t, d), F32),
        compiler_params=_cparams(("parallel",)),
        name="mix_merge_ln1",
    )(o_da, y_rw, bonus, g_out, rest, rest, x, w_da, w_rw, w_mix, gn_g, gn_b, ln_g, ln_b)


def _mem_kv_kernel(mem_ref, g_ref, b_ref, w_ref, ck_ref, cv_ref):
    m = _layer_norm_rows(mem_ref[0], g_ref[...], b_ref[...])
    kv = jnp.dot(m.astype(BF16), w_ref[...], preferred_element_type=F32)
    ck_ref[0] = kv[:, :D_MODEL].astype(BF16)
    cv_ref[0] = kv[:, D_MODEL:].astype(BF16)


def _mem_kv(mem, g, b, w_ckv):
    bsz, m, d = mem.shape
    full = lambda a: pl.BlockSpec(a.shape, lambda i: (0,) * a.ndim)
    blk = pl.BlockSpec((1, m, d), lambda i: (i, 0, 0))
    return pl.pallas_call(
        _mem_kv_kernel,
        grid=(bsz,),
        in_specs=[blk, full(g), full(b), full(w_ckv)],
        out_specs=[blk, blk],
        out_shape=[jax.ShapeDtypeStruct((bsz, m, d), BF16)] * 2,
        compiler_params=_cparams(("parallel",)),
        name="mem_kv",
    )(mem, g, b, w_ckv)


def _cross_kernel(x_ref, ck_ref, cv_ref, wq_ref, wo_ref, lng_ref, lnb_ref, o_ref, *, alpha):
    x = x_ref[...]
    cq = jnp.dot(x.astype(BF16), wq_ref[...], preferred_element_type=F32)
    cq = (cq * (CA_HD ** -0.5)).astype(BF16)
    outs = []
    for h in range(CA_HEADS):
        sl = slice(h * CA_HD, (h + 1) * CA_HD)
        s = lax.dot_general(cq[:, sl], ck_ref[0, :, sl], NT_DIMS, preferred_element_type=F32)
        s = s - jnp.max(s, axis=-1, keepdims=True)
        p = jnp.exp(s)
        p = p / jnp.sum(p, axis=-1, keepdims=True)
        outs.append(jnp.dot(p.astype(BF16), cv_ref[0, :, sl], preferred_element_type=F32))
    co = jnp.dot(jnp.concatenate(outs, axis=1).astype(BF16), wo_ref[...], preferred_element_type=F32)
    o_ref[...] = _layer_norm_rows(alpha * x + co, lng_ref[...], lnb_ref[...])


def _cross_attention(x, ck, cv, w_cq, w_co, ln_g, ln_b, seq, alpha, tm=512):
    t, d = x.shape
    tiles_per_seq = seq // tm
    full = lambda a: pl.BlockSpec(a.shape, lambda i: (0,) * a.ndim)
    rowb = pl.BlockSpec((tm, d), lambda i: (i, 0))
    memb = pl.BlockSpec((1,) + ck.shape[1:], lambda i: (i // tiles_per_seq, 0, 0))
    return pl.pallas_call(
        functools.partial(_cross_kernel, alpha=alpha),
        grid=(t // tm,),
        in_specs=[rowb, memb, memb, full(w_cq), full(w_co), full(ln_g), full(ln_b)],
        out_specs=rowb,
        out_shape=jax.ShapeDtypeStruct((t, d), F32),
        compiler_params=_cparams(("parallel",)),
        name="cross_attention_ln2",
    )(x, ck, cv, w_cq, w_co, ln_g, ln_b)


def _ffn_kernel(x_ref, win_ref, wout_ref, lng_ref, lnb_ref, o_ref, *, alpha, tf):
    x = x_ref[...]
    xb = x.astype(BF16)
    acc = alpha * x
    for j in range(D_FF // tf):
        gate = jnp.dot(xb, win_ref[:, j * tf:(j + 1) * tf], preferred_element_type=F32)
        up = jnp.dot(xb, win_ref[:, D_FF + j * tf:D_FF + (j + 1) * tf], preferred_element_type=F32)
        h = (gate * jax.nn.sigmoid(gate) * up).astype(BF16)
        acc = acc + jnp.dot(h, wout_ref[j * tf:(j + 1) * tf, :], preferred_element_type=F32)
    o_ref[...] = _layer_norm_rows(acc, lng_ref[...], lnb_ref[...])


def _ffn(x, w_in, w_out, ln_g, ln_b, alpha, tm=512, tf=1408):
    t, d = x.shape
    rowb = pl.BlockSpec((tm, d), lambda i: (i, 0))
    return pl.pallas_call(
        functools.partial(_ffn_kernel, alpha=alpha, tf=tf),
        grid=(t // tm,),
        in_specs=[rowb, _resident(w_in), _resident(w_out), _resident(ln_g), _resident(ln_b)],
        out_specs=rowb,
        out_shape=jax.ShapeDtypeStruct((t, d), F32),
        compiler_params=_cparams(("parallel",)),
        name="swiglu_ffn_ln3",
    )(x, w_in, w_out, ln_g, ln_b)


def _pad_cols(a, width):
    return jnp.pad(a, ((0, 0), (0, width - a.shape[1])))


def _pad_rows(a, height):
    return jnp.pad(a, ((0, height - a.shape[0]), (0, 0)))


def kernel(x, mem, rel_bias, w_in, shift_mu, lambda_q1, lambda_k1, lambda_q2, lambda_k2, da_subln_g, w_da_proj, rw_w0, rw_w2, rw_a0, rw_a2, rw_g2, rw_k_k, rw_k_a, rw_r_k, rw_lnx_g, rw_lnx_b, w_rw_proj, w_mix_out, ln1_g, ln1_b, mem_ln_g, mem_ln_b, w_cq, w_ckv, w_co, ln2_g, ln2_b, w_ffn_in, w_ffn_out, ln3_g, ln3_b):
    bsz, seq, d = x.shape
    depth = w_in.shape[0]
    alpha = (2.0 * depth) ** 0.25
    t = bsz * seq
    row = lambda a: a.reshape(1, -1).astype(F32)
    bias_tiles = _attn_flat_bias(rel_bias)
    xf = x.reshape(t, d)

    qkv_w = 3 * D_MODEL
    rw0 = qkv_w
    lo0 = rw0 + 3 * D_MODEL
    gate0 = lo0 + DECAY_LORA + AAA_LORA + GATE_LORA

    for l in range(depth):
        w = w_in[l]
        w_qt = w[:, :D_MODEL].T.astype(BF16)
        w_k = w[:, D_MODEL:2 * D_MODEL].astype(BF16)
        w_vt = w[:, 2 * D_MODEL:qkv_w].T.astype(BF16)
        w_rest = jnp.concatenate([
            w[:, rw0:lo0], w[:, gate0:gate0 + 2 * D_MODEL],
            _pad_cols(w[:, lo0:lo0 + DECAY_LORA], 128),
            _pad_cols(w[:, lo0 + DECAY_LORA:lo0 + DECAY_LORA + AAA_LORA], 128),
            _pad_cols(w[:, lo0 + DECAY_LORA + AAA_LORA:gate0], 256)], axis=1).astype(BF16)
        mu = shift_mu[l].astype(F32)
        mu_rkv = mu[:3 * D_MODEL].reshape(3, D_MODEL)
        o1 = 3 * D_MODEL
        mu_lo = jnp.concatenate([
            _pad_cols(mu[None, o1:o1 + DECAY_LORA], 128),
            _pad_cols(mu[None, o1 + DECAY_LORA:o1 + DECAY_LORA + AAA_LORA], 128),
            _pad_cols(mu[None, o1 + DECAY_LORA + AAA_LORA:], 256)], axis=1)
        w2 = _pad_rows(rw_w2[l], 128).astype(BF16)
        a2 = _pad_rows(rw_a2[l], 128).astype(BF16)
        g2 = _pad_rows(rw_g2[l], 256).astype(BF16)

        qt, kda, vt = _qkv_proj(xf, w_qt, w_k, w_vt, bsz, seq)
        rest = _matmul(xf, w_rest, F32, 512, 1408, "in_proj_rest")

        lam_init = 0.8 - 0.6 * math.exp(-0.3 * l)
        lamv = jnp.stack([lambda_q1[l], lambda_k1[l], lambda_q2[l], lambda_k2[l]]).astype(F32)
        o_da = _diff_attention_flat(qt, kda.reshape(bsz, seq, d), vt, lamv, bias_tiles, row(da_subln_g[l]),
                                    lam_init)

        y_rw, bon, g_out = _rw_chunks(
            rest.reshape(bsz, seq, rest.shape[1]), mu_rkv, mu_lo, row(rw_w0[l]), w2, row(rw_a0[l]),
            a2, g2, row(rw_k_k[l]), row(rw_k_a[l]), row(rw_r_k[l]))

        x1 = _mix(o_da.reshape(t, d), y_rw.reshape(t, d), bon.reshape(t, d), g_out.reshape(t, d), rest, xf,
                  w_da_proj[l].astype(BF16), w_rw_proj[l].astype(BF16), w_mix_out[l].astype(BF16),
                  row(rw_lnx_g[l]), row(rw_lnx_b[l]), row(ln1_g[l]), row(ln1_b[l]), alpha)

        ck, cv = _mem_kv(mem, row(mem_ln_g[l]), row(mem_ln_b[l]), w_ckv[l].astype(BF16))
        x2 = _cross_attention(x1, ck, cv, w_cq[l].astype(BF16), w_co[l].astype(BF16),
                              row(ln2_g[l]), row(ln2_b[l]), seq, alpha)

        xf = _ffn(x2, w_ffn_in[l].astype(BF16), w_ffn_out[l].astype(BF16),
                  row(ln3_g[l]), row(ln3_b[l]), alpha)
    return xf.reshape(bsz, seq, d)
```

```python
import functools
import math

import jax
import jax.numpy as jnp
from jax import lax
from jax.experimental import pallas as pl
from jax.experimental.pallas import tpu as pltpu

F32 = jnp.float32
BF16 = jnp.bfloat16

D_MODEL = 1024
DA_HEADS = 8
DA_HD = 64
NUM_BUCKETS = 32
MAX_DISTANCE = 128
RW_HD = 64
RW_HEADS = D_MODEL // RW_HD
DECAY_LORA = 64
AAA_LORA = 64
GATE_LORA = 160
RW_LNX_EPS = 64e-5
MEM_TOKENS = 256
CA_HEADS = 4
CA_HD = D_MODEL // CA_HEADS
D_FF = 2816
LN_EPS = 1e-5
NEG_BIG = -1e30

LOG2E = math.log2(math.e)
ATT_HEADS_PER_STEP = 2
ATT_QT = 512
ATT_KB = 256
ATT_BIAS_TYPES = 4
ATT_ROTATE = 3
ATT_ONES_ROWS = 16
RW_CHUNK = 64
RW_STEP_CHUNKS = 2
VMEM_LIMIT = 56 * 1024 * 1024

NT_DIMS = (((1,), (1,)), ((), ()))
TN_DIMS = (((0,), (0,)), ((), ()))


def _cparams(sem, vmem=VMEM_LIMIT):
    return pltpu.CompilerParams(dimension_semantics=sem, vmem_limit_bytes=vmem)


def _split2(a):
    hi = a.astype(BF16)
    lo = (a - hi.astype(F32)).astype(BF16)
    return hi, lo


def _dot3(a, b, dims=(((1,), (0,)), ((), ()))):
    ah, al = _split2(a)
    bh, bl = _split2(b)
    d = functools.partial(lax.dot_general, dimension_numbers=dims, preferred_element_type=F32)
    return d(ah, bh) + (d(ah, bl) + d(al, bh))


def _layer_norm_rows(z, g, b):
    mu = jnp.mean(z, axis=-1, keepdims=True)
    zc = z - mu
    var = jnp.mean(zc * zc, axis=-1, keepdims=True)
    return zc * lax.rsqrt(var + LN_EPS) * g + b


def _resident(a):
    return pl.BlockSpec(a.shape, lambda *_: (0,) * a.ndim, pipeline_mode=pl.Buffered(1))


def _mm_kernel(x_ref, w_ref, o_ref, *, tn):
    xb = x_ref[...].astype(BF16)
    for j in range(w_ref.shape[1] // tn):
        cols = slice(j * tn, (j + 1) * tn)
        o_ref[:, cols] = jnp.dot(xb, w_ref[:, cols], preferred_element_type=F32).astype(o_ref.dtype)


def _matmul(x, w, out_dtype, tm, tn, name):
    t, k = x.shape
    n = w.shape[1]
    return pl.pallas_call(
        functools.partial(_mm_kernel, tn=tn),
        grid=(t // tm,),
        in_specs=[pl.BlockSpec((tm, k), lambda i: (i, 0)), _resident(w)],
        out_specs=pl.BlockSpec((tm, n), lambda i: (i, 0)),
        out_shape=jax.ShapeDtypeStruct((t, n), out_dtype),
        compiler_params=_cparams(("parallel",)),
        name=name,
    )(x, w)


def _qkv_kernel(x_ref, wqt_ref, wk_ref, wvt_ref, qt_ref, k_ref, vt_ref):
    xb = x_ref[...].astype(BF16)
    k_ref[...] = jnp.dot(xb, wk_ref[...], preferred_element_type=F32).astype(BF16)
    qt = lax.dot_general(wqt_ref[...], xb, NT_DIMS, preferred_element_type=F32)
    qt_ref[0] = (qt * (DA_HD ** -0.5 * LOG2E)).astype(BF16)
    vt_ref[0] = lax.dot_general(wvt_ref[...], xb, NT_DIMS, preferred_element_type=F32).astype(BF16)


def _qkv_proj(x, wqt, wk, wvt, bsz, seq, tm=512):
    t, d = x.shape
    tiles_per_seq = seq // tm
    full = lambda a: pl.BlockSpec(a.shape, lambda i: (0,) * a.ndim)
    tblk = pl.BlockSpec((1, d, tm), lambda i: (i // tiles_per_seq, 0, i % tiles_per_seq))
    rowb = pl.BlockSpec((tm, d), lambda i: (i, 0))
    tshape = jax.ShapeDtypeStruct((bsz, d, seq), BF16)
    return pl.pallas_call(
        _qkv_kernel,
        grid=(t // tm,),
        in_specs=[rowb, full(wqt), full(wk), full(wvt)],
        out_specs=[tblk, rowb, tblk],
        out_shape=[tshape, jax.ShapeDtypeStruct((t, d), BF16), tshape],
        compiler_params=_cparams(("parallel",)),
        name="in_proj_qkv",
    )(x, wqt, wk, wvt)


def _t5_bucket(rel):
    n = jnp.maximum(rel, 0)
    max_exact = NUM_BUCKETS // 2
    nf = jnp.maximum(n, 1).astype(jnp.float32)
    large = max_exact + jnp.floor(jnp.log(nf / max_exact) / math.log(MAX_DISTANCE / max_exact)
                                  * (NUM_BUCKETS - max_exact)).astype(jnp.int32)
    large = jnp.minimum(large, NUM_BUCKETS - 1)
    return jnp.where(n < max_exact, n, large)


def _col_max(s):
    while s.shape[0] > 8:
        half = s.shape[0] // 2
        s = jnp.maximum(s[:half], s[half:])
    return jnp.max(s, axis=0, keepdims=True)


def _attn_unit_tables(seq):
    kb_per_qt = ATT_QT // ATT_KB
    qts, kbs, types = [], [], []
    for qt in range(seq // ATT_QT):
        nkb = (qt + 1) * kb_per_qt
        for kb in range(nkb):
            from_end = nkb - 1 - kb
            qts.append(qt)
            kbs.append(kb)
            types.append(ATT_BIAS_TYPES - 1 - from_end if from_end < ATT_BIAS_TYPES - 1 else 0)
    assert len(qts) % ATT_ROTATE == 0
    nunits = len(qts)
    for _ in range(ATT_ROTATE - 1):
        qts.append(0)
        kbs.append(0)
        types.append(0)
    as_i32 = lambda v: jnp.asarray(v, jnp.int32)
    return nunits, as_i32(qts), as_i32(kbs), as_i32(types)


def _attn_flat_bias(rel_bias):
    tab = rel_bias.astype(F32) - rel_bias[NUM_BUCKETS - 1].astype(F32)[None, :]
    base = jnp.arange(ATT_QT)[None, :] - jnp.arange(ATT_KB)[:, None]

    def tile(key_offset):
        rel = base - key_offset
        onehot = (_t5_bucket(rel)[:, :, None] == jnp.arange(NUM_BUCKETS)[None, None, :]).astype(F32)
        vals = jnp.einsum('kqb,bh->hkq', onehot, tab, precision=lax.Precision.HIGHEST)
        return jnp.where((rel >= 0)[None], vals, NEG_BIG)

    zeros = jnp.zeros((rel_bias.shape[1], ATT_KB, ATT_QT), F32)
    return jnp.stack([zeros, tile(-ATT_KB), tile(0), tile(ATT_KB)], axis=1) * LOG2E


def _attn_flat_kernel(tq_ref, tk_ref, tt_ref, lamv_ref, qt_ref, k_ref, vt_ref, bias_ref, g_ref, o_ref,
                      m_sc, acc_sc, sa_sc, sb_sc, sc_sc, ma_sc, mb_sc, mc_sc, *, lam_init, nunits):
    hw = 2 * DA_HD
    heads = range(ATT_HEADS_PER_STEP)
    hrows = [slice(hh * hw, (hh + 1) * hw) for hh in heads]
    ntiles = m_sc.shape[0]

    lv = lamv_ref[...]
    lam = (jnp.exp(jnp.sum(lv[0:1] * lv[1:2], axis=-1, keepdims=True))
           - jnp.exp(jnp.sum(lv[2:3] * lv[3:4], axis=-1, keepdims=True)) + lam_init)

    m_sc[...] = jnp.full(m_sc.shape, NEG_BIG, F32)
    acc_sc[...] = jnp.zeros(acc_sc.shape, F32)
    rowi = lax.broadcasted_iota(jnp.int32, (hw, ATT_QT), 0)

    def fill(buf, mbuf, u):
        q0 = pl.multiple_of(tq_ref[u] * ATT_QT, ATT_QT)
        k0 = pl.multiple_of(tk_ref[u] * ATT_KB, ATT_KB)
        ty = tt_ref[u]
        s = []
        for hh in heads:
            q = qt_ref[0, hrows[hh], pl.ds(q0, ATT_QT)]
            zero = jnp.zeros_like(q)
            qq = jnp.concatenate([jnp.where(rowi < DA_HD, q, zero), jnp.where(rowi >= DA_HD, q, zero)], axis=1)
            s.append(jnp.dot(k_ref[0, pl.ds(k0, ATT_KB), hrows[hh]], qq, preferred_element_type=F32))
        for hh in heads:
            b = bias_ref[hh, ty]
            sb = s[hh] + jnp.concatenate([b, b], axis=1)
            buf[hh] = sb
            mbuf[hh] = _col_max(sb)

    def drain(buf, mbuf, u):
        t = tq_ref[u]
        k0 = pl.multiple_of(tk_ref[u] * ATT_KB, ATT_KB)
        m_prev = [m_sc[t, hh] for hh in heads]
        m_new = [jnp.maximum(m_prev[hh], mbuf[hh]) for hh in heads]
        alpha = [jnp.exp2(m_prev[hh] - m_new[hh]) for hh in heads]
        p = [jnp.exp2((buf[hh] - m_new[hh]).astype(BF16)) for hh in heads]
        ones = jnp.ones((ATT_ONES_ROWS, ATT_KB), BF16)
        pv = [jnp.dot(jnp.concatenate([vt_ref[0, hrows[hh], pl.ds(k0, ATT_KB)], ones], axis=0), p[hh],
                      preferred_element_type=F32) for hh in heads]
        for hh in heads:
            acc_sc[t, hh] = alpha[hh] * acc_sc[t, hh] + pv[hh]
            m_sc[t, hh] = m_new[hh]

    fill(sa_sc, ma_sc, 0)
    fill(sb_sc, mb_sc, 1)

    def trip(i, carry):
        u = i * ATT_ROTATE
        fill(sc_sc, mc_sc, u + 2)
        drain(sa_sc, ma_sc, u)
        fill(sa_sc, ma_sc, u + 3)
        drain(sb_sc, mb_sc, u + 1)
        fill(sb_sc, mb_sc, u + 4)
        drain(sc_sc, mc_sc, u + 2)
        return carry

    lax.fori_loop(0, nunits // ATT_ROTATE, trip, 0)

    def finish(t, carry):
        q0 = pl.multiple_of(t * ATT_QT, ATT_QT)
        for hh in heads:
            a = acc_sc[t, hh]
            acc = a[0:hw, :] * (1.0 / a[hw:hw + 1, :])
            ot = acc[:, 0:ATT_QT] - lam * acc[:, ATT_QT:2 * ATT_QT]
            ms = jnp.mean(ot * ot, axis=0, keepdims=True)
            ot = ot * (lax.rsqrt(ms + LN_EPS) * (1.0 - lam_init))
            o_ref[0, pl.ds(q0, ATT_QT), hrows[hh]] = (ot.T * g_ref[...]).astype(o_ref.dtype)
        return carry

    lax.fori_loop(0, ntiles, finish, 0)


def _diff_attention_flat(qt, k, vt, lamv, bias, subln_g, lam_init):
    b, s, _ = k.shape
    nh = ATT_HEADS_PER_STEP
    hw = 2 * DA_HD
    nunits, tq, tk, tt = _attn_unit_tables(s)
    score = pltpu.VMEM((nh, ATT_KB, 2 * ATT_QT), F32)
    smax = pltpu.VMEM((nh, 1, 2 * ATT_QT), F32)
    grid_spec = pltpu.PrefetchScalarGridSpec(
        num_scalar_prefetch=3,
        grid=(b, DA_HEADS // nh),
        in_specs=[
            pl.BlockSpec((4, DA_HD), lambda bi, hi, *_: (0, 0)),
            pl.BlockSpec((1, nh * hw, s), lambda bi, hi, *_: (bi, hi, 0)),
            pl.BlockSpec((1, s, nh * hw), lambda bi, hi, *_: (bi, 0, hi)),
            pl.BlockSpec((1, nh * hw, s), lambda bi, hi, *_: (bi, hi, 0)),
            pl.BlockSpec((nh, ATT_BIAS_TYPES, ATT_KB, ATT_QT), lambda bi, hi, *_: (hi, 0, 0, 0)),
            pl.BlockSpec((1, hw), lambda bi, hi, *_: (0, 0)),
        ],
        out_specs=pl.BlockSpec((1, s, nh * hw), lambda bi, hi, *_: (bi, 0, hi)),
        scratch_shapes=[pltpu.VMEM((s // ATT_QT, nh, 1, 2 * ATT_QT), F32),
                        pltpu.VMEM((s // ATT_QT, nh, hw + ATT_ONES_ROWS, 2 * ATT_QT), F32),
                        score, score, score, smax, smax, smax],
    )
    return pl.pallas_call(
        functools.partial(_attn_flat_kernel, lam_init=lam_init, nunits=nunits),
        grid_spec=grid_spec,
        out_shape=jax.ShapeDtypeStruct((b, s, DA_HEADS * hw), BF16),
        compiler_params=_cparams(("parallel", "parallel")),
        name="diff_attention",
    )(tq, tk, tt, lamv, qt, k, vt, bias, subln_g)


def _dot(a, b, dims=(((1,), (0,)), ((), ()))):
    return lax.dot_general(a.astype(BF16), b.astype(BF16), dims, preferred_element_type=F32)


def _dot_split_lhs(a, b_bf16):
    ah, al = _split2(a)
    return (jnp.dot(ah, b_bf16, preferred_element_type=F32) + jnp.dot(al, b_bf16, preferred_element_type=F32))


def _dot_split_lhs_rhs(a_bf16, b):
    bh, bl = _split2(b)
    return (jnp.dot(a_bf16, bh, preferred_element_type=F32) + jnp.dot(a_bf16, bl, preferred_element_type=F32))


def _pair_consts():
    c, n = RW_CHUNK, RW_HD
    lane = lax.broadcasted_iota(jnp.int32, (c, 2 * n), 1)
    row = lax.broadcasted_iota(jnp.int32, (2 * c, 2 * c), 0)
    col = lax.broadcasted_iota(jnp.int32, (2 * c, 2 * c), 1)
    same_head = (row // n) == (col // n)
    return lane < n, same_head.astype(BF16)


def _expand(x2, head_a):
    zero = jnp.zeros_like(x2)
    return jnp.concatenate([jnp.where(head_a, x2, zero), jnp.where(head_a, zero, x2)], axis=0)


def _compact(xm):
    return xm[0:RW_CHUNK] + xm[RW_CHUNK:2 * RW_CHUNK]


def _rw_chunk_kernel(r_ref, k_ref, v_ref, lo_ref, mu_ref, mulo_ref, w0_ref, w2_ref,
                     a0_ref, a2_ref, g2_ref, kk_ref, ka_ref, rk_ref,
                     y_ref, bon_ref, g_ref,
                     z_sc, pt_sc, qt_sc, rh_sc, y0_sc, prev_sc, *, nsteps):
    c, n = RW_CHUNK, RW_HD
    pw = 2 * n
    pairs = range(z_sc.shape[0])
    lns = [slice(hp * pw, (hp + 1) * pw) for hp in pairs]
    rows = RW_STEP_CHUNKS * c
    rss = [slice(j * c, (j + 1) * c) for j in range(RW_STEP_CHUNKS)]
    head_a, seg_ones = _pair_consts()

    @pl.when(pl.program_id(1) == 0)
    def _():
        z_sc[...] = jnp.zeros(z_sc.shape, F32)
        pt_sc[...] = jnp.zeros(pt_sc.shape, F32)
        qt_sc[...] = jnp.zeros(qt_sc.shape, F32)
        rh_sc[...] = jnp.zeros(rh_sc.shape, rh_sc.dtype)
        y0_sc[...] = jnp.zeros(y0_sc.shape, F32)
        prev_sc[...] = jnp.zeros(prev_sc.shape, F32)

    z = [z_sc[hp] for hp in pairs]
    for rs in rss:
        for hp in pairs:
            y_ref[0, rs, lns[hp]] = _dot(rh_sc[rs, lns[hp]], z[hp]) + y0_sc[rs, lns[hp]]
        z = [_dot3(_expand(pt_sc[rs, lns[hp]], head_a), z[hp]) + _expand(qt_sc[rs, lns[hp]], head_a)
             for hp in pairs]
    for hp in pairs:
        z_sc[hp] = z[hp]

    row = lax.broadcasted_iota(jnp.int32, (2 * c, 2 * c), 0)
    col = lax.broadcasted_iota(jnp.int32, (2 * c, 2 * c), 1)
    strict = (row % c) > (col % c)
    incl = (row % c) >= (col % c)
    trow = lax.broadcasted_iota(jnp.int32, (c, c), 0)
    tcol = lax.broadcasted_iota(jnp.int32, (c, c), 1)
    tri_ones = (trow >= tcol).astype(BF16)
    diag2 = lax.broadcasted_iota(jnp.int32, (c, pw), 0) == (lax.broadcasted_iota(jnp.int32, (c, pw), 1) % n)
    d = functools.partial(jnp.dot, preferred_element_type=F32)

    def cumsum(lw2):
        return _dot_split_lhs_rhs(tri_ones, lw2)

    ci = pl.program_id(1)
    trow1 = lax.broadcasted_iota(jnp.int32, (rows, 1), 0)
    keep_prev = ci < nsteps - 1

    def shifted(cur_ref, slot, mu):
        p = cur_ref[0]
        w = p.shape[1]
        before = prev_sc[slot:slot + 1, 0:w]
        pm1 = jnp.where(trow1 == 0, before, pltpu.roll(p, 1, 0))
        prev_sc[slot:slot + 1, 0:w] = jnp.where(keep_prev, p[rows - 1:rows, :], before)
        return p + mu * (pm1 - p)

    r = shifted(r_ref, 0, mu_ref[0:1])
    k = shifted(k_ref, 1, mu_ref[1:2])
    v = shifted(v_ref, 2, mu_ref[2:3])
    lo = shifted(lo_ref, 3, mulo_ref[...])

    nz = -(w0_ref[...] + d(jnp.tanh(lo[:, 0:128]).astype(BF16), w2_ref[...]))
    softplus = jnp.maximum(nz, 0.0) + jnp.log(1.0 + jnp.exp(-jnp.abs(nz)))
    lw = -jnp.exp(-softplus - 0.5)
    a_ic = jax.nn.sigmoid(a0_ref[...] + d(lo[:, 128:256].astype(BF16), a2_ref[...]))
    g_ref[0] = d(jax.nn.sigmoid(lo[:, 256:512]).astype(BF16), g2_ref[...])
    kk = k * kk_ref[...]
    km = k * (1.0 + (a_ic - 1.0) * ka_ref[...])

    def intra(units):
        gpairs = range(len(units))
        cum = [cumsum(lw[rs, ln]) for rs, ln in units]
        nrm = [jnp.sqrt(_dot(kk[rs, ln] * kk[rs, ln], seg_ones)) for rs, ln in units]
        big, atm, vm, bdm, kdm = [], [], [], [], []
        for hp, (rs, ln) in enumerate(units):
            lw2, cm = lw[rs, ln], cum[hp]
            g_inv = jnp.exp(-cm)
            g_rest = jnp.exp(cm[c - 1:c, :] - cm)
            kkn = kk[rs, ln] / jnp.maximum(nrm[hp], 1e-12)
            b2 = kkn * a_ic[rs, ln]
            km2 = km[rs, ln]
            a_m = _expand(-kkn * jnp.exp(cm - lw2), head_a)
            r_m = _expand(r[rs, ln] * jnp.exp(cm), head_a)
            b_m = _expand(b2 * g_inv, head_a)
            k_m = _expand(km2 * g_inv, head_a)
            atm.append(a_m)
            bdm.append(_expand(b2 * g_rest, head_a))
            kdm.append(_expand(km2 * g_rest, head_a))
            vm.append(_expand(v[rs, ln], head_a))
            big.append(_dot(jnp.concatenate([a_m, r_m], axis=0), jnp.concatenate([b_m, k_m], axis=0), NT_DIMS))
        lmat = [jnp.where(strict, bg[0:pw, 0:pw], 0.0) for bg in big]
        m_rb = [jnp.where(incl, bg[pw:2 * pw, 0:pw], 0.0) for bg in big]
        mv = [_dot(jnp.concatenate([jnp.where(strict, bg[0:pw, pw:2 * pw], 0.0),
                                    jnp.where(incl, bg[pw:2 * pw, pw:2 * pw], 0.0)], axis=0), vm[hp])
              for hp, bg in enumerate(big)]
        x = [jnp.concatenate([atm[hp], mv[hp][0:pw]], axis=1) for hp in gpairs]
        lp = lmat
        for it in range(6):
            x = [x[hp] + _dot(lp[hp], x[hp]) for hp in gpairs]
            if it < 5:
                lp = [_dot(m, m) for m in lp]
        rb = [_dot(m_rb[hp], x[hp]) for hp in gpairs]
        bx = [_dot(bdm[hp], x[hp], TN_DIMS) for hp in gpairs]
        kv = [_dot(kdm[hp], vm[hp], TN_DIMS) for hp in gpairs]
        for hp, (rs, ln) in enumerate(units):
            g_last = jnp.exp(cum[hp][c - 1:c, :])
            pt_sc[rs, ln] = _compact(bx[hp][:, 0:pw]) + jnp.where(diag2, g_last, 0.0)
            qt_sc[rs, ln] = _compact(bx[hp][:, pw:2 * pw] + kv[hp])
            rh_sc[rs, ln] = (r[rs, ln] * jnp.exp(cum[hp]) + _compact(rb[hp][:, 0:pw])).astype(rh_sc.dtype)
            y0_sc[rs, ln] = _compact(rb[hp][:, pw:2 * pw] + mv[hp][pw:2 * pw])
            bon_ref[0, rs, ln] = _dot(r[rs, ln] * km[rs, ln] * rk_ref[:, ln], seg_ones) * v[rs, ln]

    intra([(rs, ln) for rs in rss for ln in lns])


def _rw_chunks(rest, mu_rkv, mu_lo, w0, w2, a0, a2, g2, k_k, k_a, r_k):
    b, s, _ = rest.shape
    d = D_MODEL
    c = RW_STEP_CHUNKS * RW_CHUNK
    nc = s // c
    lora_w = mu_lo.shape[1]
    cur = lambda col, w: pl.BlockSpec((1, c, w), lambda bi, ci: (bi, jnp.minimum(ci, nc - 1), col))
    prev = pl.BlockSpec((1, c, d), lambda bi, ci: (bi, jnp.maximum(ci - 1, 0), 0))
    full = lambda a: pl.BlockSpec(a.shape, lambda bi, ci: (0,) * a.ndim)
    params = (mu_rkv, mu_lo, w0, w2, a0, a2, g2, k_k, k_a, r_k)
    out = jax.ShapeDtypeStruct((b, s, d), F32)
    lora_col = (3 * d + 2 * d) // lora_w
    return pl.pallas_call(
        functools.partial(_rw_chunk_kernel, nsteps=nc),
        grid=(b, nc + 1),
        in_specs=[cur(0, d), cur(1, d), cur(2, d), cur(lora_col, lora_w)] + [full(a) for a in params],
        out_specs=[prev, cur(0, d), cur(0, d)],
        out_shape=[out, out, out],
        scratch_shapes=[pltpu.VMEM((d // (2 * RW_HD), 2 * RW_HD, 2 * RW_HD), F32),
                        pltpu.VMEM((c, d), F32), pltpu.VMEM((c, d), F32),
                        pltpu.VMEM((c, d), BF16), pltpu.VMEM((c, d), F32),
                        pltpu.VMEM((4, d), F32)],
        compiler_params=_cparams(("parallel", "arbitrary")),
        name="rwkv_chunks",
    )(rest, rest, rest, rest, *params)


def _mix_kernel(oda_ref, yrw_ref, bon_ref, gout_ref, ga_ref, gb_ref, x_ref, wda_ref, wrw_ref, wmix_ref,
                gng_ref, gnb_ref, lng_ref, lnb_ref, o_ref, *, alpha):
    y_da = jnp.dot(oda_ref[...], wda_ref[...], preferred_element_type=F32)
    _, seg_ones = _pair_consts()
    parts = []
    for hp in range(RW_HEADS // 2):
        ln = slice(hp * 2 * RW_HD, (hp + 1) * 2 * RW_HD)
        y = yrw_ref[:, ln]
        mu = _dot_split_lhs(y, seg_ones) * (1.0 / RW_HD)
        yc = y - mu
        var = _dot_split_lhs(yc * yc, seg_ones) * (1.0 / RW_HD)
        parts.append(yc * lax.rsqrt(var + RW_LNX_EPS))
    yn = jnp.concatenate(parts, axis=1) * gng_ref[...] + gnb_ref[...] + bon_ref[...]
    y_rw = jnp.dot((yn * gout_ref[...]).astype(BF16), wrw_ref[...], preferred_element_type=F32)
    mixed = jax.nn.sigmoid(ga_ref[...]) * y_da + jax.nn.sigmoid(gb_ref[...]) * y_rw
    z = alpha * x_ref[...] + jnp.dot(mixed.astype(BF16), wmix_ref[...], preferred_element_type=F32)
    o_ref[...] = _layer_norm_rows(z, lng_ref[...], lnb_ref[...])


def _mix(o_da, y_rw, bonus, g_out, rest, x, w_da, w_rw, w_mix, gn_g, gn_b, ln_g, ln_b, alpha, tm=512):
    t, d = x.shape
    rowb = lambda c: pl.BlockSpec((tm, d), lambda i: (i, c))
    full = lambda a: pl.BlockSpec(a.shape, lambda i: (0,) * a.ndim)
    return pl.pallas_call(
        functools.partial(_mix_kernel, alpha=alpha),
        grid=(t // tm,),
        in_specs=[rowb(0), rowb(0), rowb(0), rowb(0), rowb(3), rowb(4), rowb(0),
                  full(w_da), full(w_rw), full(w_mix), full(gn_g), full(gn_b), full(ln_g), full(ln_b)],
        out_specs=rowb(0),
        out_shape=jax.ShapeDtypeStruct((t, d), F32),
        compiler_params=_cparams(("parallel",)),
        name="mix_merge_ln1",
    )(o_da, y_rw, bonus, g_out, rest, rest, x, w_da, w_rw, w_mix, gn_g, gn_b, ln_g, ln_b)


def _mem_kv_kernel(mem_ref, g_ref, b_ref, w_ref, ck_ref, cv_ref):
    m = _layer_norm_rows(mem_ref[0], g_ref[...], b_ref[...])
    kv = jnp.dot(m.astype(BF16), w_ref[...], preferred_element_type=F32)
    ck_ref[0] = kv[:, :D_MODEL].astype(BF16)
    cv_ref[0] = kv[:, D_MODEL:].astype(BF16)


def _mem_kv(mem, g, b, w_ckv):
    bsz, m, d = mem.shape
    full = lambda a: pl.BlockSpec(a.shape, lambda i: (0,) * a.ndim)
    blk = pl.BlockSpec((1, m, d), lambda i: (i, 0, 0))
    return pl.pallas_call(
        _mem_kv_kernel,
        grid=(bsz,),
        in_specs=[blk, full(g), full(b), full(w_ckv)],
        out_specs=[blk, blk],
        out_shape=[jax.ShapeDtypeStruct((bsz, m, d), BF16)] * 2,
        compiler_params=_cparams(("parallel",)),
        name="mem_kv",
    )(mem, g, b, w_ckv)


def _cross_kernel(x_ref, ck_ref, cv_ref, wq_ref, wo_ref, lng_ref, lnb_ref, o_ref, *, alpha):
    x = x_ref[...]
    cq = jnp.dot(x.astype(BF16), wq_ref[...], preferred_element_type=F32)
    cq = (cq * (CA_HD ** -0.5)).astype(BF16)
    outs = []
    for h in range(CA_HEADS):
        sl = slice(h * CA_HD, (h + 1) * CA_HD)
        s = lax.dot_general(cq[:, sl], ck_ref[0, :, sl], NT_DIMS, preferred_element_type=F32)
        s = s - jnp.max(s, axis=-1, keepdims=True)
        p = jnp.exp(s)
        p = p / jnp.sum(p, axis=-1, keepdims=True)
        outs.append(jnp.dot(p.astype(BF16), cv_ref[0, :, sl], preferred_element_type=F32))
    co = jnp.dot(jnp.concatenate(outs, axis=1).astype(BF16), wo_ref[...], preferred_element_type=F32)
    o_ref[...] = _layer_norm_rows(alpha * x + co, lng_ref[...], lnb_ref[...])


def _cross_attention(x, ck, cv, w_cq, w_co, ln_g, ln_b, seq, alpha, tm=512):
    t, d = x.shape
    tiles_per_seq = seq // tm
    full = lambda a: pl.BlockSpec(a.shape, lambda i: (0,) * a.ndim)
    rowb = pl.BlockSpec((tm, d), lambda i: (i, 0))
    memb = pl.BlockSpec((1,) + ck.shape[1:], lambda i: (i // tiles_per_seq, 0, 0))
    return pl.pallas_call(
        functools.partial(_cross_kernel, alpha=alpha),
        grid=(t // tm,),
        in_specs=[rowb, memb, memb, full(w_cq), full(w_co), full(ln_g), full(ln_b)],
        out_specs=rowb,
        out_shape=jax.ShapeDtypeStruct((t, d), F32),
        compiler_params=_cparams(("parallel",)),
        name="cross_attention_ln2",
    )(x, ck, cv, w_cq, w_co, ln_g, ln_b)


def _ffn_kernel(x_ref, win_ref, wout_ref, lng_ref, lnb_ref, o_ref, *, alpha, tf):
    x = x_ref[...]
    xb = x.astype(BF16)
    acc = alpha * x
    for j in range(D_FF // tf):
        gate = jnp.dot(xb, win_ref[:, j * tf:(j + 1) * tf], preferred_element_type=F32)
        up = jnp.dot(xb, win_ref[:, D_FF + j * tf:D_FF + (j + 1) * tf], preferred_element_type=F32)
        h = (gate * jax.nn.sigmoid(gate) * up).astype(BF16)
        acc = acc + jnp.dot(h, wout_ref[j * tf:(j + 1) * tf, :], preferred_element_type=F32)
    o_ref[...] = _layer_norm_rows(acc, lng_ref[...], lnb_ref[...])


def _ffn(x, w_in, w_out, ln_g, ln_b, alpha, tm=512, tf=1408):
    t, d = x.shape
    rowb = pl.BlockSpec((tm, d), lambda i: (i, 0))
    return pl.pallas_call(
        functools.partial(_ffn_kernel, alpha=alpha, tf=tf),
        grid=(t // tm,),
        in_specs=[rowb, _resident(w_in), _resident(w_out), _resident(ln_g), _resident(ln_b)],
        out_specs=rowb,
        out_shape=jax.ShapeDtypeStruct((t, d), F32),
        compiler_params=_cparams(("parallel",)),
        name="swiglu_ffn_ln3",
    )(x, w_in, w_out, ln_g, ln_b)


def _pad_cols(a, width):
    return jnp.pad(a, ((0, 0), (0, width - a.shape[1])))


def _pad_rows(a, height):
    return jnp.pad(a, ((0, height - a.shape[0]), (0, 0)))


def kernel(x, mem, rel_bias, w_in, shift_mu, lambda_q1, lambda_k1, lambda_q2, lambda_k2, da_subln_g, w_da_proj, rw_w0, rw_w2, rw_a0, rw_a2, rw_g2, rw_k_k, rw_k_a, rw_r_k, rw_lnx_g, rw_lnx_b, w_rw_proj, w_mix_out, ln1_g, ln1_b, mem_ln_g, mem_ln_b, w_cq, w_ckv, w_co, ln2_g, ln2_b, w_ffn_in, w_ffn_out, ln3_g, ln3_b):
    bsz, seq, d = x.shape
    depth = w_in.shape[0]
    alpha = (2.0 * depth) ** 0.25
    t = bsz * seq
    row = lambda a: a.reshape(1, -1).astype(F32)
    bias_tiles = _attn_flat_bias(rel_bias)
    xf = x.reshape(t, d)

    qkv_w = 3 * D_MODEL
    rw0 = qkv_w
    lo0 = rw0 + 3 * D_MODEL
    gate0 = lo0 + DECAY_LORA + AAA_LORA + GATE_LORA

    for l in range(depth):
        w = w_in[l]
        w_qt = w[:, :D_MODEL].T.astype(BF16)
        w_k = w[:, D_MODEL:2 * D_MODEL].astype(BF16)
        w_vt = w[:, 2 * D_MODEL:qkv_w].T.astype(BF16)
        w_rest = jnp.concatenate([
            w[:, rw0:lo0], w[:, gate0:gate0 + 2 * D_MODEL],
            _pad_cols(w[:, lo0:lo0 + DECAY_LORA], 128),
            _pad_cols(w[:, lo0 + DECAY_LORA:lo0 + DECAY_LORA + AAA_LORA], 128),
            _pad_cols(w[:, lo0 + DECAY_LORA + AAA_LORA:gate0], 256)], axis=1).astype(BF16)
        mu = shift_mu[l].astype(F32)
        mu_rkv = mu[:3 * D_MODEL].reshape(3, D_MODEL)
        o1 = 3 * D_MODEL
        mu_lo = jnp.concatenate([
            _pad_cols(mu[None, o1:o1 + DECAY_LORA], 128),
            _pad_cols(mu[None, o1 + DECAY_LORA:o1 + DECAY_LORA + AAA_LORA], 128),
            _pad_cols(mu[None, o1 + DECAY_LORA + AAA_LORA:], 256)], axis=1)
        w2 = _pad_rows(rw_w2[l], 128).astype(BF16)
        a2 = _pad_rows(rw_a2[l], 128).astype(BF16)
        g2 = _pad_rows(rw_g2[l], 256).astype(BF16)

        qt, kda, vt = _qkv_proj(xf, w_qt, w_k, w_vt, bsz, seq)
        rest = _matmul(xf, w_rest, F32, 512, 1408, "in_proj_rest")

        lam_init = 0.8 - 0.6 * math.exp(-0.3 * l)
        lamv = jnp.stack([lambda_q1[l], lambda_k1[l], lambda_q2[l], lambda_k2[l]]).astype(F32)
        o_da = _diff_attention_flat(qt, kda.reshape(bsz, seq, d), vt, lamv, bias_tiles, row(da_subln_g[l]),
                                    lam_init)

        y_rw, bon, g_out = _rw_chunks(
            rest.reshape(bsz, seq, rest.shape[1]), mu_rkv, mu_lo, row(rw_w0[l]), w2, row(rw_a0[l]),
            a2, g2, row(rw_k_k[l]), row(rw_k_a[l]), row(rw_r_k[l]))

        x1 = _mix(o_da.reshape(t, d), y_rw.reshape(t, d), bon.reshape(t, d), g_out.reshape(t, d), rest, xf,
                  w_da_proj[l].astype(BF16), w_rw_proj[l].astype(BF16), w_mix_out[l].astype(BF16),
                  row(rw_lnx_g[l]), row(rw_lnx_b[l]), row(ln1_g[l]), row(ln1_b[l]), alpha)

        ck, cv = _mem_kv(mem, row(mem_ln_g[l]), row(mem_ln_b[l]), w_ckv[l].astype(BF16))
        x2 = _cross_attention(x1, ck, cv, w_cq[l].astype(BF16), w_co[l].astype(BF16),
                              row(ln2_g[l]), row(ln2_b[l]), seq, alpha)

        xf = _ffn(x2, w_ffn_in[l].astype(BF16), w_ffn_out[l].astype(BF16),
                  row(ln3_g[l]), row(ln3_b[l]), alpha)
    return xf.reshape(bsz, seq, d)
```

```python
import functools
import math

import jax
import jax.numpy as jnp
from jax import lax
from jax.experimental import pallas as pl
from jax.experimental.pallas import tpu as pltpu

F32 = jnp.float32
BF16 = jnp.bfloat16

D_MODEL = 1024
DA_HEADS = 8
DA_HD = 64
NUM_BUCKETS = 32
MAX_DISTANCE = 128
RW_HD = 64
RW_HEADS = D_MODEL // RW_HD
DECAY_LORA = 64
AAA_LORA = 64
GATE_LORA = 160
RW_LNX_EPS = 64e-5
MEM_TOKENS = 256
CA_HEADS = 4
CA_HD = D_MODEL // CA_HEADS
D_FF = 2816
LN_EPS = 1e-5
NEG_BIG = -1e30

LOG2E = math.log2(math.e)
ATT_HEADS_PER_STEP = 2
ATT_QT = 512
ATT_KB = 256
ATT_BIAS_TYPES = 4
ATT_ROTATE = 3
ATT_ONES_ROWS = 16
RW_CHUNK = 64
RW_STEP_CHUNKS = 2
VMEM_LIMIT = 56 * 1024 * 1024

NT_DIMS = (((1,), (1,)), ((), ()))
TN_DIMS = (((0,), (0,)), ((), ()))


def _cparams(sem, vmem=VMEM_LIMIT):
    return pltpu.CompilerParams(dimension_semantics=sem, vmem_limit_bytes=vmem)


def _split2(a):
    hi = a.astype(BF16)
    lo = (a - hi.astype(F32)).astype(BF16)
    return hi, lo


def _dot3(a, b, dims=(((1,), (0,)), ((), ()))):
    ah, al = _split2(a)
    bh, bl = _split2(b)
    d = functools.partial(lax.dot_general, dimension_numbers=dims, preferred_element_type=F32)
    return d(ah, bh) + (d(ah, bl) + d(al, bh))


def _layer_norm_rows(z, g, b):
    mu = jnp.mean(z, axis=-1, keepdims=True)
    zc = z - mu
    var = jnp.mean(zc * zc, axis=-1, keepdims=True)
    return zc * lax.rsqrt(var + LN_EPS) * g + b


def _resident(a):
    return pl.BlockSpec(a.shape, lambda *_: (0,) * a.ndim, pipeline_mode=pl.Buffered(1))


def _mm_kernel(x_ref, w_ref, o_ref, *, tn):
    xb = x_ref[...].astype(BF16)
    for j in range(w_ref.shape[1] // tn):
        cols = slice(j * tn, (j + 1) * tn)
        o_ref[:, cols] = jnp.dot(xb, w_ref[:, cols], preferred_element_type=F32).astype(o_ref.dtype)


def _matmul(x, w, out_dtype, tm, tn, name):
    t, k = x.shape
    n = w.shape[1]
    return pl.pallas_call(
        functools.partial(_mm_kernel, tn=tn),
        grid=(t // tm,),
        in_specs=[pl.BlockSpec((tm, k), lambda i: (i, 0)), _resident(w)],
        out_specs=pl.BlockSpec((tm, n), lambda i: (i, 0)),
        out_shape=jax.ShapeDtypeStruct((t, n), out_dtype),
        compiler_params=_cparams(("parallel",)),
        name=name,
    )(x, w)


def _qkv_kernel(x_ref, wqt_ref, wk_ref, wvt_ref, qt_ref, k_ref, vt_ref):
    xb = x_ref[...].astype(BF16)
    k_ref[...] = jnp.dot(xb, wk_ref[...], preferred_element_type=F32).astype(BF16)
    qt = lax.dot_general(wqt_ref[...], xb, NT_DIMS, preferred_element_type=F32)
    qt_ref[0] = (qt * (DA_HD ** -0.5 * LOG2E)).astype(BF16)
    vt_ref[0] = lax.dot_general(wvt_ref[...], xb, NT_DIMS, preferred_element_type=F32).astype(BF16)


def _qkv_proj(x, wqt, wk, wvt, bsz, seq, tm=512):
    t, d = x.shape
    tiles_per_seq = seq // tm
    full = lambda a: pl.BlockSpec(a.shape, lambda i: (0,) * a.ndim)
    tblk = pl.BlockSpec((1, d, tm), lambda i: (i // tiles_per_seq, 0, i % tiles_per_seq))
    rowb = pl.BlockSpec((tm, d), lambda i: (i, 0))
    tshape = jax.ShapeDtypeStruct((bsz, d, seq), BF16)
    return pl.pallas_call(
        _qkv_kernel,
        grid=(t // tm,),
        in_specs=[rowb, full(wqt), full(wk), full(wvt)],
        out_specs=[tblk, rowb, tblk],
        out_shape=[tshape, jax.ShapeDtypeStruct((t, d), BF16), tshape],
        compiler_params=_cparams(("parallel",)),
        name="in_proj_qkv",
    )(x, wqt, wk, wvt)


def _t5_bucket(rel):
    n = jnp.maximum(rel, 0)
    max_exact = NUM_BUCKETS // 2
    nf = jnp.maximum(n, 1).astype(jnp.float32)
    large = max_exact + jnp.floor(jnp.log(nf / max_exact) / math.log(MAX_DISTANCE / max_exact)
                                  * (NUM_BUCKETS - max_exact)).astype(jnp.int32)
    large = jnp.minimum(large, NUM_BUCKETS - 1)
    return jnp.where(n < max_exact, n, large)


def _col_max(s):
    while s.shape[0] > 8:
        half = s.shape[0] // 2
        s = jnp.maximum(s[:half], s[half:])
    return jnp.max(s, axis=0, keepdims=True)


def _attn_unit_tables(seq):
    kb_per_qt = ATT_QT // ATT_KB
    qts, kbs, types = [], [], []
    for qt in range(seq // ATT_QT):
        nkb = (qt + 1) * kb_per_qt
        for kb in range(nkb):
            from_end = nkb - 1 - kb
            qts.append(qt)
            kbs.append(kb)
            types.append(ATT_BIAS_TYPES - 1 - from_end if from_end < ATT_BIAS_TYPES - 1 else 0)
    assert len(qts) % ATT_ROTATE == 0
    nunits = len(qts)
    for _ in range(ATT_ROTATE - 1):
        qts.append(0)
        kbs.append(0)
        types.append(0)
    as_i32 = lambda v: jnp.asarray(v, jnp.int32)
    return nunits, as_i32(qts), as_i32(kbs), as_i32(types)


def _attn_flat_bias(rel_bias):
    tab = rel_bias.astype(F32) - rel_bias[NUM_BUCKETS - 1].astype(F32)[None, :]
    base = jnp.arange(ATT_QT)[None, :] - jnp.arange(ATT_KB)[:, None]

    def tile(key_offset):
        rel = base - key_offset
        onehot = (_t5_bucket(rel)[:, :, None] == jnp.arange(NUM_BUCKETS)[None, None, :]).astype(F32)
        vals = jnp.einsum('kqb,bh->hkq', onehot, tab, precision=lax.Precision.HIGHEST)
        return jnp.where((rel >= 0)[None], vals, NEG_BIG)

    zeros = jnp.zeros((rel_bias.shape[1], ATT_KB, ATT_QT), F32)
    return jnp.stack([zeros, tile(-ATT_KB), tile(0), tile(ATT_KB)], axis=1) * LOG2E


def _attn_flat_kernel(tq_ref, tk_ref, tt_ref, lamv_ref, qt_ref, k_ref, vt_ref, bias_ref, g_ref, o_ref,
                      m_sc, acc_sc, sa_sc, sb_sc, sc_sc, ma_sc, mb_sc, mc_sc, *, lam_init, nunits):
    hw = 2 * DA_HD
    heads = range(ATT_HEADS_PER_STEP)
    hrows = [slice(hh * hw, (hh + 1) * hw) for hh in heads]
    ntiles = m_sc.shape[0]

    lv = lamv_ref[...]
    lam = (jnp.exp(jnp.sum(lv[0:1] * lv[1:2], axis=-1, keepdims=True))
           - jnp.exp(jnp.sum(lv[2:3] * lv[3:4], axis=-1, keepdims=True)) + lam_init)

    m_sc[...] = jnp.full(m_sc.shape, NEG_BIG, F32)
    acc_sc[...] = jnp.zeros(acc_sc.shape, F32)
    rowi = lax.broadcasted_iota(jnp.int32, (hw, ATT_QT), 0)

    def fill(buf, mbuf, u):
        q0 = pl.multiple_of(tq_ref[u] * ATT_QT, ATT_QT)
        k0 = pl.multiple_of(tk_ref[u] * ATT_KB, ATT_KB)
        ty = tt_ref[u]
        s = []
        for hh in heads:
            q = qt_ref[0, hrows[hh], pl.ds(q0, ATT_QT)]
            zero = jnp.zeros_like(q)
            qq = jnp.concatenate([jnp.where(rowi < DA_HD, q, zero), jnp.where(rowi >= DA_HD, q, zero)], axis=1)
            s.append(jnp.dot(k_ref[0, pl.ds(k0, ATT_KB), hrows[hh]], qq, preferred_element_type=F32))
        for hh in heads:
            b = bias_ref[hh, ty]
            sb = s[hh] + jnp.concatenate([b, b], axis=1)
            buf[hh] = sb
            mbuf[hh] = _col_max(sb)

    def drain(buf, mbuf, u):
        t = tq_ref[u]
        k0 = pl.multiple_of(tk_ref[u] * ATT_KB, ATT_KB)
        m_prev = [m_sc[t, hh] for hh in heads]
        m_new = [jnp.maximum(m_prev[hh], mbuf[hh]) for hh in heads]
        alpha = [jnp.exp2(m_prev[hh] - m_new[hh]) for hh in heads]
        p = [jnp.exp2((buf[hh] - m_new[hh]).astype(BF16)) for hh in heads]
        ones = jnp.ones((ATT_ONES_ROWS, ATT_KB), BF16)
        pv = [jnp.dot(jnp.concatenate([vt_ref[0, hrows[hh], pl.ds(k0, ATT_KB)], ones], axis=0), p[hh],
                      preferred_element_type=F32) for hh in heads]
        for hh in heads:
            acc_sc[t, hh] = alpha[hh] * acc_sc[t, hh] + pv[hh]
            m_sc[t, hh] = m_new[hh]

    fill(sa_sc, ma_sc, 0)
    fill(sb_sc, mb_sc, 1)

    def trip(i, carry):
        u = i * ATT_ROTATE
        fill(sc_sc, mc_sc, u + 2)
        drain(sa_sc, ma_sc, u)
        fill(sa_sc, ma_sc, u + 3)
        drain(sb_sc, mb_sc, u + 1)
        fill(sb_sc, mb_sc, u + 4)
        drain(sc_sc, mc_sc, u + 2)
        return carry

    lax.fori_loop(0, nunits // ATT_ROTATE, trip, 0)

    def finish(t, carry):
        q0 = pl.multiple_of(t * ATT_QT, ATT_QT)
        for hh in heads:
            a = acc_sc[t, hh]
            acc = a[0:hw, :] * (1.0 / a[hw:hw + 1, :])
            ot = acc[:, 0:ATT_QT] - lam * acc[:, ATT_QT:2 * ATT_QT]
            ms = jnp.mean(ot * ot, axis=0, keepdims=True)
            ot = ot * (lax.rsqrt(ms + LN_EPS) * (1.0 - lam_init))
            o_ref[0, pl.ds(q0, ATT_QT), hrows[hh]] = (ot.T * g_ref[...]).astype(o_ref.dtype)
        return carry

    lax.fori_loop(0, ntiles, finish, 0)


def _diff_attention_flat(qt, k, vt, lamv, bias, subln_g, lam_init):
    b, s, _ = k.shape
    nh = ATT_HEADS_PER_STEP
    hw = 2 * DA_HD
    nunits, tq, tk, tt = _attn_unit_tables(s)
    score = pltpu.VMEM((nh, ATT_KB, 2 * ATT_QT), F32)
    smax = pltpu.VMEM((nh, 1, 2 * ATT_QT), F32)
    grid_spec = pltpu.PrefetchScalarGridSpec(
        num_scalar_prefetch=3,
        grid=(b, DA_HEADS // nh),
        in_specs=[
            pl.BlockSpec((4, DA_HD), lambda bi, hi, *_: (0, 0)),
            pl.BlockSpec((1, nh * hw, s), lambda bi, hi, *_: (bi, hi, 0)),
            pl.BlockSpec((1, s, nh * hw), lambda bi, hi, *_: (bi, 0, hi)),
            pl.BlockSpec((1, nh * hw, s), lambda bi, hi, *_: (bi, hi, 0)),
            pl.BlockSpec((nh, ATT_BIAS_TYPES, ATT_KB, ATT_QT), lambda bi, hi, *_: (hi, 0, 0, 0)),
            pl.BlockSpec((1, hw), lambda bi, hi, *_: (0, 0)),
        ],
        out_specs=pl.BlockSpec((1, s, nh * hw), lambda bi, hi, *_: (bi, 0, hi)),
        scratch_shapes=[pltpu.VMEM((s // ATT_QT, nh, 1, 2 * ATT_QT), F32),
                        pltpu.VMEM((s // ATT_QT, nh, hw + ATT_ONES_ROWS, 2 * ATT_QT), F32),
                        score, score, score, smax, smax, smax],
    )
    return pl.pallas_call(
        functools.partial(_attn_flat_kernel, lam_init=lam_init, nunits=nunits),
        grid_spec=grid_spec,
        out_shape=jax.ShapeDtypeStruct((b, s, DA_HEADS * hw), BF16),
        compiler_params=_cparams(("parallel", "parallel")),
        name="diff_attention",
    )(tq, tk, tt, lamv, qt, k, vt, bias, subln_g)


def _dot(a, b, dims=(((1,), (0,)), ((), ()))):
    return lax.dot_general(a.astype(BF16), b.astype(BF16), dims, preferred_element_type=F32)


def _dot_split_lhs(a, b_bf16):
    ah, al = _split2(a)
    return (jnp.dot(ah, b_bf16, preferred_element_type=F32) + jnp.dot(al, b_bf16, preferred_element_type=F32))


def _dot_split_lhs_rhs(a_bf16, b):
    bh, bl = _split2(b)
    return (jnp.dot(a_bf16, bh, preferred_element_type=F32) + jnp.dot(a_bf16, bl, preferred_element_type=F32))


def _pair_consts():
    c, n = RW_CHUNK, RW_HD
    lane = lax.broadcasted_iota(jnp.int32, (c, 2 * n), 1)
    row = lax.broadcasted_iota(jnp.int32, (2 * c, 2 * c), 0)
    col = lax.broadcasted_iota(jnp.int32, (2 * c, 2 * c), 1)
    same_head = (row // n) == (col // n)
    return lane < n, same_head.astype(BF16)


def _expand(x2, head_a):
    zero = jnp.zeros_like(x2)
    return jnp.concatenate([jnp.where(head_a, x2, zero), jnp.where(head_a, zero, x2)], axis=0)


def _compact(xm):
    return xm[0:RW_CHUNK] + xm[RW_CHUNK:2 * RW_CHUNK]


def _rw_chunk_kernel(r_ref, k_ref, v_ref, lo_ref, mu_ref, mulo_ref, w0_ref, w2_ref,
                     a0_ref, a2_ref, g2_ref, kk_ref, ka_ref, rk_ref,
                     y_ref, bon_ref, g_ref,
                     z_sc, pt_sc, qt_sc, rh_sc, y0_sc, prev_sc, *, nsteps):
    c, n = RW_CHUNK, RW_HD
    pw = 2 * n
    pairs = range(z_sc.shape[0])
    lns = [slice(hp * pw, (hp + 1) * pw) for hp in pairs]
    rows = RW_STEP_CHUNKS * c
    rss = [slice(j * c, (j + 1) * c) for j in range(RW_STEP_CHUNKS)]
    head_a, seg_ones = _pair_consts()

    @pl.when(pl.program_id(1) == 0)
    def _():
        z_sc[...] = jnp.zeros(z_sc.shape, F32)
        pt_sc[...] = jnp.zeros(pt_sc.shape, F32)
        qt_sc[...] = jnp.zeros(qt_sc.shape, F32)
        rh_sc[...] = jnp.zeros(rh_sc.shape, rh_sc.dtype)
        y0_sc[...] = jnp.zeros(y0_sc.shape, F32)
        prev_sc[...] = jnp.zeros(prev_sc.shape, F32)

    z = [z_sc[hp] for hp in pairs]
    for rs in rss:
        for hp in pairs:
            y_ref[0, rs, lns[hp]] = (_dot(rh_sc[rs, lns[hp]], z[hp]) + y0_sc[rs, lns[hp]]).astype(y_ref.dtype)
        z = [_dot3(_expand(pt_sc[rs, lns[hp]], head_a), z[hp]) + _expand(qt_sc[rs, lns[hp]], head_a)
             for hp in pairs]
    for hp in pairs:
        z_sc[hp] = z[hp]

    row = lax.broadcasted_iota(jnp.int32, (2 * c, 2 * c), 0)
    col = lax.broadcasted_iota(jnp.int32, (2 * c, 2 * c), 1)
    strict = (row % c) > (col % c)
    incl = (row % c) >= (col % c)
    trow = lax.broadcasted_iota(jnp.int32, (c, c), 0)
    tcol = lax.broadcasted_iota(jnp.int32, (c, c), 1)
    tri_ones = (trow >= tcol).astype(BF16)
    diag2 = lax.broadcasted_iota(jnp.int32, (c, pw), 0) == (lax.broadcasted_iota(jnp.int32, (c, pw), 1) % n)
    d = functools.partial(jnp.dot, preferred_element_type=F32)

    def cumsum(lw2):
        return _dot_split_lhs_rhs(tri_ones, lw2)

    ci = pl.program_id(1)
    trow1 = lax.broadcasted_iota(jnp.int32, (rows, 1), 0)
    keep_prev = ci < nsteps - 1

    def shifted(cur_ref, slot, mu):
        p = cur_ref[0]
        w = p.shape[1]
        before = prev_sc[slot:slot + 1, 0:w]
        pm1 = jnp.where(trow1 == 0, before, pltpu.roll(p, 1, 0))
        prev_sc[slot:slot + 1, 0:w] = jnp.where(keep_prev, p[rows - 1:rows, :], before)
        return p + mu * (pm1 - p)

    r = shifted(r_ref, 0, mu_ref[0:1])
    k = shifted(k_ref, 1, mu_ref[1:2])
    v = shifted(v_ref, 2, mu_ref[2:3])
    lo = shifted(lo_ref, 3, mulo_ref[...])

    nz = -(w0_ref[...] + d(jnp.tanh(lo[:, 0:128]).astype(BF16), w2_ref[...]))
    softplus = jnp.maximum(nz, 0.0) + jnp.log(1.0 + jnp.exp(-jnp.abs(nz)))
    lw = -jnp.exp(-softplus - 0.5)
    a_ic = jax.nn.sigmoid(a0_ref[...] + d(lo[:, 128:256].astype(BF16), a2_ref[...]))
    g_ref[0] = d(jax.nn.sigmoid(lo[:, 256:512]).astype(BF16), g2_ref[...]).astype(g_ref.dtype)
    kk = k * kk_ref[...]
    km = k * (1.0 + (a_ic - 1.0) * ka_ref[...])

    def intra(units):
        gpairs = range(len(units))
        cum = [cumsum(lw[rs, ln]) for rs, ln in units]
        nrm = [jnp.sqrt(_dot(kk[rs, ln] * kk[rs, ln], seg_ones)) for rs, ln in units]
        big, atm, vm, bdm, kdm = [], [], [], [], []
        for hp, (rs, ln) in enumerate(units):
            lw2, cm = lw[rs, ln], cum[hp]
            g_inv = jnp.exp(-cm)
            g_rest = jnp.exp(cm[c - 1:c, :] - cm)
            kkn = kk[rs, ln] / jnp.maximum(nrm[hp], 1e-12)
            b2 = kkn * a_ic[rs, ln]
            km2 = km[rs, ln]
            a_m = _expand(-kkn * jnp.exp(cm - lw2), head_a)
            r_m = _expand(r[rs, ln] * jnp.exp(cm), head_a)
            b_m = _expand(b2 * g_inv, head_a)
            k_m = _expand(km2 * g_inv, head_a)
            atm.append(a_m)
            bdm.append(_expand(b2 * g_rest, head_a))
            kdm.append(_expand(km2 * g_rest, head_a))
            vm.append(_expand(v[rs, ln], head_a))
            big.append(_dot(jnp.concatenate([a_m, r_m], axis=0), jnp.concatenate([b_m, k_m], axis=0), NT_DIMS))
        lmat = [jnp.where(strict, bg[0:pw, 0:pw], 0.0) for bg in big]
        m_rb = [jnp.where(incl, bg[pw:2 * pw, 0:pw], 0.0) for bg in big]
        mv = [_dot(jnp.concatenate([jnp.where(strict, bg[0:pw, pw:2 * pw], 0.0),
                                    jnp.where(incl, bg[pw:2 * pw, pw:2 * pw], 0.0)], axis=0), vm[hp])
              for hp, bg in enumerate(big)]
        x = [jnp.concatenate([atm[hp], mv[hp][0:pw]], axis=1) for hp in gpairs]
        lp = lmat
        for it in range(6):
            x = [x[hp] + _dot(lp[hp], x[hp]) for hp in gpairs]
            if it < 5:
                lp = [_dot(m, m) for m in lp]
        rb = [_dot(m_rb[hp], x[hp]) for hp in gpairs]
        bx = [_dot(bdm[hp], x[hp], TN_DIMS) for hp in gpairs]
        kv = [_dot(kdm[hp], vm[hp], TN_DIMS) for hp in gpairs]
        for hp, (rs, ln) in enumerate(units):
            g_last = jnp.exp(cum[hp][c - 1:c, :])
            pt_sc[rs, ln] = _compact(bx[hp][:, 0:pw]) + jnp.where(diag2, g_last, 0.0)
            qt_sc[rs, ln] = _compact(bx[hp][:, pw:2 * pw] + kv[hp])
            rh_sc[rs, ln] = (r[rs, ln] * jnp.exp(cum[hp]) + _compact(rb[hp][:, 0:pw])).astype(rh_sc.dtype)
            y0_sc[rs, ln] = _compact(rb[hp][:, pw:2 * pw] + mv[hp][pw:2 * pw])
            bon_ref[0, rs, ln] = (_dot(r[rs, ln] * km[rs, ln] * rk_ref[:, ln], seg_ones)
                                  * v[rs, ln]).astype(bon_ref.dtype)

    intra([(rs, ln) for rs in rss for ln in lns])


def _rw_chunks(rest, mu_rkv, mu_lo, w0, w2, a0, a2, g2, k_k, k_a, r_k):
    b, s, _ = rest.shape
    d = D_MODEL
    c = RW_STEP_CHUNKS * RW_CHUNK
    nc = s // c
    lora_w = mu_lo.shape[1]
    cur = lambda col, w: pl.BlockSpec((1, c, w), lambda bi, ci: (bi, jnp.minimum(ci, nc - 1), col))
    prev = pl.BlockSpec((1, c, d), lambda bi, ci: (bi, jnp.maximum(ci - 1, 0), 0))
    full = lambda a: pl.BlockSpec(a.shape, lambda bi, ci: (0,) * a.ndim)
    params = (mu_rkv, mu_lo, w0, w2, a0, a2, g2, k_k, k_a, r_k)
    out = jax.ShapeDtypeStruct((b, s, d), BF16)
    lora_col = (3 * d + 2 * d) // lora_w
    return pl.pallas_call(
        functools.partial(_rw_chunk_kernel, nsteps=nc),
        grid=(b, nc + 1),
        in_specs=[cur(0, d), cur(1, d), cur(2, d), cur(lora_col, lora_w)] + [full(a) for a in params],
        out_specs=[prev, cur(0, d), cur(0, d)],
        out_shape=[out, out, out],
        scratch_shapes=[pltpu.VMEM((d // (2 * RW_HD), 2 * RW_HD, 2 * RW_HD), F32),
                        pltpu.VMEM((c, d), F32), pltpu.VMEM((c, d), F32),
                        pltpu.VMEM((c, d), BF16), pltpu.VMEM((c, d), F32),
                        pltpu.VMEM((4, d), F32)],
        compiler_params=_cparams(("parallel", "arbitrary")),
        name="rwkv_chunks",
    )(rest, rest, rest, rest, *params)


def _mix_kernel(oda_ref, yrw_ref, bon_ref, gout_ref, ga_ref, gb_ref, x_ref, wda_ref, wrw_ref, wmix_ref,
                gng_ref, gnb_ref, lng_ref, lnb_ref, o_ref, *, alpha):
    y_da = jnp.dot(oda_ref[...], wda_ref[...], preferred_element_type=F32)
    _, seg_ones = _pair_consts()
    parts = []
    for hp in range(RW_HEADS // 2):
        ln = slice(hp * 2 * RW_HD, (hp + 1) * 2 * RW_HD)
        y = yrw_ref[:, ln].astype(F32)
        mu = _dot_split_lhs(y, seg_ones) * (1.0 / RW_HD)
        yc = y - mu
        var = _dot_split_lhs(yc * yc, seg_ones) * (1.0 / RW_HD)
        parts.append(yc * lax.rsqrt(var + RW_LNX_EPS))
    yn = jnp.concatenate(parts, axis=1) * gng_ref[...] + gnb_ref[...] + bon_ref[...].astype(F32)
    y_rw = jnp.dot((yn * gout_ref[...].astype(F32)).astype(BF16), wrw_ref[...], preferred_element_type=F32)
    mixed = jax.nn.sigmoid(ga_ref[...]) * y_da + jax.nn.sigmoid(gb_ref[...]) * y_rw
    z = alpha * x_ref[...] + jnp.dot(mixed.astype(BF16), wmix_ref[...], preferred_element_type=F32)
    o_ref[...] = _layer_norm_rows(z, lng_ref[...], lnb_ref[...])


def _mix(o_da, y_rw, bonus, g_out, rest, x, w_da, w_rw, w_mix, gn_g, gn_b, ln_g, ln_b, alpha, tm=512):
    t, d = x.shape
    rowb = lambda c: pl.BlockSpec((tm, d), lambda i: (i, c))
    full = lambda a: pl.BlockSpec(a.shape, lambda i: (0,) * a.ndim)
    return pl.pallas_call(
        functools.partial(_mix_kernel, alpha=alpha),
        grid=(t // tm,),
        in_specs=[rowb(0), rowb(0), rowb(0), rowb(0), rowb(3), rowb(4), rowb(0),
                  full(w_da), full(w_rw), full(w_mix), full(gn_g), full(gn_b), full(ln_g), full(ln_b)],
        out_specs=rowb(0),
        out_shape=jax.ShapeDtypeStruct((t, d), F32),
        compiler_params=_cparams(("parallel",)),
        name="mix_merge_ln1",
    )(o_da, y_rw, bonus, g_out, rest, rest, x, w_da, w_rw, w_mix, gn_g, gn_b, ln_g, ln_b)


def _mem_kv_kernel(mem_ref, g_ref, b_ref, w_ref, ck_ref, cv_ref):
    m = _layer_norm_rows(mem_ref[0], g_ref[...], b_ref[...])
    kv = jnp.dot(m.astype(BF16), w_ref[...], preferred_element_type=F32)
    ck_ref[0] = kv[:, :D_MODEL].astype(BF16)
    cv_ref[0] = kv[:, D_MODEL:].astype(BF16)


def _mem_kv(mem, g, b, w_ckv):
    bsz, m, d = mem.shape
    full = lambda a: pl.BlockSpec(a.shape, lambda i: (0,) * a.ndim)
    blk = pl.BlockSpec((1, m, d), lambda i: (i, 0, 0))
    return pl.pallas_call(
        _mem_kv_kernel,
        grid=(bsz,),
        in_specs=[blk, full(g), full(b), full(w_ckv)],
        out_specs=[blk, blk],
        out_shape=[jax.ShapeDtypeStruct((bsz, m, d), BF16)] * 2,
        compiler_params=_cparams(("parallel",)),
        name="mem_kv",
    )(mem, g, b, w_ckv)


def _cross_kernel(x_ref, ck_ref, cv_ref, wq_ref, wo_ref, lng_ref, lnb_ref, o_ref, *, alpha):
    x = x_ref[...]
    cq = jnp.dot(x.astype(BF16), wq_ref[...], preferred_element_type=F32)
    cq = (cq * (CA_HD ** -0.5)).astype(BF16)
    outs = []
    for h in range(CA_HEADS):
        sl = slice(h * CA_HD, (h + 1) * CA_HD)
        s = lax.dot_general(cq[:, sl], ck_ref[0, :, sl], NT_DIMS, preferred_element_type=F32)
        s = s - jnp.max(s, axis=-1, keepdims=True)
        p = jnp.exp(s)
        p = p / jnp.sum(p, axis=-1, keepdims=True)
        outs.append(jnp.dot(p.astype(BF16), cv_ref[0, :, sl], preferred_element_type=F32))
    co = jnp.dot(jnp.concatenate(outs, axis=1).astype(BF16), wo_ref[...], preferred_element_type=F32)
    o_ref[...] = _layer_norm_rows(alpha * x + co, lng_ref[...], lnb_ref[...])


def _cross_attention(x, ck, cv, w_cq, w_co, ln_g, ln_b, seq, alpha, tm=512):
    t, d = x.shape
    tiles_per_seq = seq // tm
    full = lambda a: pl.BlockSpec(a.shape, lambda i: (0,) * a.ndim)
    rowb = pl.BlockSpec((tm, d), lambda i: (i, 0))
    memb = pl.BlockSpec((1,) + ck.shape[1:], lambda i: (i // tiles_per_seq, 0, 0))
    return pl.pallas_call(
        functools.partial(_cross_kernel, alpha=alpha),
        grid=(t // tm,),
        in_specs=[rowb, memb, memb, full(w_cq), full(w_co), full(ln_g), full(ln_b)],
        out_specs=rowb,
        out_shape=jax.ShapeDtypeStruct((t, d), F32),
        compiler_params=_cparams(("parallel",)),
        name="cross_attention_ln2",
    )(x, ck, cv, w_cq, w_co, ln_g, ln_b)


def _ffn_kernel(x_ref, win_ref, wout_ref, lng_ref, lnb_ref, o_ref, *, alpha, tf):
    x = x_ref[...]
    xb = x.astype(BF16)
    acc = alpha * x
    for j in range(D_FF // tf):
        gate = jnp.dot(xb, win_ref[:, j * tf:(j + 1) * tf], preferred_element_type=F32)
        up = jnp.dot(xb, win_ref[:, D_FF + j * tf:D_FF + (j + 1) * tf], preferred_element_type=F32)
        h = (gate * jax.nn.sigmoid(gate) * up).astype(BF16)
        acc = acc + jnp.dot(h, wout_ref[j * tf:(j + 1) * tf, :], preferred_element_type=F32)
    o_ref[...] = _layer_norm_rows(acc, lng_ref[...], lnb_ref[...])


def _ffn(x, w_in, w_out, ln_g, ln_b, alpha, tm=512, tf=1408):
    t, d = x.shape
    rowb = pl.BlockSpec((tm, d), lambda i: (i, 0))
    return pl.pallas_call(
        functools.partial(_ffn_kernel, alpha=alpha, tf=tf),
        grid=(t // tm,),
        in_specs=[rowb, _resident(w_in), _resident(w_out), _resident(ln_g), _resident(ln_b)],
        out_specs=rowb,
        out_shape=jax.ShapeDtypeStruct((t, d), F32),
        compiler_params=_cparams(("parallel",)),
        name="swiglu_ffn_ln3",
    )(x, w_in, w_out, ln_g, ln_b)


def _pad_cols(a, width):
    return jnp.pad(a, ((0, 0), (0, width - a.shape[1])))


def _pad_rows(a, height):
    return jnp.pad(a, ((0, height - a.shape[0]), (0, 0)))


def kernel(x, mem, rel_bias, w_in, shift_mu, lambda_q1, lambda_k1, lambda_q2, lambda_k2, da_subln_g, w_da_proj, rw_w0, rw_w2, rw_a0, rw_a2, rw_g2, rw_k_k, rw_k_a, rw_r_k, rw_lnx_g, rw_lnx_b, w_rw_proj, w_mix_out, ln1_g, ln1_b, mem_ln_g, mem_ln_b, w_cq, w_ckv, w_co, ln2_g, ln2_b, w_ffn_in, w_ffn_out, ln3_g, ln3_b):
    bsz, seq, d = x.shape
    depth = w_in.shape[0]
    alpha = (2.0 * depth) ** 0.25
    t = bsz * seq
    row = lambda a: a.reshape(1, -1).astype(F32)
    bias_tiles = _attn_flat_bias(rel_bias)
    xf = x.reshape(t, d)

    qkv_w = 3 * D_MODEL
    rw0 = qkv_w
    lo0 = rw0 + 3 * D_MODEL
    gate0 = lo0 + DECAY_LORA + AAA_LORA + GATE_LORA

    for l in range(depth):
        w = w_in[l]
        w_qt = w[:, :D_MODEL].T.astype(BF16)
        w_k = w[:, D_MODEL:2 * D_MODEL].astype(BF16)
        w_vt = w[:, 2 * D_MODEL:qkv_w].T.astype(BF16)
        w_rest = jnp.concatenate([
            w[:, rw0:lo0], w[:, gate0:gate0 + 2 * D_MODEL],
            _pad_cols(w[:, lo0:lo0 + DECAY_LORA], 128),
            _pad_cols(w[:, lo0 + DECAY_LORA:lo0 + DECAY_LORA + AAA_LORA], 128),
            _pad_cols(w[:, lo0 + DECAY_LORA + AAA_LORA:gate0], 256)], axis=1).astype(BF16)
        mu = shift_mu[l].astype(F32)
        mu_rkv = mu[:3 * D_MODEL].reshape(3, D_MODEL)
        o1 = 3 * D_MODEL
        mu_lo = jnp.concatenate([
            _pad_cols(mu[None, o1:o1 + DECAY_LORA], 128),
            _pad_cols(mu[None, o1 + DECAY_LORA:o1 + DECAY_LORA + AAA_LORA], 128),
            _pad_cols(mu[None, o1 + DECAY_LORA + AAA_LORA:], 256)], axis=1)
        w2 = _pad_rows(rw_w2[l], 128).astype(BF16)
        a2 = _pad_rows(rw_a2[l], 128).astype(BF16)
        g2 = _pad_rows(rw_g2[l], 256).astype(BF16)

        qt, kda, vt = _qkv_proj(xf, w_qt, w_k, w_vt, bsz, seq)
        rest = _matmul(xf, w_rest, F32, 512, 1408, "in_proj_rest")

        lam_init = 0.8 - 0.6 * math.exp(-0.3 * l)
        lamv = jnp.stack([lambda_q1[l], lambda_k1[l], lambda_q2[l], lambda_k2[l]]).astype(F32)
        o_da = _diff_attention_flat(qt, kda.reshape(bsz, seq, d), vt, lamv, bias_tiles, row(da_subln_g[l]),
                                    lam_init)

        y_rw, bon, g_out = _rw_chunks(
            rest.reshape(bsz, seq, rest.shape[1]), mu_rkv, mu_lo, row(rw_w0[l]), w2, row(rw_a0[l]),
            a2, g2, row(rw_k_k[l]), row(rw_k_a[l]), row(rw_r_k[l]))

        x1 = _mix(o_da.reshape(t, d), y_rw.reshape(t, d), bon.reshape(t, d), g_out.reshape(t, d), rest, xf,
                  w_da_proj[l].astype(BF16), w_rw_proj[l].astype(BF16), w_mix_out[l].astype(BF16),
                  row(rw_lnx_g[l]), row(rw_lnx_b[l]), row(ln1_g[l]), row(ln1_b[l]), alpha)

        ck, cv = _mem_kv(mem, row(mem_ln_g[l]), row(mem_ln_b[l]), w_ckv[l].astype(BF16))
        x2 = _cross_attention(x1, ck, cv, w_cq[l].astype(BF16), w_co[l].astype(BF16),
                              row(ln2_g[l]), row(ln2_b[l]), seq, alpha)

        xf = _ffn(x2, w_ffn_in[l].astype(BF16), w_ffn_out[l].astype(BF16),
                  row(ln3_g[l]), row(ln3_b[l]), alpha)
    return xf.reshape(bsz, seq, d)
```

```python
import functools
import math

import jax
import jax.numpy as jnp
from jax import lax
from jax.experimental import pallas as pl
from jax.experimental.pallas import tpu as pltpu

F32 = jnp.float32
BF16 = jnp.bfloat16

D_MODEL = 1024
DA_HEADS = 8
DA_HD = 64
NUM_BUCKETS = 32
MAX_DISTANCE = 128
RW_HD = 64
RW_HEADS = D_MODEL // RW_HD
DECAY_LORA = 64
AAA_LORA = 64
GATE_LORA = 160
RW_LNX_EPS = 64e-5
MEM_TOKENS = 256
CA_HEADS = 4
CA_HD = D_MODEL // CA_HEADS
D_FF = 2816
LN_EPS = 1e-5
NEG_BIG = -1e30

LOG2E = math.log2(math.e)
ATT_HEADS_PER_STEP = 2
ATT_QT = 512
ATT_KB = 256
ATT_BIAS_TYPES = 4
ATT_ROTATE = 3
ATT_ONES_ROWS = 16
RW_CHUNK = 64
RW_STEP_CHUNKS = 2
VMEM_LIMIT = 56 * 1024 * 1024

NT_DIMS = (((1,), (1,)), ((), ()))
TN_DIMS = (((0,), (0,)), ((), ()))


def _cparams(sem, vmem=VMEM_LIMIT):
    return pltpu.CompilerParams(dimension_semantics=sem, vmem_limit_bytes=vmem)


def _split2(a):
    hi = a.astype(BF16)
    lo = (a - hi.astype(F32)).astype(BF16)
    return hi, lo


def _dot3(a, b, dims=(((1,), (0,)), ((), ()))):
    ah, al = _split2(a)
    bh, bl = _split2(b)
    d = functools.partial(lax.dot_general, dimension_numbers=dims, preferred_element_type=F32)
    return d(ah, bh) + (d(ah, bl) + d(al, bh))


def _layer_norm_rows(z, g, b):
    mu = jnp.mean(z, axis=-1, keepdims=True)
    zc = z - mu
    var = jnp.mean(zc * zc, axis=-1, keepdims=True)
    return zc * lax.rsqrt(var + LN_EPS) * g + b


def _resident(a):
    return pl.BlockSpec(a.shape, lambda *_: (0,) * a.ndim, pipeline_mode=pl.Buffered(1))


def _mm_kernel(x_ref, w_ref, o_ref, *, tn):
    xb = x_ref[...].astype(BF16)
    for j in range(w_ref.shape[1] // tn):
        cols = slice(j * tn, (j + 1) * tn)
        o_ref[:, cols] = jnp.dot(xb, w_ref[:, cols], preferred_element_type=F32).astype(o_ref.dtype)


def _matmul(x, w, out_dtype, tm, tn, name):
    t, k = x.shape
    n = w.shape[1]
    return pl.pallas_call(
        functools.partial(_mm_kernel, tn=tn),
        grid=(t // tm,),
        in_specs=[pl.BlockSpec((tm, k), lambda i: (i, 0)), _resident(w)],
        out_specs=pl.BlockSpec((tm, n), lambda i: (i, 0)),
        out_shape=jax.ShapeDtypeStruct((t, n), out_dtype),
        compiler_params=_cparams(("parallel",)),
        name=name,
    )(x, w)


def _qkv_kernel(x_ref, wqt_ref, wk_ref, wvt_ref, qt_ref, k_ref, vt_ref):
    xb = x_ref[...].astype(BF16)
    k_ref[...] = jnp.dot(xb, wk_ref[...], preferred_element_type=F32).astype(BF16)
    qt = lax.dot_general(wqt_ref[...], xb, NT_DIMS, preferred_element_type=F32)
    qt_ref[0] = (qt * (DA_HD ** -0.5 * LOG2E)).astype(BF16)
    vt_ref[0] = lax.dot_general(wvt_ref[...], xb, NT_DIMS, preferred_element_type=F32).astype(BF16)


def _qkv_proj(x, wqt, wk, wvt, bsz, seq, tm=512):
    t, d = x.shape
    tiles_per_seq = seq // tm
    full = lambda a: pl.BlockSpec(a.shape, lambda i: (0,) * a.ndim)
    tblk = pl.BlockSpec((1, d, tm), lambda i: (i // tiles_per_seq, 0, i % tiles_per_seq))
    rowb = pl.BlockSpec((tm, d), lambda i: (i, 0))
    tshape = jax.ShapeDtypeStruct((bsz, d, seq), BF16)
    return pl.pallas_call(
        _qkv_kernel,
        grid=(t // tm,),
        in_specs=[rowb, full(wqt), full(wk), full(wvt)],
        out_specs=[tblk, rowb, tblk],
        out_shape=[tshape, jax.ShapeDtypeStruct((t, d), BF16), tshape],
        compiler_params=_cparams(("parallel",)),
        name="in_proj_qkv",
    )(x, wqt, wk, wvt)


def _t5_bucket(rel):
    n = jnp.maximum(rel, 0)
    max_exact = NUM_BUCKETS // 2
    nf = jnp.maximum(n, 1).astype(jnp.float32)
    large = max_exact + jnp.floor(jnp.log(nf / max_exact) / math.log(MAX_DISTANCE / max_exact)
                                  * (NUM_BUCKETS - max_exact)).astype(jnp.int32)
    large = jnp.minimum(large, NUM_BUCKETS - 1)
    return jnp.where(n < max_exact, n, large)


def _col_max(s):
    while s.shape[0] > 8:
        half = s.shape[0] // 2
        s = jnp.maximum(s[:half], s[half:])
    return jnp.max(s, axis=0, keepdims=True)


def _attn_unit_tables(seq):
    kb_per_qt = ATT_QT // ATT_KB
    qts, kbs, types = [], [], []
    for qt in range(seq // ATT_QT):
        nkb = (qt + 1) * kb_per_qt
        for kb in range(nkb):
            from_end = nkb - 1 - kb
            qts.append(qt)
            kbs.append(kb)
            types.append(ATT_BIAS_TYPES - 1 - from_end if from_end < ATT_BIAS_TYPES - 1 else 0)
    assert len(qts) % ATT_ROTATE == 0
    nunits = len(qts)
    for _ in range(ATT_ROTATE - 1):
        qts.append(0)
        kbs.append(0)
        types.append(0)
    as_i32 = lambda v: jnp.asarray(v, jnp.int32)
    return nunits, as_i32(qts), as_i32(kbs), as_i32(types)


def _attn_flat_bias(rel_bias):
    tab = rel_bias.astype(F32) - rel_bias[NUM_BUCKETS - 1].astype(F32)[None, :]
    base = jnp.arange(ATT_QT)[None, :] - jnp.arange(ATT_KB)[:, None]

    def tile(key_offset):
        rel = base - key_offset
        onehot = (_t5_bucket(rel)[:, :, None] == jnp.arange(NUM_BUCKETS)[None, None, :]).astype(F32)
        vals = jnp.einsum('kqb,bh->hkq', onehot, tab, precision=lax.Precision.HIGHEST)
        return jnp.where((rel >= 0)[None], vals, NEG_BIG)

    zeros = jnp.zeros((rel_bias.shape[1], ATT_KB, ATT_QT), F32)
    return jnp.stack([zeros, tile(-ATT_KB), tile(0), tile(ATT_KB)], axis=1) * LOG2E


def _attn_flat_kernel(tq_ref, tk_ref, tt_ref, lamv_ref, qt_ref, k_ref, vt_ref, bias_ref, g_ref, o_ref,
                      m_sc, acc_sc, sa_sc, sb_sc, sc_sc, ma_sc, mb_sc, mc_sc, *, lam_init, nunits):
    hw = 2 * DA_HD
    heads = range(ATT_HEADS_PER_STEP)
    hrows = [slice(hh * hw, (hh + 1) * hw) for hh in heads]
    ntiles = m_sc.shape[0]

    lv = lamv_ref[...]
    lam = (jnp.exp(jnp.sum(lv[0:1] * lv[1:2], axis=-1, keepdims=True))
           - jnp.exp(jnp.sum(lv[2:3] * lv[3:4], axis=-1, keepdims=True)) + lam_init)

    m_sc[...] = jnp.full(m_sc.shape, NEG_BIG, F32)
    acc_sc[...] = jnp.zeros(acc_sc.shape, F32)
    rowi = lax.broadcasted_iota(jnp.int32, (hw, ATT_QT), 0)

    def fill(buf, mbuf, u):
        q0 = pl.multiple_of(tq_ref[u] * ATT_QT, ATT_QT)
        k0 = pl.multiple_of(tk_ref[u] * ATT_KB, ATT_KB)
        ty = tt_ref[u]
        s = []
        for hh in heads:
            q = qt_ref[0, hrows[hh], pl.ds(q0, ATT_QT)]
            zero = jnp.zeros_like(q)
            qq = jnp.concatenate([jnp.where(rowi < DA_HD, q, zero), jnp.where(rowi >= DA_HD, q, zero)], axis=1)
            s.append(jnp.dot(k_ref[0, pl.ds(k0, ATT_KB), hrows[hh]], qq, preferred_element_type=F32))
        for hh in heads:
            b = bias_ref[hh, ty]
            sb = s[hh] + jnp.concatenate([b, b], axis=1)
            buf[hh] = sb
            mbuf[hh] = _col_max(sb)

    def drain(buf, mbuf, u):
        t = tq_ref[u]
        k0 = pl.multiple_of(tk_ref[u] * ATT_KB, ATT_KB)
        m_prev = [m_sc[t, hh] for hh in heads]
        m_new = [jnp.maximum(m_prev[hh], mbuf[hh]) for hh in heads]
        alpha = [jnp.exp2(m_prev[hh] - m_new[hh]) for hh in heads]
        p = [jnp.exp2((buf[hh] - m_new[hh]).astype(BF16)) for hh in heads]
        ones = jnp.ones((ATT_ONES_ROWS, ATT_KB), BF16)
        pv = [jnp.dot(jnp.concatenate([vt_ref[0, hrows[hh], pl.ds(k0, ATT_KB)], ones], axis=0), p[hh],
                      preferred_element_type=F32) for hh in heads]
        for hh in heads:
            acc_sc[t, hh] = alpha[hh] * acc_sc[t, hh] + pv[hh]
            m_sc[t, hh] = m_new[hh]

    fill(sa_sc, ma_sc, 0)
    fill(sb_sc, mb_sc, 1)

    def trip(i, carry):
        u = i * ATT_ROTATE
        fill(sc_sc, mc_sc, u + 2)
        drain(sa_sc, ma_sc, u)
        fill(sa_sc, ma_sc, u + 3)
        drain(sb_sc, mb_sc, u + 1)
        fill(sb_sc, mb_sc, u + 4)
        drain(sc_sc, mc_sc, u + 2)
        return carry

    lax.fori_loop(0, nunits // ATT_ROTATE, trip, 0)

    def finish(t, carry):
        q0 = pl.multiple_of(t * ATT_QT, ATT_QT)
        for hh in heads:
            a = acc_sc[t, hh]
            acc = a[0:hw, :] * (1.0 / a[hw:hw + 1, :])
            ot = acc[:, 0:ATT_QT] - lam * acc[:, ATT_QT:2 * ATT_QT]
            ms = jnp.mean(ot * ot, axis=0, keepdims=True)
            ot = ot * (lax.rsqrt(ms + LN_EPS) * (1.0 - lam_init))
            o_ref[0, pl.ds(q0, ATT_QT), hrows[hh]] = (ot.T * g_ref[...]).astype(o_ref.dtype)
        return carry

    lax.fori_loop(0, ntiles, finish, 0)


def _diff_attention_flat(qt, k, vt, lamv, bias, subln_g, lam_init):
    b, s, _ = k.shape
    nh = ATT_HEADS_PER_STEP
    hw = 2 * DA_HD
    nunits, tq, tk, tt = _attn_unit_tables(s)
    score = pltpu.VMEM((nh, ATT_KB, 2 * ATT_QT), F32)
    smax = pltpu.VMEM((nh, 1, 2 * ATT_QT), F32)
    grid_spec = pltpu.PrefetchScalarGridSpec(
        num_scalar_prefetch=3,
        grid=(b, DA_HEADS // nh),
        in_specs=[
            pl.BlockSpec((4, DA_HD), lambda bi, hi, *_: (0, 0)),
            pl.BlockSpec((1, nh * hw, s), lambda bi, hi, *_: (bi, hi, 0)),
            pl.BlockSpec((1, s, nh * hw), lambda bi, hi, *_: (bi, 0, hi)),
            pl.BlockSpec((1, nh * hw, s), lambda bi, hi, *_: (bi, hi, 0)),
            pl.BlockSpec((nh, ATT_BIAS_TYPES, ATT_KB, ATT_QT), lambda bi, hi, *_: (hi, 0, 0, 0)),
            pl.BlockSpec((1, hw), lambda bi, hi, *_: (0, 0)),
        ],
        out_specs=pl.BlockSpec((1, s, nh * hw), lambda bi, hi, *_: (bi, 0, hi)),
        scratch_shapes=[pltpu.VMEM((s // ATT_QT, nh, 1, 2 * ATT_QT), F32),
                        pltpu.VMEM((s // ATT_QT, nh, hw + ATT_ONES_ROWS, 2 * ATT_QT), F32),
                        score, score, score, smax, smax, smax],
    )
    return pl.pallas_call(
        functools.partial(_attn_flat_kernel, lam_init=lam_init, nunits=nunits),
        grid_spec=grid_spec,
        out_shape=jax.ShapeDtypeStruct((b, s, DA_HEADS * hw), BF16),
        compiler_params=_cparams(("parallel", "parallel")),
        name="diff_attention",
    )(tq, tk, tt, lamv, qt, k, vt, bias, subln_g)


def _dot(a, b, dims=(((1,), (0,)), ((), ()))):
    return lax.dot_general(a.astype(BF16), b.astype(BF16), dims, preferred_element_type=F32)


def _dot_split_lhs_rhs(a_bf16, b):
    bh, bl = _split2(b)
    return (jnp.dot(a_bf16, bh, preferred_element_type=F32) + jnp.dot(a_bf16, bl, preferred_element_type=F32))


def _pair_consts():
    c, n = RW_CHUNK, RW_HD
    lane = lax.broadcasted_iota(jnp.int32, (c, 2 * n), 1)
    row = lax.broadcasted_iota(jnp.int32, (2 * c, 2 * c), 0)
    col = lax.broadcasted_iota(jnp.int32, (2 * c, 2 * c), 1)
    same_head = (row // n) == (col // n)
    return lane < n, same_head.astype(BF16)


def _expand(x2, head_a):
    zero = jnp.zeros_like(x2)
    return jnp.concatenate([jnp.where(head_a, x2, zero), jnp.where(head_a, zero, x2)], axis=0)


def _compact(xm):
    return xm[0:RW_CHUNK] + xm[RW_CHUNK:2 * RW_CHUNK]


def _rw_chunk_kernel(r_ref, k_ref, v_ref, lo_ref, mu_ref, mulo_ref, w0_ref, w2_ref,
                     a0_ref, a2_ref, g2_ref, kk_ref, ka_ref, rk_ref,
                     y_ref, bon_ref, g_ref,
                     z_sc, pt_sc, qt_sc, rh_sc, y0_sc, prev_sc, *, nsteps):
    c, n = RW_CHUNK, RW_HD
    pw = 2 * n
    pairs = range(z_sc.shape[0])
    lns = [slice(hp * pw, (hp + 1) * pw) for hp in pairs]
    rows = RW_STEP_CHUNKS * c
    rss = [slice(j * c, (j + 1) * c) for j in range(RW_STEP_CHUNKS)]
    head_a, seg_ones = _pair_consts()

    @pl.when(pl.program_id(1) == 0)
    def _():
        z_sc[...] = jnp.zeros(z_sc.shape, F32)
        pt_sc[...] = jnp.zeros(pt_sc.shape, F32)
        qt_sc[...] = jnp.zeros(qt_sc.shape, F32)
        rh_sc[...] = jnp.zeros(rh_sc.shape, rh_sc.dtype)
        y0_sc[...] = jnp.zeros(y0_sc.shape, F32)
        prev_sc[...] = jnp.zeros(prev_sc.shape, F32)

    z = [z_sc[hp] for hp in pairs]
    for rs in rss:
        for hp in pairs:
            y_ref[0, rs, lns[hp]] = _dot(rh_sc[rs, lns[hp]], z[hp]) + y0_sc[rs, lns[hp]]
        z = [_dot3(_expand(pt_sc[rs, lns[hp]], head_a), z[hp]) + _expand(qt_sc[rs, lns[hp]], head_a)
             for hp in pairs]
    for hp in pairs:
        z_sc[hp] = z[hp]

    row = lax.broadcasted_iota(jnp.int32, (2 * c, 2 * c), 0)
    col = lax.broadcasted_iota(jnp.int32, (2 * c, 2 * c), 1)
    strict = (row % c) > (col % c)
    incl = (row % c) >= (col % c)
    trow = lax.broadcasted_iota(jnp.int32, (c, c), 0)
    tcol = lax.broadcasted_iota(jnp.int32, (c, c), 1)
    tri_ones = (trow >= tcol).astype(BF16)
    diag2 = lax.broadcasted_iota(jnp.int32, (c, pw), 0) == (lax.broadcasted_iota(jnp.int32, (c, pw), 1) % n)
    d = functools.partial(jnp.dot, preferred_element_type=F32)

    def cumsum(lw2):
        return _dot_split_lhs_rhs(tri_ones, lw2)

    ci = pl.program_id(1)
    trow1 = lax.broadcasted_iota(jnp.int32, (rows, 1), 0)
    keep_prev = ci < nsteps - 1

    def shifted(cur_ref, slot, mu):
        p = cur_ref[0]
        w = p.shape[1]
        before = prev_sc[slot:slot + 1, 0:w]
        pm1 = jnp.where(trow1 == 0, before, pltpu.roll(p, 1, 0))
        prev_sc[slot:slot + 1, 0:w] = jnp.where(keep_prev, p[rows - 1:rows, :], before)
        return p + mu * (pm1 - p)

    r = shifted(r_ref, 0, mu_ref[0:1])
    k = shifted(k_ref, 1, mu_ref[1:2])
    v = shifted(v_ref, 2, mu_ref[2:3])
    lo = shifted(lo_ref, 3, mulo_ref[...])

    nz = -(w0_ref[...] + d(jnp.tanh(lo[:, 0:128]).astype(BF16), w2_ref[...]))
    softplus = jnp.maximum(nz, 0.0) + jnp.log(1.0 + jnp.exp(-jnp.abs(nz)))
    lw = -jnp.exp(-softplus - 0.5)
    a_ic = jax.nn.sigmoid(a0_ref[...] + d(lo[:, 128:256].astype(BF16), a2_ref[...]))
    g_ref[0] = d(jax.nn.sigmoid(lo[:, 256:512]).astype(BF16), g2_ref[...])
    kk = k * kk_ref[...]
    km = k * (1.0 + (a_ic - 1.0) * ka_ref[...])

    def intra(units):
        gpairs = range(len(units))
        cum = [cumsum(lw[rs, ln]) for rs, ln in units]
        nrm = [jnp.sqrt(_dot(kk[rs, ln] * kk[rs, ln], seg_ones)) for rs, ln in units]
        big, atm, vm, bdm, kdm = [], [], [], [], []
        for hp, (rs, ln) in enumerate(units):
            lw2, cm = lw[rs, ln], cum[hp]
            g_inv = jnp.exp(-cm)
            g_rest = jnp.exp(cm[c - 1:c, :] - cm)
            kkn = kk[rs, ln] / jnp.maximum(nrm[hp], 1e-12)
            b2 = kkn * a_ic[rs, ln]
            km2 = km[rs, ln]
            a_m = _expand(-kkn * jnp.exp(cm - lw2), head_a)
            r_m = _expand(r[rs, ln] * jnp.exp(cm), head_a)
            b_m = _expand(b2 * g_inv, head_a)
            k_m = _expand(km2 * g_inv, head_a)
            atm.append(a_m)
            bdm.append(_expand(b2 * g_rest, head_a))
            kdm.append(_expand(km2 * g_rest, head_a))
            vm.append(_expand(v[rs, ln], head_a))
            big.append(_dot(jnp.concatenate([a_m, r_m], axis=0), jnp.concatenate([b_m, k_m], axis=0), NT_DIMS))
        lmat = [jnp.where(strict, bg[0:pw, 0:pw], 0.0) for bg in big]
        m_rb = [jnp.where(incl, bg[pw:2 * pw, 0:pw], 0.0) for bg in big]
        mv = [_dot(jnp.concatenate([jnp.where(strict, bg[0:pw, pw:2 * pw], 0.0),
                                    jnp.where(incl, bg[pw:2 * pw, pw:2 * pw], 0.0)], axis=0), vm[hp])
              for hp, bg in enumerate(big)]
        x = [jnp.concatenate([atm[hp], mv[hp][0:pw]], axis=1) for hp in gpairs]
        lp = lmat
        for it in range(6):
            x = [x[hp] + _dot(lp[hp], x[hp]) for hp in gpairs]
            if it < 5:
                lp = [_dot(m, m) for m in lp]
        rb = [_dot(m_rb[hp], x[hp]) for hp in gpairs]
        bx = [_dot(bdm[hp], x[hp], TN_DIMS) for hp in gpairs]
        kv = [_dot(kdm[hp], vm[hp], TN_DIMS) for hp in gpairs]
        for hp, (rs, ln) in enumerate(units):
            g_last = jnp.exp(cum[hp][c - 1:c, :])
            pt_sc[rs, ln] = _compact(bx[hp][:, 0:pw]) + jnp.where(diag2, g_last, 0.0)
            qt_sc[rs, ln] = _compact(bx[hp][:, pw:2 * pw] + kv[hp])
            rh_sc[rs, ln] = (r[rs, ln] * jnp.exp(cum[hp]) + _compact(rb[hp][:, 0:pw])).astype(rh_sc.dtype)
            y0_sc[rs, ln] = _compact(rb[hp][:, pw:2 * pw] + mv[hp][pw:2 * pw])
            bon_ref[0, rs, ln] = _dot(r[rs, ln] * km[rs, ln] * rk_ref[:, ln], seg_ones) * v[rs, ln]

    intra([(rs, ln) for rs in rss for ln in lns])


def _rw_chunks(rest, mu_rkv, mu_lo, w0, w2, a0, a2, g2, k_k, k_a, r_k):
    b, s, _ = rest.shape
    d = D_MODEL
    c = RW_STEP_CHUNKS * RW_CHUNK
    nc = s // c
    lora_w = mu_lo.shape[1]
    cur = lambda col, w: pl.BlockSpec((1, c, w), lambda bi, ci: (bi, jnp.minimum(ci, nc - 1), col))
    prev = pl.BlockSpec((1, c, d), lambda bi, ci: (bi, jnp.maximum(ci - 1, 0), 0))
    full = lambda a: pl.BlockSpec(a.shape, lambda bi, ci: (0,) * a.ndim)
    params = (mu_rkv, mu_lo, w0, w2, a0, a2, g2, k_k, k_a, r_k)
    out = jax.ShapeDtypeStruct((b, s, d), F32)
    lora_col = (3 * d + 2 * d) // lora_w
    return pl.pallas_call(
        functools.partial(_rw_chunk_kernel, nsteps=nc),
        grid=(b, nc + 1),
        in_specs=[cur(0, d), cur(1, d), cur(2, d), cur(lora_col, lora_w)] + [full(a) for a in params],
        out_specs=[prev, cur(0, d), cur(0, d)],
        out_shape=[out, out, out],
        scratch_shapes=[pltpu.VMEM((d // (2 * RW_HD), 2 * RW_HD, 2 * RW_HD), F32),
                        pltpu.VMEM((c, d), F32), pltpu.VMEM((c, d), F32),
                        pltpu.VMEM((c, d), BF16), pltpu.VMEM((c, d), F32),
                        pltpu.VMEM((4, d), F32)],
        compiler_params=_cparams(("parallel", "arbitrary")),
        name="rwkv_chunks",
    )(rest, rest, rest, rest, *params)


def _mix_kernel(oda_ref, yrw_ref, bon_ref, gout_ref, ga_ref, gb_ref, x_ref, wda_ref, wrw_ref, wmix_ref,
                gng_ref, gnb_ref, lng_ref, lnb_ref, o_ref, *, alpha):
    y_da = jnp.dot(oda_ref[...], wda_ref[...], preferred_element_type=F32)
    _, seg_ones = _pair_consts()
    parts = []
    for hp in range(RW_HEADS // 2):
        ln = slice(hp * 2 * RW_HD, (hp + 1) * 2 * RW_HD)
        y = yrw_ref[:, ln]
        mu = _dot(y, seg_ones) * (1.0 / RW_HD)
        yc = y - mu
        var = _dot(yc * yc, seg_ones) * (1.0 / RW_HD)
        parts.append(yc * lax.rsqrt(var + RW_LNX_EPS))
    yn = jnp.concatenate(parts, axis=1) * gng_ref[...] + gnb_ref[...] + bon_ref[...]
    y_rw = jnp.dot((yn * gout_ref[...]).astype(BF16), wrw_ref[...], preferred_element_type=F32)
    mixed = jax.nn.sigmoid(ga_ref[...]) * y_da + jax.nn.sigmoid(gb_ref[...]) * y_rw
    z = alpha * x_ref[...] + jnp.dot(mixed.astype(BF16), wmix_ref[...], preferred_element_type=F32)
    o_ref[...] = _layer_norm_rows(z, lng_ref[...], lnb_ref[...])


def _mix(o_da, y_rw, bonus, g_out, rest, x, w_da, w_rw, w_mix, gn_g, gn_b, ln_g, ln_b, alpha, tm=512):
    t, d = x.shape
    rowb = lambda c: pl.BlockSpec((tm, d), lambda i: (i, c))
    full = lambda a: pl.BlockSpec(a.shape, lambda i: (0,) * a.ndim)
    return pl.pallas_call(
        functools.partial(_mix_kernel, alpha=alpha),
        grid=(t // tm,),
        in_specs=[rowb(0), rowb(0), rowb(0), rowb(0), rowb(3), rowb(4), rowb(0),
                  full(w_da), full(w_rw), full(w_mix), full(gn_g), full(gn_b), full(ln_g), full(ln_b)],
        out_specs=rowb(0),
        out_shape=jax.ShapeDtypeStruct((t, d), F32),
        compiler_params=_cparams(("parallel",)),
        name="mix_merge_ln1",
    )(o_da, y_rw, bonus, g_out, rest, rest, x, w_da, w_rw, w_mix, gn_g, gn_b, ln_g, ln_b)


def _mem_kv_kernel(mem_ref, g_ref, b_ref, w_ref, ck_ref, cv_ref):
    m = _layer_norm_rows(mem_ref[0], g_ref[...], b_ref[...])
    kv = jnp.dot(m.astype(BF16), w_ref[...], preferred_element_type=F32)
    ck_ref[0] = kv[:, :D_MODEL].astype(BF16)
    cv_ref[0] = kv[:, D_MODEL:].astype(BF16)


def _mem_kv(mem, g, b, w_ckv):
    bsz, m, d = mem.shape
    full = lambda a: pl.BlockSpec(a.shape, lambda i: (0,) * a.ndim)
    blk = pl.BlockSpec((1, m, d), lambda i: (i, 0, 0))
    return pl.pallas_call(
        _mem_kv_kernel,
        grid=(bsz,),
        in_specs=[blk, full(g), full(b), full(w_ckv)],
        out_specs=[blk, blk],
        out_shape=[jax.ShapeDtypeStruct((bsz, m, d), BF16)] * 2,
        compiler_params=_cparams(("parallel",)),
        name="mem_kv",
    )(mem, g, b, w_ckv)


def _cross_kernel(x_ref, ck_ref, cv_ref, wq_ref, wo_ref, lng_ref, lnb_ref, o_ref, *, alpha):
    x = x_ref[...]
    cq = jnp.dot(x.astype(BF16), wq_ref[...], preferred_element_type=F32)
    cq = (cq * (CA_HD ** -0.5)).astype(BF16)
    outs = []
    for h in range(CA_HEADS):
        sl = slice(h * CA_HD, (h + 1) * CA_HD)
        s = lax.dot_general(cq[:, sl], ck_ref[0, :, sl], NT_DIMS, preferred_element_type=F32)
        s = s - jnp.max(s, axis=-1, keepdims=True)
        p = jnp.exp(s)
        p = p / jnp.sum(p, axis=-1, keepdims=True)
        outs.append(jnp.dot(p.astype(BF16), cv_ref[0, :, sl], preferred_element_type=F32))
    co = jnp.dot(jnp.concatenate(outs, axis=1).astype(BF16), wo_ref[...], preferred_element_type=F32)
    o_ref[...] = _layer_norm_rows(alpha * x + co, lng_ref[...], lnb_ref[...])


def _cross_attention(x, ck, cv, w_cq, w_co, ln_g, ln_b, seq, alpha, tm=512):
    t, d = x.shape
    tiles_per_seq = seq // tm
    full = lambda a: pl.BlockSpec(a.shape, lambda i: (0,) * a.ndim)
    rowb = pl.BlockSpec((tm, d), lambda i: (i, 0))
    memb = pl.BlockSpec((1,) + ck.shape[1:], lambda i: (i // tiles_per_seq, 0, 0))
    return pl.pallas_call(
        functools.partial(_cross_kernel, alpha=alpha),
        grid=(t // tm,),
        in_specs=[rowb, memb, memb, full(w_cq), full(w_co), full(ln_g), full(ln_b)],
        out_specs=rowb,
        out_shape=jax.ShapeDtypeStruct((t, d), F32),
        compiler_params=_cparams(("parallel",)),
        name="cross_attention_ln2",
    )(x, ck, cv, w_cq, w_co, ln_g, ln_b)


def _ffn_kernel(x_ref, win_ref, wout_ref, lng_ref, lnb_ref, o_ref, *, alpha, tf):
    x = x_ref[...]
    xb = x.astype(BF16)
    acc = alpha * x
    for j in range(D_FF // tf):
        gate = jnp.dot(xb, win_ref[:, j * tf:(j + 1) * tf], preferred_element_type=F32)
        up = jnp.dot(xb, win_ref[:, D_FF + j * tf:D_FF + (j + 1) * tf], preferred_element_type=F32)
        h = (gate * jax.nn.sigmoid(gate) * up).astype(BF16)
        acc = acc + jnp.dot(h, wout_ref[j * tf:(j + 1) * tf, :], preferred_element_type=F32)
    o_ref[...] = _layer_norm_rows(acc, lng_ref[...], lnb_ref[...])


def _ffn(x, w_in, w_out, ln_g, ln_b, alpha, tm=512, tf=1408):
    t, d = x.shape
    rowb = pl.BlockSpec((tm, d), lambda i: (i, 0))
    return pl.pallas_call(
        functools.partial(_ffn_kernel, alpha=alpha, tf=tf),
        grid=(t // tm,),
        in_specs=[rowb, _resident(w_in), _resident(w_out), _resident(ln_g), _resident(ln_b)],
        out_specs=rowb,
        out_shape=jax.ShapeDtypeStruct((t, d), F32),
        compiler_params=_cparams(("parallel",)),
        name="swiglu_ffn_ln3",
    )(x, w_in, w_out, ln_g, ln_b)


def _pad_cols(a, width):
    return jnp.pad(a, ((0, 0), (0, width - a.shape[1])))


def _pad_rows(a, height):
    return jnp.pad(a, ((0, height - a.shape[0]), (0, 0)))


def kernel(x, mem, rel_bias, w_in, shift_mu, lambda_q1, lambda_k1, lambda_q2, lambda_k2, da_subln_g, w_da_proj, rw_w0, rw_w2, rw_a0, rw_a2, rw_g2, rw_k_k, rw_k_a, rw_r_k, rw_lnx_g, rw_lnx_b, w_rw_proj, w_mix_out, ln1_g, ln1_b, mem_ln_g, mem_ln_b, w_cq, w_ckv, w_co, ln2_g, ln2_b, w_ffn_in, w_ffn_out, ln3_g, ln3_b):
    bsz, seq, d = x.shape
    depth = w_in.shape[0]
    alpha = (2.0 * depth) ** 0.25
    t = bsz * seq
    row = lambda a: a.reshape(1, -1).astype(F32)
    bias_tiles = _attn_flat_bias(rel_bias)
    xf = x.reshape(t, d)

    qkv_w = 3 * D_MODEL
    rw0 = qkv_w
    lo0 = rw0 + 3 * D_MODEL
    gate0 = lo0 + DECAY_LORA + AAA_LORA + GATE_LORA

    for l in range(depth):
        w = w_in[l]
        w_qt = w[:, :D_MODEL].T.astype(BF16)
        w_k = w[:, D_MODEL:2 * D_MODEL].astype(BF16)
        w_vt = w[:, 2 * D_MODEL:qkv_w].T.astype(BF16)
        w_rest = jnp.concatenate([
            w[:, rw0:lo0], w[:, gate0:gate0 + 2 * D_MODEL],
            _pad_cols(w[:, lo0:lo0 + DECAY_LORA], 128),
            _pad_cols(w[:, lo0 + DECAY_LORA:lo0 + DECAY_LORA + AAA_LORA], 128),
            _pad_cols(w[:, lo0 + DECAY_LORA + AAA_LORA:gate0], 256)], axis=1).astype(BF16)
        mu = shift_mu[l].astype(F32)
        mu_rkv = mu[:3 * D_MODEL].reshape(3, D_MODEL)
        o1 = 3 * D_MODEL
        mu_lo = jnp.concatenate([
            _pad_cols(mu[None, o1:o1 + DECAY_LORA], 128),
            _pad_cols(mu[None, o1 + DECAY_LORA:o1 + DECAY_LORA + AAA_LORA], 128),
            _pad_cols(mu[None, o1 + DECAY_LORA + AAA_LORA:], 256)], axis=1)
        w2 = _pad_rows(rw_w2[l], 128).astype(BF16)
        a2 = _pad_rows(rw_a2[l], 128).astype(BF16)
        g2 = _pad_rows(rw_g2[l], 256).astype(BF16)

        qt, kda, vt = _qkv_proj(xf, w_qt, w_k, w_vt, bsz, seq)
        rest = _matmul(xf, w_rest, F32, 512, 1408, "in_proj_rest")

        lam_init = 0.8 - 0.6 * math.exp(-0.3 * l)
        lamv = jnp.stack([lambda_q1[l], lambda_k1[l], lambda_q2[l], lambda_k2[l]]).astype(F32)
        o_da = _diff_attention_flat(qt, kda.reshape(bsz, seq, d), vt, lamv, bias_tiles, row(da_subln_g[l]),
                                    lam_init)

        y_rw, bon, g_out = _rw_chunks(
            rest.reshape(bsz, seq, rest.shape[1]), mu_rkv, mu_lo, row(rw_w0[l]), w2, row(rw_a0[l]),
            a2, g2, row(rw_k_k[l]), row(rw_k_a[l]), row(rw_r_k[l]))

        x1 = _mix(o_da.reshape(t, d), y_rw.reshape(t, d), bon.reshape(t, d), g_out.reshape(t, d), rest, xf,
                  w_da_proj[l].astype(BF16), w_rw_proj[l].astype(BF16), w_mix_out[l].astype(BF16),
                  row(rw_lnx_g[l]), row(rw_lnx_b[l]), row(ln1_g[l]), row(ln1_b[l]), alpha)

        ck, cv = _mem_kv(mem, row(mem_ln_g[l]), row(mem_ln_b[l]), w_ckv[l].astype(BF16))
        x2 = _cross_attention(x1, ck, cv, w_cq[l].astype(BF16), w_co[l].astype(BF16),
                              row(ln2_g[l]), row(ln2_b[l]), seq, alpha)

        xf = _ffn(x2, w_ffn_in[l].astype(BF16), w_ffn_out[l].astype(BF16),
                  row(ln3_g[l]), row(ln3_b[l]), alpha)
    return xf.reshape(bsz, seq, d)
```

```python
import functools
import math

import jax
import jax.numpy as jnp
from jax import lax
from jax.experimental import pallas as pl
from jax.experimental.pallas import tpu as pltpu

F32 = jnp.float32
BF16 = jnp.bfloat16

D_MODEL = 1024
DA_HEADS = 8
DA_HD = 64
NUM_BUCKETS = 32
MAX_DISTANCE = 128
RW_HD = 64
RW_HEADS = D_MODEL // RW_HD
DECAY_LORA = 64
AAA_LORA = 64
GATE_LORA = 160
RW_LNX_EPS = 64e-5
MEM_TOKENS = 256
CA_HEADS = 4
CA_HD = D_MODEL // CA_HEADS
D_FF = 2816
LN_EPS = 1e-5
NEG_BIG = -1e30

LOG2E = math.log2(math.e)
ATT_HEADS_PER_STEP = 2
ATT_QT = 512
ATT_KB = 256
ATT_BIAS_TYPES = 4
ATT_ROTATE = 3
ATT_ONES_ROWS = 16
RW_CHUNK = 64
RW_STEP_CHUNKS = 2
VMEM_LIMIT = 56 * 1024 * 1024

NT_DIMS = (((1,), (1,)), ((), ()))
TN_DIMS = (((0,), (0,)), ((), ()))


def _cparams(sem, vmem=VMEM_LIMIT):
    return pltpu.CompilerParams(dimension_semantics=sem, vmem_limit_bytes=vmem)


def _split2(a):
    hi = a.astype(BF16)
    lo = (a - hi.astype(F32)).astype(BF16)
    return hi, lo


def _dot3(a, b, dims=(((1,), (0,)), ((), ()))):
    ah, al = _split2(a)
    bh, bl = _split2(b)
    d = functools.partial(lax.dot_general, dimension_numbers=dims, preferred_element_type=F32)
    return d(ah, bh) + (d(ah, bl) + d(al, bh))


def _layer_norm_rows(z, g, b):
    mu = jnp.mean(z, axis=-1, keepdims=True)
    zc = z - mu
    var = jnp.mean(zc * zc, axis=-1, keepdims=True)
    return zc * lax.rsqrt(var + LN_EPS) * g + b


def _resident(a):
    return pl.BlockSpec(a.shape, lambda *_: (0,) * a.ndim, pipeline_mode=pl.Buffered(1))


def _mm_kernel(x_ref, w_ref, o_ref, *, tn):
    xb = x_ref[...].astype(BF16)
    for j in range(w_ref.shape[1] // tn):
        cols = slice(j * tn, (j + 1) * tn)
        o_ref[:, cols] = jnp.dot(xb, w_ref[:, cols], preferred_element_type=F32).astype(o_ref.dtype)


def _matmul(x, w, out_dtype, tm, tn, name):
    t, k = x.shape
    n = w.shape[1]
    return pl.pallas_call(
        functools.partial(_mm_kernel, tn=tn),
        grid=(t // tm,),
        in_specs=[pl.BlockSpec((tm, k), lambda i: (i, 0)), _resident(w)],
        out_specs=pl.BlockSpec((tm, n), lambda i: (i, 0)),
        out_shape=jax.ShapeDtypeStruct((t, n), out_dtype),
        compiler_params=_cparams(("parallel",)),
        name=name,
    )(x, w)


def _qkv_kernel(x_ref, wqt_ref, wk_ref, wvt_ref, qt_ref, k_ref, vt_ref):
    xb = x_ref[...].astype(BF16)
    k_ref[...] = jnp.dot(xb, wk_ref[...], preferred_element_type=F32).astype(BF16)
    qt = lax.dot_general(wqt_ref[...], xb, NT_DIMS, preferred_element_type=F32)
    qt_ref[0] = (qt * (DA_HD ** -0.5 * LOG2E)).astype(BF16)
    vt_ref[0] = lax.dot_general(wvt_ref[...], xb, NT_DIMS, preferred_element_type=F32).astype(BF16)


def _qkv_proj(x, wqt, wk, wvt, bsz, seq, tm=512):
    t, d = x.shape
    tiles_per_seq = seq // tm
    full = lambda a: pl.BlockSpec(a.shape, lambda i: (0,) * a.ndim)
    tblk = pl.BlockSpec((1, d, tm), lambda i: (i // tiles_per_seq, 0, i % tiles_per_seq))
    rowb = pl.BlockSpec((tm, d), lambda i: (i, 0))
    tshape = jax.ShapeDtypeStruct((bsz, d, seq), BF16)
    return pl.pallas_call(
        _qkv_kernel,
        grid=(t // tm,),
        in_specs=[rowb, full(wqt), full(wk), full(wvt)],
        out_specs=[tblk, rowb, tblk],
        out_shape=[tshape, jax.ShapeDtypeStruct((t, d), BF16), tshape],
        compiler_params=_cparams(("parallel",)),
        name="in_proj_qkv",
    )(x, wqt, wk, wvt)


def _t5_bucket(rel):
    n = jnp.maximum(rel, 0)
    max_exact = NUM_BUCKETS // 2
    nf = jnp.maximum(n, 1).astype(jnp.float32)
    large = max_exact + jnp.floor(jnp.log(nf / max_exact) / math.log(MAX_DISTANCE / max_exact)
                                  * (NUM_BUCKETS - max_exact)).astype(jnp.int32)
    large = jnp.minimum(large, NUM_BUCKETS - 1)
    return jnp.where(n < max_exact, n, large)


def _col_max(s):
    while s.shape[0] > 8:
        half = s.shape[0] // 2
        s = jnp.maximum(s[:half], s[half:])
    return jnp.max(s, axis=0, keepdims=True)


def _attn_unit_tables(seq):
    kb_per_qt = ATT_QT // ATT_KB
    qts, kbs, types = [], [], []
    for qt in range(seq // ATT_QT):
        nkb = (qt + 1) * kb_per_qt
        for kb in range(nkb):
            from_end = nkb - 1 - kb
            qts.append(qt)
            kbs.append(kb)
            types.append(ATT_BIAS_TYPES - 1 - from_end if from_end < ATT_BIAS_TYPES - 1 else 0)
    assert len(qts) % ATT_ROTATE == 0
    nunits = len(qts)
    for _ in range(ATT_ROTATE - 1):
        qts.append(0)
        kbs.append(0)
        types.append(0)
    as_i32 = lambda v: jnp.asarray(v, jnp.int32)
    return nunits, as_i32(qts), as_i32(kbs), as_i32(types)


def _attn_flat_bias(rel_bias):
    tab = rel_bias.astype(F32) - rel_bias[NUM_BUCKETS - 1].astype(F32)[None, :]
    base = jnp.arange(ATT_QT)[None, :] - jnp.arange(ATT_KB)[:, None]

    def tile(key_offset):
        rel = base - key_offset
        onehot = (_t5_bucket(rel)[:, :, None] == jnp.arange(NUM_BUCKETS)[None, None, :]).astype(F32)
        vals = jnp.einsum('kqb,bh->hkq', onehot, tab, precision=lax.Precision.HIGHEST)
        return jnp.where((rel >= 0)[None], vals, NEG_BIG)

    zeros = jnp.zeros((rel_bias.shape[1], ATT_KB, ATT_QT), F32)
    return jnp.stack([zeros, tile(-ATT_KB), tile(0), tile(ATT_KB)], axis=1) * LOG2E


def _attn_flat_kernel(tq_ref, tk_ref, tt_ref, lamv_ref, qt_ref, k_ref, vt_ref, bias_ref, g_ref, o_ref,
                      m_sc, acc_sc, sa_sc, sb_sc, sc_sc, ma_sc, mb_sc, mc_sc, *, lam_init, nunits):
    hw = 2 * DA_HD
    heads = range(ATT_HEADS_PER_STEP)
    hrows = [slice(hh * hw, (hh + 1) * hw) for hh in heads]
    ntiles = m_sc.shape[0]

    lv = lamv_ref[...]
    lam = (jnp.exp(jnp.sum(lv[0:1] * lv[1:2], axis=-1, keepdims=True))
           - jnp.exp(jnp.sum(lv[2:3] * lv[3:4], axis=-1, keepdims=True)) + lam_init)

    m_sc[...] = jnp.full(m_sc.shape, NEG_BIG, F32)
    acc_sc[...] = jnp.zeros(acc_sc.shape, F32)
    rowi = lax.broadcasted_iota(jnp.int32, (hw, ATT_QT), 0)

    def fill(buf, mbuf, u):
        q0 = pl.multiple_of(tq_ref[u] * ATT_QT, ATT_QT)
        k0 = pl.multiple_of(tk_ref[u] * ATT_KB, ATT_KB)
        ty = tt_ref[u]
        s = []
        for hh in heads:
            q = qt_ref[0, hrows[hh], pl.ds(q0, ATT_QT)]
            zero = jnp.zeros_like(q)
            qq = jnp.concatenate([jnp.where(rowi < DA_HD, q, zero), jnp.where(rowi >= DA_HD, q, zero)], axis=1)
            s.append(jnp.dot(k_ref[0, pl.ds(k0, ATT_KB), hrows[hh]], qq, preferred_element_type=F32))
        for hh in heads:
            b = bias_ref[hh, ty]
            sb = s[hh] + jnp.concatenate([b, b], axis=1)
            buf[hh] = sb
            mbuf[hh] = _col_max(sb)

    def drain(buf, mbuf, u):
        t = tq_ref[u]
        k0 = pl.multiple_of(tk_ref[u] * ATT_KB, ATT_KB)
        m_prev = [m_sc[t, hh] for hh in heads]
        m_new = [jnp.maximum(m_prev[hh], mbuf[hh]) for hh in heads]
        alpha = [jnp.exp2(m_prev[hh] - m_new[hh]) for hh in heads]
        p = [jnp.exp2((buf[hh] - m_new[hh]).astype(BF16)) for hh in heads]
        ones = jnp.ones((ATT_ONES_ROWS, ATT_KB), BF16)
        pv = [jnp.dot(jnp.concatenate([vt_ref[0, hrows[hh], pl.ds(k0, ATT_KB)], ones], axis=0), p[hh],
                      preferred_element_type=F32) for hh in heads]
        for hh in heads:
            acc_sc[t, hh] = alpha[hh] * acc_sc[t, hh] + pv[hh]
            m_sc[t, hh] = m_new[hh]

    fill(sa_sc, ma_sc, 0)
    fill(sb_sc, mb_sc, 1)

    def trip(i, carry):
        u = i * ATT_ROTATE
        fill(sc_sc, mc_sc, u + 2)
        drain(sa_sc, ma_sc, u)
        fill(sa_sc, ma_sc, u + 3)
        drain(sb_sc, mb_sc, u + 1)
        fill(sb_sc, mb_sc, u + 4)
        drain(sc_sc, mc_sc, u + 2)
        return carry

    lax.fori_loop(0, nunits // ATT_ROTATE, trip, 0)

    def finish(t, carry):
        q0 = pl.multiple_of(t * ATT_QT, ATT_QT)
        for hh in heads:
            a = acc_sc[t, hh]
            acc = a[0:hw, :] * (1.0 / a[hw:hw + 1, :])
            ot = acc[:, 0:ATT_QT] - lam * acc[:, ATT_QT:2 * ATT_QT]
            ms = jnp.mean(ot * ot, axis=0, keepdims=True)
            ot = ot * (lax.rsqrt(ms + LN_EPS) * (1.0 - lam_init))
            o_ref[0, pl.ds(q0, ATT_QT), hrows[hh]] = (ot.T * g_ref[...]).astype(o_ref.dtype)
        return carry

    lax.fori_loop(0, ntiles, finish, 0)


def _diff_attention_flat(qt, k, vt, lamv, bias, subln_g, lam_init):
    b, s, _ = k.shape
    nh = ATT_HEADS_PER_STEP
    hw = 2 * DA_HD
    nunits, tq, tk, tt = _attn_unit_tables(s)
    score = pltpu.VMEM((nh, ATT_KB, 2 * ATT_QT), F32)
    smax = pltpu.VMEM((nh, 1, 2 * ATT_QT), F32)
    grid_spec = pltpu.PrefetchScalarGridSpec(
        num_scalar_prefetch=3,
        grid=(b, DA_HEADS // nh),
        in_specs=[
            pl.BlockSpec((4, DA_HD), lambda bi, hi, *_: (0, 0)),
            pl.BlockSpec((1, nh * hw, s), lambda bi, hi, *_: (bi, hi, 0)),
            pl.BlockSpec((1, s, nh * hw), lambda bi, hi, *_: (bi, 0, hi)),
            pl.BlockSpec((1, nh * hw, s), lambda bi, hi, *_: (bi, hi, 0)),
            pl.BlockSpec((nh, ATT_BIAS_TYPES, ATT_KB, ATT_QT), lambda bi, hi, *_: (hi, 0, 0, 0)),
            pl.BlockSpec((1, hw), lambda bi, hi, *_: (0, 0)),
        ],
        out_specs=pl.BlockSpec((1, s, nh * hw), lambda bi, hi, *_: (bi, 0, hi)),
        scratch_shapes=[pltpu.VMEM((s // ATT_QT, nh, 1, 2 * ATT_QT), F32),
                        pltpu.VMEM((s // ATT_QT, nh, hw + ATT_ONES_ROWS, 2 * ATT_QT), F32),
                        score, score, score, smax, smax, smax],
    )
    return pl.pallas_call(
        functools.partial(_attn_flat_kernel, lam_init=lam_init, nunits=nunits),
        grid_spec=grid_spec,
        out_shape=jax.ShapeDtypeStruct((b, s, DA_HEADS * hw), BF16),
        compiler_params=_cparams(("parallel", "parallel")),
        name="diff_attention",
    )(tq, tk, tt, lamv, qt, k, vt, bias, subln_g)


def _dot(a, b, dims=(((1,), (0,)), ((), ()))):
    return lax.dot_general(a.astype(BF16), b.astype(BF16), dims, preferred_element_type=F32)


def _dot_split_lhs_rhs(a_bf16, b):
    bh, bl = _split2(b)
    return (jnp.dot(a_bf16, bh, preferred_element_type=F32) + jnp.dot(a_bf16, bl, preferred_element_type=F32))


def _pair_consts():
    c, n = RW_CHUNK, RW_HD
    lane = lax.broadcasted_iota(jnp.int32, (c, 2 * n), 1)
    row = lax.broadcasted_iota(jnp.int32, (2 * c, 2 * c), 0)
    col = lax.broadcasted_iota(jnp.int32, (2 * c, 2 * c), 1)
    same_head = (row // n) == (col // n)
    return lane < n, same_head.astype(BF16)


def _expand(x2, head_a):
    zero = jnp.zeros_like(x2)
    return jnp.concatenate([jnp.where(head_a, x2, zero), jnp.where(head_a, zero, x2)], axis=0)


def _compact(xm):
    return xm[0:RW_CHUNK] + xm[RW_CHUNK:2 * RW_CHUNK]


def _rw_chunk_kernel(r_ref, k_ref, v_ref, lo_ref, mu_ref, mulo_ref, w0_ref, w2_ref,
                     a0_ref, a2_ref, g2_ref, kk_ref, ka_ref, rk_ref,
                     y_ref, bon_ref, g_ref,
                     z_sc, pt_sc, qt_sc, rh_sc, y0_sc, prev_sc, *, nsteps):
    c, n = RW_CHUNK, RW_HD
    pw = 2 * n
    pairs = range(z_sc.shape[0])
    lns = [slice(hp * pw, (hp + 1) * pw) for hp in pairs]
    rows = RW_STEP_CHUNKS * c
    rss = [slice(j * c, (j + 1) * c) for j in range(RW_STEP_CHUNKS)]
    head_a, seg_ones = _pair_consts()

    @pl.when(pl.program_id(1) == 0)
    def _():
        z_sc[...] = jnp.zeros(z_sc.shape, F32)
        pt_sc[...] = jnp.zeros(pt_sc.shape, F32)
        qt_sc[...] = jnp.zeros(qt_sc.shape, F32)
        rh_sc[...] = jnp.zeros(rh_sc.shape, rh_sc.dtype)
        y0_sc[...] = jnp.zeros(y0_sc.shape, F32)
        prev_sc[...] = jnp.zeros(prev_sc.shape, F32)

    z = [z_sc[hp] for hp in pairs]
    for rs in rss:
        for hp in pairs:
            y_ref[0, rs, lns[hp]] = _dot(rh_sc[rs, lns[hp]], z[hp]) + y0_sc[rs, lns[hp]]
        z = [_dot3(_expand(pt_sc[rs, lns[hp]], head_a), z[hp]) + _expand(qt_sc[rs, lns[hp]], head_a)
             for hp in pairs]
    for hp in pairs:
        z_sc[hp] = z[hp]

    row = lax.broadcasted_iota(jnp.int32, (2 * c, 2 * c), 0)
    col = lax.broadcasted_iota(jnp.int32, (2 * c, 2 * c), 1)
    strict = (row % c) > (col % c)
    incl = (row % c) >= (col % c)
    trow = lax.broadcasted_iota(jnp.int32, (c, c), 0)
    tcol = lax.broadcasted_iota(jnp.int32, (c, c), 1)
    tri_ones = (trow >= tcol).astype(BF16)
    diag2 = lax.broadcasted_iota(jnp.int32, (c, pw), 0) == (lax.broadcasted_iota(jnp.int32, (c, pw), 1) % n)
    d = functools.partial(jnp.dot, preferred_element_type=F32)

    def cumsum(lw2):
        return _dot_split_lhs_rhs(tri_ones, lw2)

    ci = pl.program_id(1)
    trow1 = lax.broadcasted_iota(jnp.int32, (rows, 1), 0)
    keep_prev = ci < nsteps - 1

    def shifted(cur_ref, slot, mu):
        p = cur_ref[0]
        w = p.shape[1]
        before = prev_sc[slot:slot + 1, 0:w]
        pm1 = jnp.where(trow1 == 0, before, pltpu.roll(p, 1, 0))
        prev_sc[slot:slot + 1, 0:w] = jnp.where(keep_prev, p[rows - 1:rows, :], before)
        return p + mu * (pm1 - p)

    r = shifted(r_ref, 0, mu_ref[0:1])
    k = shifted(k_ref, 1, mu_ref[1:2])
    v = shifted(v_ref, 2, mu_ref[2:3])
    lo = shifted(lo_ref, 3, mulo_ref[...])

    nz = -(w0_ref[...] + d(jnp.tanh(lo[:, 0:128]).astype(BF16), w2_ref[...]))
    softplus = jnp.maximum(nz, 0.0) + jnp.log(1.0 + jnp.exp(-jnp.abs(nz)))
    lw = -jnp.exp(-softplus - 0.5)
    a_ic = jax.nn.sigmoid(a0_ref[...] + d(lo[:, 128:256].astype(BF16), a2_ref[...]))
    g_ref[0] = d(jax.nn.sigmoid(lo[:, 256:512]).astype(BF16), g2_ref[...])
    kk = k * kk_ref[...]
    km = k * (1.0 + (a_ic - 1.0) * ka_ref[...])

    def intra(units):
        gpairs = range(len(units))
        cum = [cumsum(lw[rs, ln]) for rs, ln in units]
        nrm = [jnp.sqrt(_dot(kk[rs, ln] * kk[rs, ln], seg_ones)) for rs, ln in units]
        big, atm, vm, bdm, kdm = [], [], [], [], []
        for hp, (rs, ln) in enumerate(units):
            lw2, cm = lw[rs, ln], cum[hp]
            g_inv = jnp.exp(-cm)
            g_rest = jnp.exp(cm[c - 1:c, :] - cm)
            kkn = kk[rs, ln] / jnp.maximum(nrm[hp], 1e-12)
            b2 = kkn * a_ic[rs, ln]
            km2 = km[rs, ln]
            a_m = _expand(-kkn * jnp.exp(cm - lw2), head_a)
            r_m = _expand(r[rs, ln] * jnp.exp(cm), head_a)
            b_m = _expand(b2 * g_inv, head_a)
            k_m = _expand(km2 * g_inv, head_a)
            atm.append(a_m)
            bdm.append(_expand(b2 * g_rest, head_a))
            kdm.append(_expand(km2 * g_rest, head_a))
            vm.append(_expand(v[rs, ln], head_a))
            big.append(_dot(jnp.concatenate([a_m, r_m], axis=0), jnp.concatenate([b_m, k_m], axis=0), NT_DIMS))
        lmat = [jnp.where(strict, bg[0:pw, 0:pw], 0.0) for bg in big]
        m_rb = [jnp.where(incl, bg[pw:2 * pw, 0:pw], 0.0) for bg in big]
        mv = [_dot(jnp.concatenate([jnp.where(strict, bg[0:pw, pw:2 * pw], 0.0),
                                    jnp.where(incl, bg[pw:2 * pw, pw:2 * pw], 0.0)], axis=0), vm[hp])
              for hp, bg in enumerate(big)]
        x = [jnp.concatenate([atm[hp], mv[hp][0:pw]], axis=1) for hp in gpairs]
        lp = lmat
        for it in range(6):
            x = [x[hp] + _dot(lp[hp], x[hp]) for hp in gpairs]
            if it < 5:
                lp = [_dot(m, m) for m in lp]
        rb = [_dot(m_rb[hp], x[hp]) for hp in gpairs]
        bx = [_dot(bdm[hp], x[hp], TN_DIMS) for hp in gpairs]
        kv = [_dot(kdm[hp], vm[hp], TN_DIMS) for hp in gpairs]
        for hp, (rs, ln) in enumerate(units):
            g_last = jnp.exp(cum[hp][c - 1:c, :])
            pt_sc[rs, ln] = _compact(bx[hp][:, 0:pw]) + jnp.where(diag2, g_last, 0.0)
            qt_sc[rs, ln] = _compact(bx[hp][:, pw:2 * pw] + kv[hp])
            rh_sc[rs, ln] = (r[rs, ln] * jnp.exp(cum[hp]) + _compact(rb[hp][:, 0:pw])).astype(rh_sc.dtype)
            y0_sc[rs, ln] = _compact(rb[hp][:, pw:2 * pw] + mv[hp][pw:2 * pw])
            bon_ref[0, rs, ln] = _dot(r[rs, ln] * km[rs, ln] * rk_ref[:, ln], seg_ones) * v[rs, ln]

    intra([(rs, ln) for rs in rss for ln in lns])


def _rw_chunks(rest, mu_rkv, mu_lo, w0, w2, a0, a2, g2, k_k, k_a, r_k):
    b, s, _ = rest.shape
    d = D_MODEL
    c = RW_STEP_CHUNKS * RW_CHUNK
    nc = s // c
    lora_w = mu_lo.shape[1]
    cur = lambda col, w: pl.BlockSpec((1, c, w), lambda bi, ci: (bi, jnp.minimum(ci, nc - 1), col))
    prev = pl.BlockSpec((1, c, d), lambda bi, ci: (bi, jnp.maximum(ci - 1, 0), 0))
    full = lambda a: pl.BlockSpec(a.shape, lambda bi, ci: (0,) * a.ndim)
    params = (mu_rkv, mu_lo, w0, w2, a0, a2, g2, k_k, k_a, r_k)
    out = jax.ShapeDtypeStruct((b, s, d), F32)
    lora_col = (3 * d + 2 * d) // lora_w
    return pl.pallas_call(
        functools.partial(_rw_chunk_kernel, nsteps=nc),
        grid=(b, nc + 1),
        in_specs=[cur(0, d), cur(1, d), cur(2, d), cur(lora_col, lora_w)] + [full(a) for a in params],
        out_specs=[prev, cur(0, d), cur(0, d)],
        out_shape=[out, out, out],
        scratch_shapes=[pltpu.VMEM((d // (2 * RW_HD), 2 * RW_HD, 2 * RW_HD), F32),
                        pltpu.VMEM((c, d), F32), pltpu.VMEM((c, d), F32),
                        pltpu.VMEM((c, d), BF16), pltpu.VMEM((c, d), F32),
                        pltpu.VMEM((4, d), F32)],
        compiler_params=_cparams(("parallel", "arbitrary")),
        name="rwkv_chunks",
    )(rest, rest, rest, rest, *params)


def _mix_kernel(oda_ref, yrw_ref, bon_ref, gout_ref, ga_ref, gb_ref, x_ref, wda_ref, wrw_ref, wmix_ref,
                gng_ref, gnb_ref, lng_ref, lnb_ref, o_ref, *, alpha):
    y_da = jnp.dot(oda_ref[...], wda_ref[...], preferred_element_type=F32)
    _, seg_ones = _pair_consts()
    parts = []
    for hp in range(RW_HEADS // 2):
        ln = slice(hp * 2 * RW_HD, (hp + 1) * 2 * RW_HD)
        y = yrw_ref[:, ln]
        mu = _dot(y, seg_ones) * (1.0 / RW_HD)
        yc = y - mu
        var = _dot(yc * yc, seg_ones) * (1.0 / RW_HD)
        parts.append(yc * lax.rsqrt(var + RW_LNX_EPS))
    yn = jnp.concatenate(parts, axis=1) * gng_ref[...] + gnb_ref[...] + bon_ref[...]
    y_rw = jnp.dot((yn * gout_ref[...]).astype(BF16), wrw_ref[...], preferred_element_type=F32)
    mixed = jax.nn.sigmoid(ga_ref[...]) * y_da + jax.nn.sigmoid(gb_ref[...]) * y_rw
    z = alpha * x_ref[...] + jnp.dot(mixed.astype(BF16), wmix_ref[...], preferred_element_type=F32)
    o_ref[...] = _layer_norm_rows(z, lng_ref[...], lnb_ref[...])


def _mix_cross_kernel(*refs, alpha):
    mix_in, cross_in, o_ref, x1_sc = refs[:14], refs[14:20], refs[20], refs[21]
    _mix_kernel(*mix_in, x1_sc, alpha=alpha)
    _cross_kernel(x1_sc, *cross_in, o_ref, alpha=alpha)


def _mix_cross(o_da, y_rw, bonus, g_out, rest, x, w_da, w_rw, w_mix, gn_g, gn_b, ln_g, ln_b,
               ck, cv, w_cq, w_co, ln2_g, ln2_b, seq, alpha, tm=256):
    t, d = x.shape
    tiles_per_seq = seq // tm
    rowb = lambda c: pl.BlockSpec((tm, d), lambda i: (i, c))
    memb = pl.BlockSpec((1,) + ck.shape[1:], lambda i: (i // tiles_per_seq, 0, 0))
    consts_mix = (w_da, w_rw, w_mix, gn_g, gn_b, ln_g, ln_b)
    consts_cross = (w_cq, w_co, ln2_g, ln2_b)
    return pl.pallas_call(
        functools.partial(_mix_cross_kernel, alpha=alpha),
        grid=(t // tm,),
        in_specs=[rowb(0), rowb(0), rowb(0), rowb(0), rowb(3), rowb(4), rowb(0)]
                 + [_resident(a) for a in consts_mix] + [memb, memb] + [_resident(a) for a in consts_cross],
        out_specs=rowb(0),
        out_shape=jax.ShapeDtypeStruct((t, d), F32),
        scratch_shapes=[pltpu.VMEM((tm, d), F32)],
        compiler_params=_cparams(("parallel",)),
        name="mix_cross_ln1_ln2",
    )(o_da, y_rw, bonus, g_out, rest, rest, x, *consts_mix, ck, cv, *consts_cross)


def _mem_kv_kernel(mem_ref, g_ref, b_ref, w_ref, ck_ref, cv_ref):
    m = _layer_norm_rows(mem_ref[0], g_ref[...], b_ref[...])
    kv = jnp.dot(m.astype(BF16), w_ref[...], preferred_element_type=F32)
    ck_ref[0] = kv[:, :D_MODEL].astype(BF16)
    cv_ref[0] = kv[:, D_MODEL:].astype(BF16)


def _mem_kv(mem, g, b, w_ckv):
    bsz, m, d = mem.shape
    full = lambda a: pl.BlockSpec(a.shape, lambda i: (0,) * a.ndim)
    blk = pl.BlockSpec((1, m, d), lambda i: (i, 0, 0))
    return pl.pallas_call(
        _mem_kv_kernel,
        grid=(bsz,),
        in_specs=[blk, full(g), full(b), full(w_ckv)],
        out_specs=[blk, blk],
        out_shape=[jax.ShapeDtypeStruct((bsz, m, d), BF16)] * 2,
        compiler_params=_cparams(("parallel",)),
        name="mem_kv",
    )(mem, g, b, w_ckv)


def _cross_kernel(x_ref, ck_ref, cv_ref, wq_ref, wo_ref, lng_ref, lnb_ref, o_ref, *, alpha):
    x = x_ref[...]
    cq = jnp.dot(x.astype(BF16), wq_ref[...], preferred_element_type=F32)
    cq = (cq * (CA_HD ** -0.5)).astype(BF16)
    outs = []
    for h in range(CA_HEADS):
        sl = slice(h * CA_HD, (h + 1) * CA_HD)
        s = lax.dot_general(cq[:, sl], ck_ref[0, :, sl], NT_DIMS, preferred_element_type=F32)
        s = s - jnp.max(s, axis=-1, keepdims=True)
        p = jnp.exp(s)
        p = p / jnp.sum(p, axis=-1, keepdims=True)
        outs.append(jnp.dot(p.astype(BF16), cv_ref[0, :, sl], preferred_element_type=F32))
    co = jnp.dot(jnp.concatenate(outs, axis=1).astype(BF16), wo_ref[...], preferred_element_type=F32)
    o_ref[...] = _layer_norm_rows(alpha * x + co, lng_ref[...], lnb_ref[...])


def _ffn_kernel(x_ref, win_ref, wout_ref, lng_ref, lnb_ref, o_ref, *, alpha, tf):
    x = x_ref[...]
    xb = x.astype(BF16)
    acc = alpha * x
    for j in range(D_FF // tf):
        gate = jnp.dot(xb, win_ref[:, j * tf:(j + 1) * tf], preferred_element_type=F32)
        up = jnp.dot(xb, win_ref[:, D_FF + j * tf:D_FF + (j + 1) * tf], preferred_element_type=F32)
        h = (gate * jax.nn.sigmoid(gate) * up).astype(BF16)
        acc = acc + jnp.dot(h, wout_ref[j * tf:(j + 1) * tf, :], preferred_element_type=F32)
    o_ref[...] = _layer_norm_rows(acc, lng_ref[...], lnb_ref[...])


def _ffn(x, w_in, w_out, ln_g, ln_b, alpha, tm=512, tf=1408):
    t, d = x.shape
    rowb = pl.BlockSpec((tm, d), lambda i: (i, 0))
    return pl.pallas_call(
        functools.partial(_ffn_kernel, alpha=alpha, tf=tf),
        grid=(t // tm,),
        in_specs=[rowb, _resident(w_in), _resident(w_out), _resident(ln_g), _resident(ln_b)],
        out_specs=rowb,
        out_shape=jax.ShapeDtypeStruct((t, d), F32),
        compiler_params=_cparams(("parallel",)),
        name="swiglu_ffn_ln3",
    )(x, w_in, w_out, ln_g, ln_b)


def _pad_cols(a, width):
    return jnp.pad(a, ((0, 0), (0, width - a.shape[1])))


def _pad_rows(a, height):
    return jnp.pad(a, ((0, height - a.shape[0]), (0, 0)))


def kernel(x, mem, rel_bias, w_in, shift_mu, lambda_q1, lambda_k1, lambda_q2, lambda_k2, da_subln_g, w_da_proj, rw_w0, rw_w2, rw_a0, rw_a2, rw_g2, rw_k_k, rw_k_a, rw_r_k, rw_lnx_g, rw_lnx_b, w_rw_proj, w_mix_out, ln1_g, ln1_b, mem_ln_g, mem_ln_b, w_cq, w_ckv, w_co, ln2_g, ln2_b, w_ffn_in, w_ffn_out, ln3_g, ln3_b):
    bsz, seq, d = x.shape
    depth = w_in.shape[0]
    alpha = (2.0 * depth) ** 0.25
    t = bsz * seq
    row = lambda a: a.reshape(1, -1).astype(F32)
    bias_tiles = _attn_flat_bias(rel_bias)
    xf = x.reshape(t, d)

    qkv_w = 3 * D_MODEL
    rw0 = qkv_w
    lo0 = rw0 + 3 * D_MODEL
    gate0 = lo0 + DECAY_LORA + AAA_LORA + GATE_LORA

    for l in range(depth):
        w = w_in[l]
        w_qt = w[:, :D_MODEL].T.astype(BF16)
        w_k = w[:, D_MODEL:2 * D_MODEL].astype(BF16)
        w_vt = w[:, 2 * D_MODEL:qkv_w].T.astype(BF16)
        w_rest = jnp.concatenate([
            w[:, rw0:lo0], w[:, gate0:gate0 + 2 * D_MODEL],
            _pad_cols(w[:, lo0:lo0 + DECAY_LORA], 128),
            _pad_cols(w[:, lo0 + DECAY_LORA:lo0 + DECAY_LORA + AAA_LORA], 128),
            _pad_cols(w[:, lo0 + DECAY_LORA + AAA_LORA:gate0], 256)], axis=1).astype(BF16)
        mu = shift_mu[l].astype(F32)
        mu_rkv = mu[:3 * D_MODEL].reshape(3, D_MODEL)
        o1 = 3 * D_MODEL
        mu_lo = jnp.concatenate([
            _pad_cols(mu[None, o1:o1 + DECAY_LORA], 128),
            _pad_cols(mu[None, o1 + DECAY_LORA:o1 + DECAY_LORA + AAA_LORA], 128),
            _pad_cols(mu[None, o1 + DECAY_LORA + AAA_LORA:], 256)], axis=1)
        w2 = _pad_rows(rw_w2[l], 128).astype(BF16)
        a2 = _pad_rows(rw_a2[l], 128).astype(BF16)
        g2 = _pad_rows(rw_g2[l], 256).astype(BF16)

        qt, kda, vt = _qkv_proj(xf, w_qt, w_k, w_vt, bsz, seq)
        rest = _matmul(xf, w_rest, F32, 512, 1408, "in_proj_rest")

        lam_init = 0.8 - 0.6 * math.exp(-0.3 * l)
        lamv = jnp.stack([lambda_q1[l], lambda_k1[l], lambda_q2[l], lambda_k2[l]]).astype(F32)
        o_da = _diff_attention_flat(qt, kda.reshape(bsz, seq, d), vt, lamv, bias_tiles, row(da_subln_g[l]),
                                    lam_init)

        y_rw, bon, g_out = _rw_chunks(
            rest.reshape(bsz, seq, rest.shape[1]), mu_rkv, mu_lo, row(rw_w0[l]), w2, row(rw_a0[l]),
            a2, g2, row(rw_k_k[l]), row(rw_k_a[l]), row(rw_r_k[l]))

        ck, cv = _mem_kv(mem, row(mem_ln_g[l]), row(mem_ln_b[l]), w_ckv[l].astype(BF16))
        x2 = _mix_cross(o_da.reshape(t, d), y_rw.reshape(t, d), bon.reshape(t, d), g_out.reshape(t, d), rest, xf,
                        w_da_proj[l].astype(BF16), w_rw_proj[l].astype(BF16), w_mix_out[l].astype(BF16),
                        row(rw_lnx_g[l]), row(rw_lnx_b[l]), row(ln1_g[l]), row(ln1_b[l]),
                        ck, cv, w_cq[l].astype(BF16), w_co[l].astype(BF16), row(ln2_g[l]), row(ln2_b[l]),
                        seq, alpha)

        xf = _ffn(x2, w_ffn_in[l].astype(BF16), w_ffn_out[l].astype(BF16),
                  row(ln3_g[l]), row(ln3_b[l]), alpha)
    return xf.reshape(bsz, seq, d)
```
